```python
import math
import jax
import jax.numpy as jnp
from jax import lax
import numpy as np

D_MODEL = 4096
BATCH = 4
SEQ = 4096
DEPTH = 2

CTX_LEN = 256
GRID_W = 64
N_MOD = 6
NORM_EPS = 1e-6

POOL_WINDOWS = (2, 4, 8, 16)
POOL_GROUPS = 4
POOL_GROUP_DIM = D_MODEL // 16
POOL_DIM = POOL_GROUPS * POOL_GROUP_DIM

GDN_HEAD_DIM = 128
GDN_DIM = D_MODEL - POOL_DIM
GDN_HEADS = GDN_DIM // GDN_HEAD_DIM
GDN_CONV = 5
GDN_CHUNK = 64

FOURIER_GROUPS = 4
FOURIER_GROUP_DIM = D_MODEL // 16
FOURIER_DIM = FOURIER_GROUPS * FOURIER_GROUP_DIM

GLA_HEADS = 6
GLA_V_DIM = D_MODEL - FOURIER_DIM
GLA_K_DIM = GLA_V_DIM // 2
GLA_HEAD_K = GLA_K_DIM // GLA_HEADS
GLA_HEAD_V = GLA_V_DIM // GLA_HEADS
GLA_GATE_RANK = 16
GLA_GATE_TEMP = 16.0
GLA_CHUNK = 32

MOE_GROUPS = 4
MOE_EXPERTS_PER_GROUP = 8
MOE_EXPERTS = MOE_GROUPS * MOE_EXPERTS_PER_GROUP
MOE_TOP_K = 2
MOE_HIDDEN = D_MODEL // 8
MOE_BLOCK = 128

AB_SPLITS = (POOL_DIM, POOL_DIM + 3 * GDN_DIM, POOL_DIM + 4 * GDN_DIM, POOL_DIM + 4 * GDN_DIM + 2 * GDN_HEADS)
AB_IN = POOL_DIM + 4 * GDN_DIM + 4 * GDN_HEADS
CD_SPLITS = (FOURIER_DIM, FOURIER_DIM + GLA_K_DIM, FOURIER_DIM + 2 * GLA_K_DIM,
             FOURIER_DIM + 2 * GLA_K_DIM + GLA_V_DIM, FOURIER_DIM + 2 * GLA_K_DIM + 2 * GLA_V_DIM)
CD_IN = FOURIER_DIM + 2 * GLA_K_DIM + 2 * GLA_V_DIM + 2 * GLA_GATE_RANK

kernel_name = 'hybrid_pool_gdn_fourier_gla_hmoe_dit'


def rms_norm(x, w):
    x32 = x.astype(jnp.float32)
    y = x32 * lax.rsqrt(jnp.mean(x32 * x32, axis=-1, keepdims=True) + NORM_EPS)
    return (y * w.astype(jnp.float32)).astype(x.dtype)


def modulate(h, shift, scale):
    return h * (1 + scale) + shift


def l2_normalise(t):
    t32 = t.astype(jnp.float32)
    return t32 * lax.rsqrt(jnp.sum(t32 * t32, axis=-1, keepdims=True) + NORM_EPS)


def split_heads(t, n_heads):
    b, n, _ = t.shape
    return t.reshape(b, n, n_heads, -1).transpose(0, 2, 1, 3)


def gated_head_norm(o, z, w):
    b, h, n, dv = o.shape
    y = rms_norm(o.transpose(0, 2, 1, 3), w) * jax.nn.silu(z.reshape(b, n, h, dv).astype(jnp.float32))
    return y.reshape(b, n, h * dv).astype(z.dtype)


def centred_conv(u, w):
    k = w.shape[0]
    pad = k // 2
    n = u.shape[1]
    up = jnp.pad(u, ((0, 0), (pad, pad), (0, 0)))
    return sum(up[:, j:j + n] * w[j] for j in range(k))


def pool_mix(u, row_len, pool_w, pool_scale):
    b, n, _ = u.shape
    rows = n // row_len
    u5 = u.reshape(b, rows, row_len, POOL_GROUPS, POOL_GROUP_DIM).astype(jnp.float32)
    cs = jnp.pad(jnp.cumsum(u5, axis=2), ((0, 0), (0, 0), (1, 0), (0, 0), (0, 0)))
    pos = jnp.arange(row_len)[:, None]
    half = jnp.array(POOL_WINDOWS, dtype=jnp.int32)[None, :] // 2
    lo = jnp.clip(pos - half, 0, row_len - 1)
    hi = jnp.clip(pos + half - 1, 0, row_len - 1)
    g_idx = jnp.arange(POOL_GROUPS)[None, :]
    win_sum = cs[:, :, hi + 1, g_idx] - cs[:, :, lo, g_idx]
    count = (hi - lo + 1).astype(jnp.float32)[None, None, :, :, None]
    d = (win_sum / count - u5).astype(u.dtype)
    y = jnp.einsum('brwgc,gcd->brwgd', d, pool_w) * pool_scale.reshape(POOL_GROUPS, POOL_GROUP_DIM)
    return y.reshape(b, n, POOL_DIM)


def fourier_mix(u, w):
    b, n, _ = u.shape
    u4 = u.astype(jnp.float32).reshape(b, n, FOURIER_GROUPS, FOURIER_GROUP_DIM)
    f = jnp.fft.fft2(u4, axes=(1, 3), norm='ortho').real.astype(u.dtype)
    return jnp.einsum('bngc,gcd->bngd', f, w).reshape(b, n, FOURIER_DIM)


def gdn_chunked(q, k, v, beta, g, state0):
    b, h, n, dk = q.shape
    dv = v.shape[-1]
    c = GDN_CHUNK
    nc = n // c
    q, k, v = (t.astype(jnp.float32).reshape(b, h, nc, c, -1) for t in (q, k, v))
    beta = beta.astype(jnp.float32).reshape(b, h, nc, c)
    g = jnp.cumsum(g.astype(jnp.float32).reshape(b, h, nc, c), axis=-1)
    tril = jnp.tril(jnp.ones((c, c), dtype=bool))
    strict = jnp.tril(jnp.ones((c, c), dtype=bool), -1)
    decay = jnp.exp(jnp.where(tril, g[..., :, None] - g[..., None, :], -jnp.inf))
    k_beta = k * beta[..., None]
    lower = jnp.where(strict, jnp.einsum('bhncd,bhnsd->bhncs', k_beta, k) * decay, 0.0)
    rhs = jnp.concatenate([v * beta[..., None], k_beta * jnp.exp(g)[..., None]], axis=-1)
    sol = lax.linalg.triangular_solve(jnp.eye(c, dtype=jnp.float32) + lower, rhs,
                                      left_side=True, lower=True, unit_diagonal=True)
    u, w = sol[..., :dv], sol[..., dv:]
    attn = jnp.einsum('bhncd,bhnsd->bhncs', q, k) * decay
    q_dec = q * jnp.exp(g)[..., None]
    k_end = k * jnp.exp(g[..., -1:] - g)[..., None]
    g_end = jnp.exp(g[..., -1])

    def step(state, xs):
        u_i, w_i, attn_i, q_i, k_i, ge_i = xs
        v_new = u_i - jnp.einsum('bhck,bhkv->bhcv', w_i, state)
        o = jnp.einsum('bhck,bhkv->bhcv', q_i, state) + jnp.einsum('bhcs,bhsv->bhcv', attn_i, v_new)
        state = state * ge_i[..., None, None] + jnp.einsum('bhck,bhcv->bhkv', k_i, v_new)
        return state, o

    xs = tuple(jnp.moveaxis(t, 2, 0) for t in (u, w, attn, q_dec, k_end, g_end))
    state, o = lax.scan(step, state0, xs)
    return jnp.moveaxis(o, 0, 2).reshape(b, h, n, dv), state


def gla_chunked(q, k, v, gk, state0):
    b, h, n, dk = q.shape
    dv = v.shape[-1]
    c = GLA_CHUNK
    nc = n // c
    q, k, v, gk = (t.astype(jnp.float32).reshape(b, h, nc, c, -1) for t in (q, k, v, gk))
    gcum = jnp.cumsum(gk, axis=3)
    q_dec = q * jnp.exp(gcum)
    k_inv = k * jnp.exp(-gcum)
    tril = jnp.tril(jnp.ones((c, c), dtype=bool))
    attn = jnp.where(tril, jnp.einsum('bhnck,bhnsk->bhncs', q_dec, k_inv), 0.0)
    o_intra = jnp.einsum('bhncs,bhnsv->bhncv', attn, v)
    k_end = k * jnp.exp(gcum[..., -1:, :] - gcum)
    g_end = jnp.exp(gcum[..., -1, :])

    def step(state, xs):
        q_i, k_i, v_i, ge_i = xs
        o = jnp.einsum('bhck,bhkv->bhcv', q_i, state)
        state = state * ge_i[..., None] + jnp.einsum('bhck,bhcv->bhkv', k_i, v_i)
        return state, o

    xs = tuple(jnp.moveaxis(t, 2, 0) for t in (q_dec, k_end, v, g_end))
    state, o_inter = lax.scan(step, state0, xs)
    o = o_intra + jnp.moveaxis(o_inter, 0, 2)
    return o.reshape(b, h, n, dv), state


def orient(t, reverse):
    return jnp.flip(t, axis=2) if reverse else t


def bidirectional_scan(chunk_fn, ctx_dirs, lat_dirs, state0):
    o_ctx, o_lat = [], []
    for d in range(2):
        rev = d == 1
        oc, s_ctx = chunk_fn(*[orient(t, rev) for t in ctx_dirs[d]], state0)
        ol, _ = chunk_fn(*[orient(t, rev) for t in lat_dirs[d]], s_ctx)
        o_ctx.append(orient(oc, rev))
        o_lat.append(orient(ol, rev))
    return o_ctx[0] + o_ctx[1], o_lat[0] + o_lat[1]


def mixer_pool_gdn(h_ctx, h_lat, w_in, pool_w, pool_scale, conv_w, a_log, dt_bias, norm_w, w_out, want_ctx):
    n_ctx = h_ctx.shape[1]
    proj = jnp.concatenate([h_ctx, h_lat], axis=1) @ w_in
    sides = []
    for p in (proj[:, :n_ctx], proj[:, n_ctx:]):
        b, n, _ = p.shape
        pool_in, qkv, z, beta_raw, alpha_raw = jnp.split(p, AB_SPLITS, axis=-1)
        q, k, v = jnp.split(jax.nn.silu(centred_conv(qkv, conv_w)), 3, axis=-1)
        q = l2_normalise(split_heads(q, GDN_HEADS)) * (GDN_HEAD_DIM ** -0.5)
        k = l2_normalise(split_heads(k, GDN_HEADS))
        v = split_heads(v, GDN_HEADS)
        beta = jax.nn.sigmoid(beta_raw.astype(jnp.float32)).reshape(b, n, 2, GDN_HEADS).transpose(2, 0, 3, 1)
        alpha = alpha_raw.astype(jnp.float32).reshape(b, n, 2, GDN_HEADS)
        g = (-jnp.exp(a_log.astype(jnp.float32)) * jax.nn.softplus(alpha + dt_bias.astype(jnp.float32))).transpose(2, 0, 3, 1)
        sides.append((pool_in, z, [(q, k, v, beta[d], g[d]) for d in range(2)]))
    (pool_c, z_c, dirs_c), (pool_l, z_l, dirs_l) = sides
    state0 = jnp.zeros((h_lat.shape[0], GDN_HEADS, GDN_HEAD_DIM, GDN_HEAD_DIM), jnp.float32)
    o_c, o_l = bidirectional_scan(gdn_chunked, dirs_c, dirs_l, state0)
    out_l = jnp.concatenate([pool_mix(pool_l, GRID_W, pool_w, pool_scale),
                             gated_head_norm(o_l, z_l, norm_w)], axis=-1) @ w_out
    out_c = None
    if want_ctx:
        out_c = jnp.concatenate([pool_mix(pool_c, n_ctx, pool_w, pool_scale),
                                 gated_head_norm(o_c, z_c, norm_w)], axis=-1) @ w_out
    return out_c, out_l


def mixer_fourier_gla(h_ctx, h_lat, w_in, fourier_w, gate_up, gate_b, norm_w, w_out, want_ctx):
    n_ctx = h_ctx.shape[1]
    proj = jnp.concatenate([h_ctx, h_lat], axis=1) @ w_in
    sides = []
    for p in (proj[:, :n_ctx], proj[:, n_ctx:]):
        b, n, _ = p.shape
        f_in, q, k, v, zg, lr = jnp.split(p, CD_SPLITS, axis=-1)
        q = split_heads(q, GLA_HEADS) * (GLA_HEAD_K ** -0.5)
        k = split_heads(k, GLA_HEADS)
        v = split_heads(v, GLA_HEADS)
        lr = lr.reshape(b, n, 2, GLA_GATE_RANK)
        gk = jax.nn.log_sigmoid((jnp.einsum('bndr,drk->dbnk', lr, gate_up)
                                 + gate_b[:, None, None, :]).astype(jnp.float32)) / GLA_GATE_TEMP
        sides.append((f_in, zg, [(q, k, v, split_heads(gk[d], GLA_HEADS)) for d in range(2)]))
    (f_c, z_c, dirs_c), (f_l, z_l, dirs_l) = sides
    state0 = jnp.zeros((h_lat.shape[0], GLA_HEADS, GLA_HEAD_K, GLA_HEAD_V), jnp.float32)
    o_c, o_l = bidirectional_scan(gla_chunked, dirs_c, dirs_l, state0)
    out_l = jnp.concatenate([fourier_mix(f_l, fourier_w), gated_head_norm(o_l, z_l, norm_w)], axis=-1) @ w_out
    out_c = None
    if want_ctx:
        out_c = jnp.concatenate([fourier_mix(f_c, fourier_w), gated_head_norm(o_c, z_c, norm_w)], axis=-1) @ w_out
    return out_c, out_l


def hier_moe(h, group_w, group_b, expert_w, expert_b, w1, w3, w2):
    t, d = h.shape
    rows = jnp.arange(t)
    group_logits = (h @ group_w + group_b).astype(jnp.float32)
    group = jnp.argmax(group_logits, axis=-1)
    p_group = jax.nn.softmax(group_logits, axis=-1)[rows, group][:, None]
    expert_logits = jnp.einsum('td,gde->tge', h, expert_w) + expert_b
    sel = expert_logits[rows, group].astype(jnp.float32)
    top_val, top_idx = lax.top_k(sel, MOE_TOP_K)
    weight = (p_group * jax.nn.softmax(top_val, axis=-1)).reshape(-1).astype(h.dtype)
    expert = (group[:, None] * MOE_EXPERTS_PER_GROUP + top_idx).reshape(-1)
    token = jnp.repeat(rows, MOE_TOP_K)
    tk = t * MOE_TOP_K
    order = jnp.argsort(expert)
    e_s, tok_s, w_s = expert[order], token[order], weight[order]
    counts = jnp.bincount(expert, length=MOE_EXPERTS)
    start = jnp.cumsum(counts) - counts
    padded = (counts + MOE_BLOCK - 1) // MOE_BLOCK * MOE_BLOCK
    pad_end = jnp.cumsum(padded)
    pad_start = pad_end - padded
    dest = pad_start[e_s] + jnp.arange(tk) - start[e_s]
    n_blocks = -(-tk // MOE_BLOCK) + MOE_EXPERTS
    xbuf = jnp.zeros((n_blocks * MOE_BLOCK, d), h.dtype).at[dest].set(h[tok_s])
    block_expert = jnp.minimum(jnp.searchsorted(pad_end, jnp.arange(n_blocks) * MOE_BLOCK, side='right'),
                               MOE_EXPERTS - 1)

    def expert_block(args):
        xb, e = args
        return (jax.nn.silu(xb @ w1[e]) * (xb @ w3[e])) @ w2[e]

    ybuf = lax.map(expert_block, (xbuf.reshape(n_blocks, MOE_BLOCK, d), block_expert)).reshape(-1, d)
    return jnp.zeros_like(h).at[tok_s].add(ybuf[dest] * w_s[:, None])


def setup_inputs(seed: int = 0) -> dict:
    key = jax.random.key(seed)
    keys = iter(jax.random.split(key, 40))

    def normal(shape, scale):
        return scale * jax.random.normal(next(keys), shape, jnp.float32)

    def gain(shape):
        return 1.0 + 0.05 * jax.random.normal(next(keys), shape, jnp.float32)

    ne, no = (DEPTH + 1) // 2, DEPTH // 2
    d = D_MODEL
    a_log = jnp.log(jax.random.uniform(next(keys), (ne, 2, GDN_HEADS), jnp.float32, 1.0, 16.0))
    dt = jnp.exp(jax.random.uniform(next(keys), (ne, 2, GDN_HEADS), jnp.float32, math.log(1e-3), math.log(1e-1)))
    dt_bias = dt + jnp.log(-jnp.expm1(-dt))
    return {
        'x': normal((BATCH, SEQ, d), 1.0),
        'c': normal((BATCH, d), 1.0),
        'ctx': normal((BATCH, CTX_LEN, d), 1.0),
        'c_ctx': normal((d,), 1.0),
        'mod_w': normal((DEPTH, d, N_MOD * d), 0.5 * d ** -0.5),
        'mod_b': normal((DEPTH, N_MOD * d), 0.02),
        'norm1_w': gain((DEPTH, d)),
        'norm2_w': gain((DEPTH, d)),
        'ab_w_in': normal((ne, d, AB_IN), d ** -0.5),
        'pool_w': normal((ne, POOL_GROUPS, POOL_GROUP_DIM, POOL_GROUP_DIM), POOL_GROUP_DIM ** -0.5),
        'pool_scale': gain((ne, POOL_DIM)),
        'gdn_conv_w': normal((ne, GDN_CONV, 3 * GDN_DIM), GDN_CONV ** -0.5),
        'gdn_a_log': a_log,
        'gdn_dt_bias': dt_bias,
        'gdn_norm_w': gain((ne, GDN_HEAD_DIM)),
        'ab_w_out': normal((ne, d, d), d ** -0.5),
        'cd_w_in': normal((no, d, CD_IN), d ** -0.5),
        'fourier_w': normal((no, FOURIER_GROUPS, FOURIER_GROUP_DIM, FOURIER_GROUP_DIM), FOURIER_GROUP_DIM ** -0.5),
        'gla_gate_up': normal((no, 2, GLA_GATE_RANK, GLA_K_DIM), GLA_GATE_RANK ** -0.5),
        'gla_gate_b': normal((no, 2, GLA_K_DIM), 0.1),
        'gla_norm_w': gain((no, GLA_HEAD_V)),
        'cd_w_out': normal((no, d, d), d ** -0.5),
        'moe_group_w': normal((DEPTH, d, MOE_GROUPS), d ** -0.5),
        'moe_group_b': normal((DEPTH, MOE_GROUPS), 0.01),
        'moe_expert_w': normal((DEPTH, MOE_GROUPS, d, MOE_EXPERTS_PER_GROUP), d ** -0.5),
        'moe_expert_b': normal((DEPTH, MOE_GROUPS, MOE_EXPERTS_PER_GROUP), 0.01),
        'moe_w1': normal((DEPTH, MOE_EXPERTS, d, MOE_HIDDEN), d ** -0.5),
        'moe_w3': normal((DEPTH, MOE_EXPERTS, d, MOE_HIDDEN), d ** -0.5),
        'moe_w2': normal((DEPTH, MOE_EXPERTS, MOE_HIDDEN, d), MOE_HIDDEN ** -0.5),
        'final_norm_w': gain((d,)),
    }


def reference(x, c, ctx, c_ctx, mod_w, mod_b, norm1_w, norm2_w, ab_w_in, pool_w, pool_scale, gdn_conv_w,
              gdn_a_log, gdn_dt_bias, gdn_norm_w, ab_w_out, cd_w_in, fourier_w, gla_gate_up, gla_gate_b,
              gla_norm_w, cd_w_out, moe_group_w, moe_group_b, moe_expert_w, moe_expert_b, moe_w1, moe_w3,
              moe_w2, final_norm_w):
    d = x.shape[-1]
    x_lat, x_ctx = x, ctx
    for i in range(DEPTH):
        last = i == DEPTH - 1
        j = i // 2
        mod_l = (jax.nn.silu(c) @ mod_w[i] + mod_b[i])[:, None, :]
        mod_c = (jax.nn.silu(c_ctx) @ mod_w[i] + mod_b[i])[None, None, :]
        sh1_l, sc1_l, g1_l, sh2_l, sc2_l, g2_l = jnp.split(mod_l, N_MOD, axis=-1)
        sh1_c, sc1_c, g1_c, sh2_c, sc2_c, g2_c = jnp.split(mod_c, N_MOD, axis=-1)
        h_l = modulate(rms_norm(x_lat, norm1_w[i]), sh1_l, sc1_l)
        h_c = modulate(rms_norm(x_ctx, norm1_w[i]), sh1_c, sc1_c)
        if i % 2 == 0:
            m_c, m_l = mixer_pool_gdn(h_c, h_l, ab_w_in[j], pool_w[j], pool_scale[j], gdn_conv_w[j],
                                      gdn_a_log[j], gdn_dt_bias[j], gdn_norm_w[j], ab_w_out[j], not last)
        else:
            m_c, m_l = mixer_fourier_gla(h_c, h_l, cd_w_in[j], fourier_w[j], gla_gate_up[j], gla_gate_b[j],
                                         gla_norm_w[j], cd_w_out[j], not last)
        x_lat = x_lat + g1_l * m_l
        f_l = modulate(rms_norm(x_lat, norm2_w[i]), sh2_l, sc2_l).reshape(-1, d)
        moe_args = (moe_group_w[i], moe_group_b[i], moe_expert_w[i], moe_expert_b[i], moe_w1[i], moe_w3[i], moe_w2[i])
        if last:
            x_lat = x_lat + g2_l * hier_moe(f_l, *moe_args).reshape(x_lat.shape)
        else:
            x_ctx = x_ctx + g1_c * m_c
            f_c = modulate(rms_norm(x_ctx, norm2_w[i]), sh2_c, sc2_c).reshape(-1, d)
            y = hier_moe(jnp.concatenate([f_c, f_l], axis=0), *moe_args)
            n_c = f_c.shape[0]
            x_ctx = x_ctx + g2_c * y[:n_c].reshape(x_ctx.shape)
            x_lat = x_lat + g2_l * y[n_c:].reshape(x_lat.shape)
    return rms_norm(x_lat, final_norm_w)
```

```python
import functools
import math

import jax
import jax.numpy as jnp
from jax import lax
from jax.experimental import pallas as pl
from jax.experimental.pallas import tpu as pltpu

D_MODEL = 4096
DEPTH = 2
CTX_LEN = 256
GRID_W = 64
N_MOD = 6
NORM_EPS = 1e-6

POOL_WINDOWS = (2, 4, 8, 16)
POOL_GROUPS = 4
POOL_GROUP_DIM = D_MODEL // 16
POOL_DIM = POOL_GROUPS * POOL_GROUP_DIM

GDN_HEAD_DIM = 128
GDN_DIM = D_MODEL - POOL_DIM
GDN_HEADS = GDN_DIM // GDN_HEAD_DIM
GDN_CONV = 5
GDN_CHUNK = 64

FOURIER_GROUPS = 4
FOURIER_GROUP_DIM = D_MODEL // 16
FOURIER_DIM = FOURIER_GROUPS * FOURIER_GROUP_DIM

GLA_HEADS = 6
GLA_V_DIM = D_MODEL - FOURIER_DIM
GLA_K_DIM = GLA_V_DIM // 2
GLA_HEAD_K = GLA_K_DIM // GLA_HEADS
GLA_HEAD_V = GLA_V_DIM // GLA_HEADS
GLA_GATE_RANK = 16
GLA_GATE_TEMP = 16.0
GLA_CHUNK = 32

MOE_GROUPS = 4
MOE_EXPERTS_PER_GROUP = 8
MOE_EXPERTS = MOE_GROUPS * MOE_EXPERTS_PER_GROUP
MOE_TOP_K = 2
MOE_HIDDEN = D_MODEL // 8

AB_SPLITS = (POOL_DIM, POOL_DIM + 3 * GDN_DIM, POOL_DIM + 4 * GDN_DIM, POOL_DIM + 4 * GDN_DIM + 2 * GDN_HEADS)
AB_MAIN = POOL_DIM + 4 * GDN_DIM
CD_SPLITS = (FOURIER_DIM, FOURIER_DIM + GLA_K_DIM, FOURIER_DIM + 2 * GLA_K_DIM,
             FOURIER_DIM + 2 * GLA_K_DIM + GLA_V_DIM, FOURIER_DIM + 2 * GLA_K_DIM + 2 * GLA_V_DIM)
CD_MAIN = FOURIER_DIM + 2 * GLA_K_DIM + 2 * GLA_V_DIM

LANES = 128
VMEM_LIMIT = 48 * 1024 * 1024
MOE_BLOCK_ROWS = 256


def _mm_body(x_ref, w_ref, o_ref):
    o_ref[...] = jnp.dot(x_ref[...], w_ref[...], preferred_element_type=jnp.float32).astype(o_ref.dtype)


def dense_matmul(x, w, out_dtype=jnp.float32, tm=1024, tn=512):
    m, k = x.shape
    n = w.shape[1]
    tm = min(tm, m)
    tn = min(tn, n)
    assert m % tm == 0 and n % tn == 0, (m, n, tm, tn)
    return pl.pallas_call(
        _mm_body,
        out_shape=jax.ShapeDtypeStruct((m, n), out_dtype),
        grid=(m // tm, n // tn),
        in_specs=[pl.BlockSpec((tm, k), lambda i, j: (i, 0)),
                  pl.BlockSpec((k, tn), lambda i, j: (0, j))],
        out_specs=pl.BlockSpec((tm, tn), lambda i, j: (i, j)),
        compiler_params=pltpu.CompilerParams(dimension_semantics=("parallel", "parallel"),
                                             vmem_limit_bytes=VMEM_LIMIT),
        name="dense_matmul",
    )(x, w)


def _pad_cols(w, mult=LANES):
    n = w.shape[1]
    pad = (-n) % mult
    return jnp.pad(w, ((0, 0), (0, pad))) if pad else w


def _moe_body(be_ref, x_ref, w1_ref, w3_ref, w2_ref, o_ref):
    del be_ref
    x = x_ref[...]
    a = jnp.dot(x, w1_ref[0], preferred_element_type=jnp.float32)
    b = jnp.dot(x, w3_ref[0], preferred_element_type=jnp.float32)
    h = (a * jax.nn.sigmoid(a) * b).astype(jnp.bfloat16)
    o_ref[...] = jnp.dot(h, w2_ref[0], preferred_element_type=jnp.float32)


def moe_expert_blocks(xbuf, block_expert, w1, w3, w2):
    p, d = xbuf.shape
    nb = p // MOE_BLOCK_ROWS
    hid = w1.shape[-1]
    grid_spec = pltpu.PrefetchScalarGridSpec(
        num_scalar_prefetch=1,
        grid=(nb,),
        in_specs=[pl.BlockSpec((MOE_BLOCK_ROWS, d), lambda i, be: (i, 0)),
                  pl.BlockSpec((1, d, hid), lambda i, be: (be[i], 0, 0)),
                  pl.BlockSpec((1, d, hid), lambda i, be: (be[i], 0, 0)),
                  pl.BlockSpec((1, hid, d), lambda i, be: (be[i], 0, 0))],
        out_specs=pl.BlockSpec((MOE_BLOCK_ROWS, d), lambda i, be: (i, 0)),
    )
    return pl.pallas_call(
        _moe_body,
        out_shape=jax.ShapeDtypeStruct((p, d), jnp.float32),
        grid_spec=grid_spec,
        compiler_params=pltpu.CompilerParams(dimension_semantics=("arbitrary",), vmem_limit_bytes=VMEM_LIMIT),
        name="moe_expert_blocks",
    )(block_expert, xbuf, w1, w3, w2)


def hier_moe(h, group_w, group_b, expert_w, expert_b, w1, w3, w2):
    t, d = h.shape
    rows = jnp.arange(t)
    router_w = jnp.concatenate([group_w, expert_w.transpose(1, 0, 2).reshape(d, MOE_EXPERTS)], axis=1)
    logits = jnp.dot(h, router_w, precision=lax.Precision.HIGHEST)
    group_logits = logits[:, :MOE_GROUPS] + group_b
    group = jnp.argmax(group_logits, axis=-1)
    p_group = jax.nn.softmax(group_logits, axis=-1)[rows, group][:, None]
    expert_logits = logits[:, MOE_GROUPS:].reshape(t, MOE_GROUPS, MOE_EXPERTS_PER_GROUP) + expert_b
    sel = expert_logits[rows, group]
    top_val, top_idx = lax.top_k(sel, MOE_TOP_K)
    weight = p_group * jax.nn.softmax(top_val, axis=-1)
    expert = (group[:, None] * MOE_EXPERTS_PER_GROUP + top_idx).reshape(-1)
    tk = t * MOE_TOP_K
    onehot = (expert[:, None] == jnp.arange(MOE_EXPERTS)[None, :]).astype(jnp.int32)
    counts = jnp.sum(onehot, axis=0)
    rank = jnp.sum((jnp.cumsum(onehot, axis=0) - onehot) * onehot, axis=1)
    bm = MOE_BLOCK_ROWS
    padded = (counts + bm - 1) // bm * bm
    pad_end = jnp.cumsum(padded)
    pad_start = pad_end - padded
    dest = pad_start[expert] + rank
    n_blocks = -(-tk // bm) + MOE_EXPERTS
    src = jnp.zeros((n_blocks * bm,), jnp.int32).at[dest].set(jnp.repeat(rows, MOE_TOP_K).astype(jnp.int32))
    block_expert = jnp.minimum(jnp.searchsorted(pad_end, jnp.arange(n_blocks) * bm, side='right'),
                               MOE_EXPERTS - 1).astype(jnp.int32)
    xbuf = h.astype(jnp.bfloat16)[src]
    ybuf = moe_expert_blocks(xbuf, block_expert, w1, w3, w2)
    y2 = ybuf[dest].reshape(t, MOE_TOP_K, d)
    return y2[:, 0] * weight[:, 0:1] + y2[:, 1] * weight[:, 1:2]


def rms_norm(x, w):
    x32 = x.astype(jnp.float32)
    y = x32 * lax.rsqrt(jnp.mean(x32 * x32, axis=-1, keepdims=True) + NORM_EPS)
    return (y * w.astype(jnp.float32)).astype(x.dtype)


def modulate(h, shift, scale):
    return h * (1 + scale) + shift


def l2_normalise(t):
    t32 = t.astype(jnp.float32)
    return t32 * lax.rsqrt(jnp.sum(t32 * t32, axis=-1, keepdims=True) + NORM_EPS)


def split_heads(t, n_heads):
    b, n, _ = t.shape
    return t.reshape(b, n, n_heads, -1).transpose(0, 2, 1, 3)


def gated_head_norm(o, z, w):
    b, h, n, dv = o.shape
    y = rms_norm(o.transpose(0, 2, 1, 3), w) * jax.nn.silu(z.reshape(b, n, h, dv).astype(jnp.float32))
    return y.reshape(b, n, h * dv).astype(z.dtype)


def centred_conv(u, w):
    k = w.shape[0]
    pad = k // 2
    n = u.shape[1]
    up = jnp.pad(u, ((0, 0), (pad, pad), (0, 0)))
    return sum(up[:, j:j + n] * w[j] for j in range(k))


def pool_mix(u, row_len, pool_w, pool_scale):
    b, n, _ = u.shape
    rows = n // row_len
    u5 = u.reshape(b, rows, row_len, POOL_GROUPS, POOL_GROUP_DIM).astype(jnp.float32)
    cs = jnp.pad(jnp.cumsum(u5, axis=2), ((0, 0), (0, 0), (1, 0), (0, 0), (0, 0)))
    pos = jnp.arange(row_len)[:, None]
    half = jnp.array(POOL_WINDOWS, dtype=jnp.int32)[None, :] // 2
    lo = jnp.clip(pos - half, 0, row_len - 1)
    hi = jnp.clip(pos + half - 1, 0, row_len - 1)
    g_idx = jnp.arange(POOL_GROUPS)[None, :]
    win_sum = cs[:, :, hi + 1, g_idx] - cs[:, :, lo, g_idx]
    count = (hi - lo + 1).astype(jnp.float32)[None, None, :, :, None]
    d = (win_sum / count - u5).astype(u.dtype)
    y = jnp.einsum('brwgc,gcd->brwgd', d, pool_w) * pool_scale.reshape(POOL_GROUPS, POOL_GROUP_DIM)
    return y.reshape(b, n, POOL_DIM)


def fourier_mix(u, w):
    b, n, _ = u.shape
    u4 = u.astype(jnp.float32).reshape(b, n, FOURIER_GROUPS, FOURIER_GROUP_DIM)
    f = jnp.fft.fft2(u4, axes=(1, 3), norm='ortho').real.astype(u.dtype)
    return jnp.einsum('bngc,gcd->bngd', f, w).reshape(b, n, FOURIER_DIM)


def gdn_chunked(q, k, v, beta, g, state0):
    b, h, n, dk = q.shape
    dv = v.shape[-1]
    c = GDN_CHUNK
    nc = n // c
    q, k, v = (t.astype(jnp.float32).reshape(b, h, nc, c, -1) for t in (q, k, v))
    beta = beta.astype(jnp.float32).reshape(b, h, nc, c)
    g = jnp.cumsum(g.astype(jnp.float32).reshape(b, h, nc, c), axis=-1)
    tril = jnp.tril(jnp.ones((c, c), dtype=bool))
    strict = jnp.tril(jnp.ones((c, c), dtype=bool), -1)
    decay = jnp.exp(jnp.where(tril, g[..., :, None] - g[..., None, :], -jnp.inf))
    k_beta = k * beta[..., None]
    lower = jnp.where(strict, jnp.einsum('bhncd,bhnsd->bhncs', k_beta, k) * decay, 0.0)
    rhs = jnp.concatenate([v * beta[..., None], k_beta * jnp.exp(g)[..., None]], axis=-1)
    sol = lax.linalg.triangular_solve(jnp.eye(c, dtype=jnp.float32) + lower, rhs,
                                      left_side=True, lower=True, unit_diagonal=True)
    u, w = sol[..., :dv], sol[..., dv:]
    attn = jnp.einsum('bhncd,bhnsd->bhncs', q, k) * decay
    q_dec = q * jnp.exp(g)[..., None]
    k_end = k * jnp.exp(g[..., -1:] - g)[..., None]
    g_end = jnp.exp(g[..., -1])

    def step(state, xs):
        u_i, w_i, attn_i, q_i, k_i, ge_i = xs
        v_new = u_i - jnp.einsum('bhck,bhkv->bhcv', w_i, state)
        o = jnp.einsum('bhck,bhkv->bhcv', q_i, state) + jnp.einsum('bhcs,bhsv->bhcv', attn_i, v_new)
        state = state * ge_i[..., None, None] + jnp.einsum('bhck,bhcv->bhkv', k_i, v_new)
        return state, o

    xs = tuple(jnp.moveaxis(t, 2, 0) for t in (u, w, attn, q_dec, k_end, g_end))
    state, o = lax.scan(step, state0, xs)
    return jnp.moveaxis(o, 0, 2).reshape(b, h, n, dv), state


def gla_chunked(q, k, v, gk, state0):
    b, h, n, dk = q.shape
    dv = v.shape[-1]
    c = GLA_CHUNK
    nc = n // c
    q, k, v, gk = (t.astype(jnp.float32).reshape(b, h, nc, c, -1) for t in (q, k, v, gk))
    gcum = jnp.cumsum(gk, axis=3)
    q_dec = q * jnp.exp(gcum)
    k_inv = k * jnp.exp(-gcum)
    tril = jnp.tril(jnp.ones((c, c), dtype=bool))
    attn = jnp.where(tril, jnp.einsum('bhnck,bhnsk->bhncs', q_dec, k_inv), 0.0)
    o_intra = jnp.einsum('bhncs,bhnsv->bhncv', attn, v)
    k_end = k * jnp.exp(gcum[..., -1:, :] - gcum)
    g_end = jnp.exp(gcum[..., -1, :])

    def step(state, xs):
        q_i, k_i, v_i, ge_i = xs
        o = jnp.einsum('bhck,bhkv->bhcv', q_i, state)
        state = state * ge_i[..., None] + jnp.einsum('bhck,bhcv->bhkv', k_i, v_i)
        return state, o

    xs = tuple(jnp.moveaxis(t, 2, 0) for t in (q_dec, k_end, v, g_end))
    state, o_inter = lax.scan(step, state0, xs)
    o = o_intra + jnp.moveaxis(o_inter, 0, 2)
    return o.reshape(b, h, n, dv), state


def orient(t, reverse):
    return jnp.flip(t, axis=2) if reverse else t


def bidirectional_scan(chunk_fn, ctx_dirs, lat_dirs, state0):
    o_ctx, o_lat = [], []
    for d in range(2):
        rev = d == 1
        oc, s_ctx = chunk_fn(*[orient(t, rev) for t in ctx_dirs[d]], state0)
        ol, _ = chunk_fn(*[orient(t, rev) for t in lat_dirs[d]], s_ctx)
        o_ctx.append(orient(oc, rev))
        o_lat.append(orient(ol, rev))
    return o_ctx[0] + o_ctx[1], o_lat[0] + o_lat[1]


def in_projection(h_ctx, h_lat, w_in, n_main):
    b = h_lat.shape[0]
    hcat = jnp.concatenate([h_ctx, h_lat], axis=1)
    n_tok = hcat.shape[1]
    x = hcat.reshape(b * n_tok, D_MODEL).astype(jnp.bfloat16)
    wb = w_in.astype(jnp.bfloat16)
    main = dense_matmul(x, wb[:, :n_main])
    n_tail = w_in.shape[1] - n_main
    tail = dense_matmul(x, _pad_cols(wb[:, n_main:]))[:, :n_tail]
    proj = jnp.concatenate([main, tail], axis=1)
    return proj.reshape(b, n_tok, -1)


def out_projection(y, w_out):
    b, n, d = y.shape
    out = dense_matmul(y.reshape(b * n, d).astype(jnp.bfloat16), w_out.astype(jnp.bfloat16))
    return out.reshape(b, n, -1)


def mixer_pool_gdn(h_ctx, h_lat, w_in, pool_w, pool_scale, conv_w, a_log, dt_bias, norm_w, w_out, want_ctx):
    n_ctx = h_ctx.shape[1]
    proj = in_projection(h_ctx, h_lat, w_in, AB_MAIN)
    sides = []
    for p in (proj[:, :n_ctx], proj[:, n_ctx:]):
        b, n, _ = p.shape
        pool_in, qkv, z, beta_raw, alpha_raw = jnp.split(p, AB_SPLITS, axis=-1)
        q, k, v = jnp.split(jax.nn.silu(centred_conv(qkv, conv_w)), 3, axis=-1)
        q = l2_normalise(split_heads(q, GDN_HEADS)) * (GDN_HEAD_DIM ** -0.5)
        k = l2_normalise(split_heads(k, GDN_HEADS))
        v = split_heads(v, GDN_HEADS)
        beta = jax.nn.sigmoid(beta_raw.astype(jnp.float32)).reshape(b, n, 2, GDN_HEADS).transpose(2, 0, 3, 1)
        alpha = alpha_raw.astype(jnp.float32).reshape(b, n, 2, GDN_HEADS)
        g = (-jnp.exp(a_log.astype(jnp.float32)) * jax.nn.softplus(alpha + dt_bias.astype(jnp.float32))).transpose(2, 0, 3, 1)
        sides.append((pool_in, z, [(q, k, v, beta[d], g[d]) for d in range(2)]))
    (pool_c, z_c, dirs_c), (pool_l, z_l, dirs_l) = sides
    state0 = jnp.zeros((h_lat.shape[0], GDN_HEADS, GDN_HEAD_DIM, GDN_HEAD_DIM), jnp.float32)
    o_c, o_l = bidirectional_scan(gdn_chunked, dirs_c, dirs_l, state0)
    out_l = out_projection(jnp.concatenate([pool_mix(pool_l, GRID_W, pool_w, pool_scale),
                                            gated_head_norm(o_l, z_l, norm_w)], axis=-1), w_out)
    out_c = None
    if want_ctx:
        out_c = out_projection(jnp.concatenate([pool_mix(pool_c, n_ctx, pool_w, pool_scale),
                                                gated_head_norm(o_c, z_c, norm_w)], axis=-1), w_out)
    return out_c, out_l


def mixer_fourier_gla(h_ctx, h_lat, w_in, fourier_w, gate_up, gate_b, norm_w, w_out, want_ctx):
    n_ctx = h_ctx.shape[1]
    proj = in_projection(h_ctx, h_lat, w_in, CD_MAIN)
    sides = []
    for p in (proj[:, :n_ctx], proj[:, n_ctx:]):
        b, n, _ = p.shape
        f_in, q, k, v, zg, lr = jnp.split(p, CD_SPLITS, axis=-1)
        q = split_heads(q, GLA_HEADS) * (GLA_HEAD_K ** -0.5)
        k = split_heads(k, GLA_HEADS)
        v = split_heads(v, GLA_HEADS)
        lr = lr.reshape(b, n, 2, GLA_GATE_RANK)
        gk = jax.nn.log_sigmoid((jnp.einsum('bndr,drk->dbnk', lr, gate_up)
                                 + gate_b[:, None, None, :]).astype(jnp.float32)) / GLA_GATE_TEMP
        sides.append((f_in, zg, [(q, k, v, split_heads(gk[d], GLA_HEADS)) for d in range(2)]))
    (f_c, z_c, dirs_c), (f_l, z_l, dirs_l) = sides
    state0 = jnp.zeros((h_lat.shape[0], GLA_HEADS, GLA_HEAD_K, GLA_HEAD_V), jnp.float32)
    o_c, o_l = bidirectional_scan(gla_chunked, dirs_c, dirs_l, state0)
    out_l = out_projection(jnp.concatenate([fourier_mix(f_l, fourier_w), gated_head_norm(o_l, z_l, norm_w)], axis=-1), w_out)
    out_c = None
    if want_ctx:
        out_c = out_projection(jnp.concatenate([fourier_mix(f_c, fourier_w), gated_head_norm(o_c, z_c, norm_w)], axis=-1), w_out)
    return out_c, out_l


def modulation(c, c_ctx, mod_w, mod_b):
    b = c.shape[0]
    rows = jnp.concatenate([c, c_ctx[None, :], jnp.zeros((8 - b - 1, c.shape[1]), c.dtype)], axis=0)
    out = dense_matmul(jax.nn.silu(rows), mod_w, tm=8, tn=512) + mod_b
    return out[:b, None, :], out[b][None, None, :]


def kernel(x, c, ctx, c_ctx, mod_w, mod_b, norm1_w, norm2_w, ab_w_in, pool_w, pool_scale, gdn_conv_w,
           gdn_a_log, gdn_dt_bias, gdn_norm_w, ab_w_out, cd_w_in, fourier_w, gla_gate_up, gla_gate_b,
           gla_norm_w, cd_w_out, moe_group_w, moe_group_b, moe_expert_w, moe_expert_b, moe_w1, moe_w3,
           moe_w2, final_norm_w):
    d = x.shape[-1]
    x_lat, x_ctx = x, ctx
    for i in range(DEPTH):
        last = i == DEPTH - 1
        j = i // 2
        mod_l, mod_c = modulation(c, c_ctx, mod_w[i], mod_b[i])
        sh1_l, sc1_l, g1_l, sh2_l, sc2_l, g2_l = jnp.split(mod_l, N_MOD, axis=-1)
        sh1_c, sc1_c, g1_c, sh2_c, sc2_c, g2_c = jnp.split(mod_c, N_MOD, axis=-1)
        h_l = modulate(rms_norm(x_lat, norm1_w[i]), sh1_l, sc1_l)
        h_c = modulate(rms_norm(x_ctx, norm1_w[i]), sh1_c, sc1_c)
        if i % 2 == 0:
            m_c, m_l = mixer_pool_gdn(h_c, h_l, ab_w_in[j], pool_w[j], pool_scale[j], gdn_conv_w[j],
                                      gdn_a_log[j], gdn_dt_bias[j], gdn_norm_w[j], ab_w_out[j], not last)
        else:
            m_c, m_l = mixer_fourier_gla(h_c, h_l, cd_w_in[j], fourier_w[j], gla_gate_up[j], gla_gate_b[j],
                                         gla_norm_w[j], cd_w_out[j], not last)
        x_lat = x_lat + g1_l * m_l
        f_l = modulate(rms_norm(x_lat, norm2_w[i]), sh2_l, sc2_l).reshape(-1, d)
        w1 = moe_w1[i].astype(jnp.bfloat16)
        w3 = moe_w3[i].astype(jnp.bfloat16)
        w2 = moe_w2[i].astype(jnp.bfloat16)
        moe_args = (moe_group_w[i], moe_group_b[i], moe_expert_w[i], moe_expert_b[i], w1, w3, w2)
        if last:
            x_lat = x_lat + g2_l * hier_moe(f_l, *moe_args).reshape(x_lat.shape)
        else:
            x_ctx = x_ctx + g1_c * m_c
            f_c = modulate(rms_norm(x_ctx, norm2_w[i]), sh2_c, sc2_c).reshape(-1, d)
            y = hier_moe(jnp.concatenate([f_c, f_l], axis=0), *moe_args)
            n_c = f_c.shape[0]
            x_ctx = x_ctx + g2_c * y[:n_c].reshape(x_ctx.shape)
            x_lat = x_lat + g2_l * y[n_c:].reshape(x_lat.shape)
    return rms_norm(x_lat, final_norm_w)
```

```python
import functools

import jax
import jax.numpy as jnp
from jax import lax
from jax.experimental import pallas as pl
from jax.experimental.pallas import tpu as pltpu

D_MODEL = 4096
DEPTH = 2
CTX_LEN = 256
GRID_W = 64
N_MOD = 6
NORM_EPS = 1e-6

POOL_WINDOWS = (2, 4, 8, 16)
POOL_GROUPS = 4
POOL_GROUP_DIM = D_MODEL // 16
POOL_DIM = POOL_GROUPS * POOL_GROUP_DIM

GDN_HEAD_DIM = 128
GDN_DIM = D_MODEL - POOL_DIM
GDN_HEADS = GDN_DIM // GDN_HEAD_DIM
GDN_CONV = 5
GDN_CHUNK = 64

FOURIER_GROUPS = 4
FOURIER_GROUP_DIM = D_MODEL // 16
FOURIER_DIM = FOURIER_GROUPS * FOURIER_GROUP_DIM

GLA_HEADS = 6
GLA_V_DIM = D_MODEL - FOURIER_DIM
GLA_K_DIM = GLA_V_DIM // 2
GLA_HEAD_K = GLA_K_DIM // GLA_HEADS
GLA_HEAD_V = GLA_V_DIM // GLA_HEADS
GLA_GATE_RANK = 16
GLA_GATE_TEMP = 16.0
GLA_CHUNK = 32

MOE_GROUPS = 4
MOE_EXPERTS_PER_GROUP = 8
MOE_EXPERTS = MOE_GROUPS * MOE_EXPERTS_PER_GROUP
MOE_TOP_K = 2
MOE_HIDDEN = D_MODEL // 8

AB_MAIN = POOL_DIM + 4 * GDN_DIM
CD_SPLITS = (FOURIER_DIM, FOURIER_DIM + GLA_K_DIM, FOURIER_DIM + 2 * GLA_K_DIM,
             FOURIER_DIM + 2 * GLA_K_DIM + GLA_V_DIM, FOURIER_DIM + 2 * GLA_K_DIM + 2 * GLA_V_DIM)
CD_MAIN = FOURIER_DIM + 2 * GLA_K_DIM + 2 * GLA_V_DIM

LANES = 128
SUBLANES = 8
VMEM_LIMIT = 48 * 1024 * 1024
MOE_BLOCK_ROWS = 256
ROW_BLOCK = 256
DENSE_TM = 1024
GDN_PREP_HEADS = 4
GDN_SCAN_HEADS = 8


def _cparams(*sem):
    return pltpu.CompilerParams(dimension_semantics=sem, vmem_limit_bytes=VMEM_LIMIT)


def _dot(a, b):
    return jnp.dot(a, b, preferred_element_type=jnp.float32)


def _dot_nt(a, b):
    return lax.dot_general(a, b, (((1,), (1,)), ((), ())), preferred_element_type=jnp.float32)


def _dot_tn(a, b):
    return lax.dot_general(a, b, (((0,), (0,)), ((), ())), preferred_element_type=jnp.float32)


def _bf(x):
    return x.astype(jnp.bfloat16)


def _split3(x):
    hi = _bf(x)
    r1 = x - hi.astype(jnp.float32)
    mid = _bf(r1)
    lo = _bf(r1 - mid.astype(jnp.float32))
    return hi, mid, lo


def _mm_body(*refs, n_pairs, has_res):
    o_ref = refs[-1]
    acc = _dot(refs[0][...], refs[1][...])
    for p in range(1, n_pairs):
        acc = acc + _dot(refs[2 * p][...], refs[2 * p + 1][...])
    if has_res:
        res_ref, gate_ref = refs[2 * n_pairs], refs[2 * n_pairs + 1]
        acc = res_ref[...] + gate_ref[0] * acc
    o_ref[...] = acc.astype(o_ref.dtype)


def dense_matmul(pairs, out_dtype=jnp.float32, tm=DENSE_TM, tn=512, m=None, residual=None, gates=None,
                 gate_index=None, out_shape=None, out_index=None):
    m = pairs[0][0].shape[0] if m is None else m
    n = pairs[0][1].shape[1]
    tm = min(tm, m)
    tn = min(tn, n)
    assert m % tm == 0 and n % tn == 0, (m, n, tm, tn)
    in_specs, args = [], []
    for x, w in pairs:
        k = x.shape[1]
        in_specs += [pl.BlockSpec((tm, k), lambda i, j: (i, 0)), pl.BlockSpec((k, tn), lambda i, j: (0, j))]
        args += [x, w]
    if residual is not None:
        in_specs += [pl.BlockSpec((tm, tn), lambda i, j: (i, j)),
                     pl.BlockSpec((1, 1, tn), lambda i, j: (gate_index(i), 0, j))]
        args += [residual, gates]
    return pl.pallas_call(
        functools.partial(_mm_body, n_pairs=len(pairs), has_res=residual is not None),
        out_shape=jax.ShapeDtypeStruct((m, n) if out_shape is None else out_shape, out_dtype),
        grid=(m // tm, n // tn),
        in_specs=in_specs,
        out_specs=pl.BlockSpec((tm, tn), (lambda i, j: (i, j)) if out_index is None else out_index),
        compiler_params=_cparams("parallel", "parallel"),
        name="dense_matmul",
    )(*args)


def _pad_cols(w, mult=LANES):
    pad = (-w.shape[1]) % mult
    return jnp.pad(w, ((0, 0), (0, pad))) if pad else w


def _norm_mod_body(x_ref, nw_ref, shift_ref, scale_ref, o_ref):
    x = x_ref[...]
    y = x * lax.rsqrt(jnp.mean(x * x, axis=-1, keepdims=True) + NORM_EPS) * nw_ref[...]
    o_ref[...] = (y * (1.0 + scale_ref[0]) + shift_ref[0]).astype(o_ref.dtype)


def norm_modulate(x, norm_w, shift, scale, row_index, m=None, tm=512):
    m = x.shape[0] if m is None else m
    d = x.shape[1]
    vec = pl.BlockSpec((1, 1, d), lambda i: (row_index(i), 0, 0))
    return pl.pallas_call(
        _norm_mod_body,
        out_shape=jax.ShapeDtypeStruct((m, d), jnp.bfloat16),
        grid=(m // tm,),
        in_specs=[pl.BlockSpec((tm, d), lambda i: (i, 0)), pl.BlockSpec((1, d), lambda i: (0, 0)), vec, vec],
        out_specs=pl.BlockSpec((tm, d), lambda i: (i, 0)),
        compiler_params=_cparams("parallel"),
        name="norm_modulate",
    )(x, norm_w.reshape(1, d), shift, scale)


def _gdn_gates_body(tail_ref, par_ref, o_ref, *, n_heads, chunk):
    t = tail_ref[...]
    rows = t.shape[0]
    lane = lax.broadcasted_iota(jnp.int32, t.shape, 1)
    a_row = par_ref[0:1, :]
    dtb_row = par_ref[1:2, :]
    beta = jax.nn.sigmoid(t)
    z = t + dtb_row
    g = a_row * (jnp.maximum(z, 0.0) + jnp.log1p(jnp.exp(-jnp.abs(z))))
    ri = lax.broadcasted_iota(jnp.int32, (rows, rows), 0)
    ci = lax.broadcasted_iota(jnp.int32, (rows, rows), 1)
    shift = chunk.bit_length() - 1
    same = (ri >> shift) == (ci >> shift)
    pre = _bf(jnp.where(same & (ci <= ri), 1.0, 0.0))
    suf = _bf(jnp.where(same & (ci >= ri), 1.0, 0.0))
    ones = _bf(jnp.where(same, 1.0, 0.0))
    parts = _split3(g)
    cf = _dot(pre, parts[0]) + _dot(pre, parts[1]) + _dot(pre, parts[2])
    cb = _dot(suf, parts[0]) + _dot(suf, parts[1]) + _dot(suf, parts[2])
    tot = _dot(ones, parts[0]) + _dot(ones, parts[1]) + _dot(ones, parts[2])
    o_ref[:, :LANES] = jnp.where(lane < 2 * n_heads, beta,
                                 jnp.where(lane < 3 * n_heads, cf, jnp.where(lane < 4 * n_heads, cb, 0.0)))
    o_ref[:, LANES:] = tot


def gdn_gates(tail, a_log, dt_bias, n_heads):
    t = tail.shape[0]
    par = jnp.zeros((SUBLANES, LANES), jnp.float32)
    par = par.at[0, 2 * n_heads:4 * n_heads].set(-jnp.exp(a_log.astype(jnp.float32)).reshape(-1))
    par = par.at[1, 2 * n_heads:4 * n_heads].set(dt_bias.astype(jnp.float32).reshape(-1))
    return pl.pallas_call(
        functools.partial(_gdn_gates_body, n_heads=n_heads, chunk=GDN_CHUNK),
        out_shape=jax.ShapeDtypeStruct((t, 2 * LANES), jnp.float32),
        grid=(t // ROW_BLOCK,),
        in_specs=[pl.BlockSpec((ROW_BLOCK, LANES), lambda i: (i, 0)),
                  pl.BlockSpec((SUBLANES, LANES), lambda i: (0, 0))],
        out_specs=pl.BlockSpec((ROW_BLOCK, 2 * LANES), lambda i: (i, 0)),
        compiler_params=_cparams("parallel"),
        name="gdn_gates",
    )(tail, par)


def _gdn_prep_body(q_ref, k_ref, v_ref, qp_ref, kp_ref, vp_ref, qn_ref, kn_ref, vn_ref,
                   wq_ref, wk_ref, wv_ref, gt_ref,
                   u_ref, w_ref, qd_ref, ke_ref, at_ref, ge_ref, *, heads, n_heads, blocks_per_seq, n_lat_blocks):
    rb = pl.program_id(0)
    hg = pl.program_id(1)
    n = ROW_BLOCK
    c = GDN_CHUNK
    n_ch = n // c
    cs = c.bit_length() - 1
    hd = GDN_HEAD_DIM
    is_ctx = rb >= n_lat_blocks
    first = jnp.logical_or(is_ctx, rb % blocks_per_seq == 0)
    last = jnp.logical_or(is_ctx, rb % blocks_per_seq == blocks_per_seq - 1)
    m_prev = jnp.where(first, 0.0, 1.0)
    m_next = jnp.where(last, 0.0, 1.0)

    def conv_silu(cur_ref, prev_ref, next_ref, cw_ref):
        ext = jnp.concatenate([prev_ref[...] * m_prev, cur_ref[...], next_ref[...] * m_next], axis=0)
        acc = None
        for j in range(GDN_CONV):
            lo = SUBLANES - GDN_CONV // 2 + j
            term = ext[lo:lo + ROW_BLOCK, :] * cw_ref[j:j + 1, :]
            acc = term if acc is None else acc + term
        return acc * jax.nn.sigmoid(acc)

    qc = conv_silu(q_ref, qp_ref, qn_ref, wq_ref)
    kc = conv_silu(k_ref, kp_ref, kn_ref, wk_ref)
    vc = conv_silu(v_ref, vp_ref, vn_ref, wv_ref)

    gt = gt_ref[...]
    lane = lax.broadcasted_iota(jnp.int32, gt.shape, 1)
    ri = lax.broadcasted_iota(jnp.int32, (n, n), 0)
    ci = lax.broadcasted_iota(jnp.int32, (n, n), 1)
    eye = ri == ci
    blk = (ri >> cs) == (ci >> cs)
    masks = ((blk & (ri >= ci), blk & (ri > ci)), (blk & (ri <= ci), blk & (ri < ci)))
    wr = lax.broadcasted_iota(jnp.int32, (c, n), 0)
    wc = lax.broadcasted_iota(jnp.int32, (c, n), 1)
    eye_wide = jnp.where(wr == (wc & (c - 1)), 1.0, 0.0)

    def tall(wide):
        return _bf(jnp.where(blk, jnp.concatenate([wide] * n_ch, axis=0), 0.0))

    def column(idx):
        return jnp.sum(jnp.where(lane == idx, gt, 0.0), axis=1, keepdims=True)

    for gi in range(heads):
        h = hg * heads + gi
        sl = slice(gi * hd, (gi + 1) * hd)
        qh, kh, v = qc[:, sl], kc[:, sl], vc[:, sl]
        q = qh * lax.rsqrt(jnp.sum(qh * qh, axis=-1, keepdims=True) + NORM_EPS) * (hd ** -0.5)
        k = kh * lax.rsqrt(jnp.sum(kh * kh, axis=-1, keepdims=True) + NORM_EPS)
        k16 = _bf(k)
        qk = _dot_nt(_bf(q), k16)
        kk = _dot_nt(k16, k16)
        for d in range(2):
            beta = column(d * n_heads + h)
            gc = column((2 + d) * n_heads + h)
            tot = column(LANES + (2 + d) * n_heads + h)
            gc_row = jnp.sum(jnp.where(eye, gc, 0.0), axis=0, keepdims=True)
            incl, strict = masks[d]
            decay = jnp.where(incl, jnp.exp(jnp.where(incl, gc - gc_row, 0.0)), 0.0)
            a = jnp.where(strict, (beta * kk) * decay, 0.0)
            p = -a[0:c]
            for j in range(1, n_ch):
                p = p - a[j * c:(j + 1) * c]
            t = eye_wide + p
            p_tall = tall(p)
            span = 2
            while span < c:
                p = _dot(_bf(p), p_tall)
                p_tall = tall(p)
                t = t + _dot(_bf(t), p_tall)
                span *= 2
            eg = jnp.exp(gc)
            kbeta = k * beta
            sol = _dot(tall(t), _bf(jnp.concatenate([v * beta, kbeta * eg], axis=1)))
            u_ref[d, :, sl] = sol[:, :hd]
            w_ref[d, :, sl] = _bf(sol[:, hd:])
            qd_ref[d, :, sl] = _bf(q * eg)
            ke_ref[d, :, sl] = _bf(k * jnp.exp(tot - gc))
            attn = qk * decay
            for j in range(n_ch):
                at_ref[d, j * c:(j + 1) * c, gi * c:(gi + 1) * c] = _bf(attn[j * c:(j + 1) * c, j * c:(j + 1) * c])
                ge_ref[d, j * SUBLANES:(j + 1) * SUBLANES, sl] = jnp.broadcast_to(
                    jnp.exp(tot[j * c:j * c + SUBLANES]), (SUBLANES, hd))


def gdn_prepare(proj, tail_gates, conv_w, n_heads, qkv_col0, blocks_per_seq, n_lat_blocks):
    t = proj.shape[0]
    hd = GDN_HEAD_DIM
    g = GDN_PREP_HEADS
    gw = g * hd
    dim = n_heads * hd
    nb = t // ROW_BLOCK
    qkv = proj[:, qkv_col0:qkv_col0 + 3 * dim].reshape(nb, ROW_BLOCK, 3 * dim)
    head8 = qkv[:, :SUBLANES].reshape(nb * SUBLANES, 3 * dim)
    tail8 = qkv[:, ROW_BLOCK - SUBLANES:].reshape(nb * SUBLANES, 3 * dim)
    cb0 = qkv_col0 // gw
    per = dim // gw

    def cur(off):
        return pl.BlockSpec((ROW_BLOCK, gw), lambda rb, hg: (rb, cb0 + off * per + hg))

    def prev(off):
        return pl.BlockSpec((SUBLANES, gw), lambda rb, hg: (jnp.maximum(rb - 1, 0), off * per + hg))

    def nxt(off):
        return pl.BlockSpec((SUBLANES, gw), lambda rb, hg: (jnp.minimum(rb + 1, nb - 1), off * per + hg))

    def cw(off):
        return pl.BlockSpec((GDN_CONV, gw), lambda rb, hg: (0, off * per + hg))

    out_tok = lambda width: pl.BlockSpec((2, ROW_BLOCK, width), lambda rb, hg: (0, rb, hg))
    outs = pl.pallas_call(
        functools.partial(_gdn_prep_body, heads=g, n_heads=n_heads, blocks_per_seq=blocks_per_seq,
                          n_lat_blocks=n_lat_blocks),
        out_shape=(jax.ShapeDtypeStruct((2, t, dim), jnp.float32),
                   jax.ShapeDtypeStruct((2, t, dim), jnp.bfloat16),
                   jax.ShapeDtypeStruct((2, t, dim), jnp.bfloat16),
                   jax.ShapeDtypeStruct((2, t, dim), jnp.bfloat16),
                   jax.ShapeDtypeStruct((2, t, n_heads * GDN_CHUNK), jnp.bfloat16),
                   jax.ShapeDtypeStruct((2, t // SUBLANES, dim), jnp.float32)),
        grid=(nb, n_heads // g),
        in_specs=[cur(0), cur(1), cur(2), prev(0), prev(1), prev(2), nxt(0), nxt(1), nxt(2),
                  cw(0), cw(1), cw(2), pl.BlockSpec((ROW_BLOCK, 2 * LANES), lambda rb, hg: (rb, 0))],
        out_specs=(out_tok(gw), out_tok(gw), out_tok(gw), out_tok(gw), out_tok(g * GDN_CHUNK),
                   pl.BlockSpec((2, ROW_BLOCK // SUBLANES, gw), lambda rb, hg: (0, rb, hg))),
        compiler_params=_cparams("parallel", "parallel"),
        name="gdn_prepare",
    )(proj, proj, proj, tail8, tail8, tail8, head8, head8, head8, conv_w, conv_w, conv_w, tail_gates)
    return outs


def _gdn_scan_body(uf, wf, qf, kf, af, gf, ub, wb, qb, kb, ab, gb, of_ref, ob_ref, s_ref, *, heads):
    t = pl.program_id(2)
    c = GDN_CHUNK
    hd = GDN_HEAD_DIM
    n_ch = ROW_BLOCK // c

    @pl.when(t == 0)
    def _():
        s_ref[...] = jnp.zeros_like(s_ref)

    pw = 2 * hd
    s_mask = ((lax.broadcasted_iota(jnp.int32, (pw, pw), 0) >> (hd.bit_length() - 1))
              == (lax.broadcasted_iota(jnp.int32, (pw, pw), 1) >> (hd.bit_length() - 1)))
    v_mask = ((lax.broadcasted_iota(jnp.int32, (2 * c, pw), 0) >> (c.bit_length() - 1))
              == (lax.broadcasted_iota(jnp.int32, (2 * c, pw), 1) >> (hd.bit_length() - 1)))
    views = ((uf, wf, qf, kf, af, gf, of_ref), (ub, wb, qb, kb, ab, gb, ob_ref))
    for step in range(n_ch):
        for d in range(2):
            u_r, w_r, q_r, k_r, a_r, g_r, o_r = views[d]
            ch = step if d == 0 else n_ch - 1 - step
            rows = slice(ch * c, (ch + 1) * c)
            for p in range(heads // 2):
                sl = slice(p * pw, (p + 1) * pw)
                s = s_ref[d, p]
                s_diag = _bf(jnp.where(s_mask, jnp.concatenate([s, s], axis=0), 0.0))
                wq = jnp.concatenate([w_r[0, rows, sl], q_r[0, rows, sl]], axis=0)
                r = _dot(wq, s_diag)
                v_new = u_r[0, rows, sl] - r[:c]
                v_diag = _bf(jnp.where(v_mask, jnp.concatenate([v_new, v_new], axis=0), 0.0))
                o_r[0, rows, sl] = r[c:] + _dot(a_r[0, rows, p * 2 * c:(p + 1) * 2 * c], v_diag)
                k_stack = jnp.concatenate([k_r[0, rows, p * pw:p * pw + hd], k_r[0, rows, p * pw + hd:(p + 1) * pw]],
                                          axis=0)
                ge = g_r[0, ch * SUBLANES:ch * SUBLANES + 1, sl]
                s_ref[d, p] = s * ge + _dot_tn(k_stack, v_diag)


def gdn_scan(u, w, qd, ke, at, ge, n_heads, n_batch, blocks_per_seq):
    t = u.shape[1]
    hd = GDN_HEAD_DIM
    g = GDN_SCAN_HEADS
    gw = g * hd
    n_lat_blocks = n_batch * blocks_per_seq

    def fwd(b, s):
        return jnp.where(s == 0, n_lat_blocks + b, b * blocks_per_seq + s - 1)

    def bwd(b, s):
        return jnp.where(s == 0, n_lat_blocks + b, b * blocks_per_seq + blocks_per_seq - s)

    def tok(d, width, blk):
        return pl.BlockSpec((1, ROW_BLOCK, width), lambda b, hg, s: (d, blk(b, s), hg))

    def gsp(d, blk):
        return pl.BlockSpec((1, ROW_BLOCK // SUBLANES, gw), lambda b, hg, s: (d, blk(b, s), hg))

    in_specs = []
    for d, blk in ((0, fwd), (1, bwd)):
        in_specs += [tok(d, gw, blk), tok(d, gw, blk), tok(d, gw, blk), tok(d, gw, blk),
                     tok(d, g * GDN_CHUNK, blk), gsp(d, blk)]
    o_f, o_b = pl.pallas_call(
        functools.partial(_gdn_scan_body, heads=g),
        out_shape=(jax.ShapeDtypeStruct((1, t, n_heads * hd), jnp.float32),) * 2,
        grid=(n_batch, n_heads // g, blocks_per_seq + 1),
        in_specs=in_specs,
        out_specs=(pl.BlockSpec((1, ROW_BLOCK, gw), lambda b, hg, s: (0, fwd(b, s), hg)),
                   pl.BlockSpec((1, ROW_BLOCK, gw), lambda b, hg, s: (0, bwd(b, s), hg))),
        scratch_shapes=[pltpu.VMEM((2, g // 2, hd, 2 * hd), jnp.float32)],
        compiler_params=_cparams("parallel", "parallel", "arbitrary"),
        name="gdn_scan",
    )(u, w, qd, ke, at, ge, u, w, qd, ke, at, ge)
    return o_f[0], o_b[0]


GATED_NORM_COLS = 1024


def _gated_norm_body(of_ref, ob_ref, z_ref, nw_ref, y_ref, *, hd):
    for h in range(GATED_NORM_COLS // hd):
        sl = slice(h * hd, (h + 1) * hd)
        o = of_ref[:, sl] + ob_ref[:, sl]
        y = o * lax.rsqrt(jnp.mean(o * o, axis=-1, keepdims=True) + NORM_EPS) * nw_ref[...]
        z = z_ref[:, sl]
        y_ref[:, sl] = (y * (z * jax.nn.sigmoid(z))).astype(y_ref.dtype)


def gated_head_norm(o_f, o_b, proj, z_col0, norm_w, hd, m=None):
    m = o_f.shape[0] if m is None else m
    dim = o_f.shape[1]
    cw = GATED_NORM_COLS
    assert z_col0 % cw == 0 and dim % cw == 0 and cw % hd == 0
    return pl.pallas_call(
        functools.partial(_gated_norm_body, hd=hd),
        out_shape=jax.ShapeDtypeStruct((m, dim), jnp.bfloat16),
        grid=(m // ROW_BLOCK, dim // cw),
        in_specs=[pl.BlockSpec((ROW_BLOCK, cw), lambda i, j: (i, j)),
                  pl.BlockSpec((ROW_BLOCK, cw), lambda i, j: (i, j)),
                  pl.BlockSpec((ROW_BLOCK, cw), lambda i, j: (i, z_col0 // cw + j)),
                  pl.BlockSpec((1, hd), lambda i, j: (0, 0))],
        out_specs=pl.BlockSpec((ROW_BLOCK, cw), lambda i, j: (i, j)),
        compiler_params=_cparams("parallel", "parallel"),
        name="gated_head_norm",
    )(o_f, o_b, proj, norm_w.reshape(1, hd))


def _pool_body(u_ref, w_ref, sc_ref, y_ref, *, n_lat_blocks):
    rb = pl.program_id(0)
    is_ctx = rb >= n_lat_blocks
    row_len = jnp.where(is_ctx, CTX_LEN, GRID_W)
    shift = jnp.where(is_ctx, CTX_LEN.bit_length() - 1, GRID_W.bit_length() - 1)
    n = ROW_BLOCK
    ri = lax.broadcasted_iota(jnp.int32, (n, n), 0)
    ci = lax.broadcasted_iota(jnp.int32, (n, n), 1)
    same = (ri >> shift) == (ci >> shift)
    pos_r = ri & (row_len - 1)
    pos_c = ci & (row_len - 1)
    rcol = lax.broadcasted_iota(jnp.int32, (n, 1), 0) & (row_len - 1)
    gd = POOL_GROUP_DIM
    for g, win in enumerate(POOL_WINDOWS):
        half = win // 2
        lo = jnp.maximum(pos_r - half, 0)
        hi = jnp.minimum(pos_r + half - 1, row_len - 1)
        band = _bf(jnp.where(same & (pos_c >= lo) & (pos_c <= hi), 1.0, 0.0))
        cnt = (jnp.minimum(rcol + half - 1, row_len - 1) - jnp.maximum(rcol - half, 0) + 1).astype(jnp.float32)
        u = u_ref[:, g * gd:(g + 1) * gd]
        parts = _split3(u)
        win_sum = _dot(band, parts[0]) + _dot(band, parts[1]) + _dot(band, parts[2])
        dlt = win_sum / cnt - u
        y = _dot(_bf(dlt), _bf(w_ref[g])) * sc_ref[:, g * gd:(g + 1) * gd]
        y_ref[:, g * gd:(g + 1) * gd] = y.astype(y_ref.dtype)


def pool_mix(proj, pool_w, pool_scale, n_lat_blocks):
    t = proj.shape[0]
    return pl.pallas_call(
        functools.partial(_pool_body, n_lat_blocks=n_lat_blocks),
        out_shape=jax.ShapeDtypeStruct((t, POOL_DIM), jnp.bfloat16),
        grid=(t // ROW_BLOCK,),
        in_specs=[pl.BlockSpec((ROW_BLOCK, POOL_DIM), lambda i: (i, 0)),
                  pl.BlockSpec((POOL_GROUPS, POOL_GROUP_DIM, POOL_GROUP_DIM), lambda i: (0, 0, 0)),
                  pl.BlockSpec((1, POOL_DIM), lambda i: (0, 0))],
        out_specs=pl.BlockSpec((ROW_BLOCK, POOL_DIM), lambda i: (i, 0)),
        compiler_params=_cparams("parallel"),
        name="pool_mix",
    )(proj, pool_w, pool_scale.reshape(1, POOL_DIM))


def _moe_body(be_ref, x_ref, w1_ref, w3_ref, w2_ref, o_ref):
    del be_ref
    x = x_ref[...]
    a = _dot(x, w1_ref[0])
    b = _dot(x, w3_ref[0])
    h = _bf(a * jax.nn.sigmoid(a) * b)
    o_ref[...] = _dot(h, w2_ref[0])


def moe_expert_blocks(xbuf, block_expert, w1, w3, w2):
    p, d = xbuf.shape
    nb = p // MOE_BLOCK_ROWS
    hid = w1.shape[-1]
    grid_spec = pltpu.PrefetchScalarGridSpec(
        num_scalar_prefetch=1,
        grid=(nb,),
        in_specs=[pl.BlockSpec((MOE_BLOCK_ROWS, d), lambda i, be: (i, 0)),
                  pl.BlockSpec((1, d, hid), lambda i, be: (be[i], 0, 0)),
                  pl.BlockSpec((1, d, hid), lambda i, be: (be[i], 0, 0)),
                  pl.BlockSpec((1, hid, d), lambda i, be: (be[i], 0, 0))],
        out_specs=pl.BlockSpec((MOE_BLOCK_ROWS, d), lambda i, be: (i, 0)),
    )
    return pl.pallas_call(
        _moe_body,
        out_shape=jax.ShapeDtypeStruct((p, d), jnp.float32),
        grid_spec=grid_spec,
        compiler_params=_cparams("arbitrary"),
        name="moe_expert_blocks",
    )(block_expert, xbuf, w1, w3, w2)


def hier_moe(h, group_w, group_b, expert_w, expert_b, w1, w3, w2):
    t, d = h.shape
    rows = jnp.arange(t)
    router_w = jnp.concatenate([group_w, expert_w.transpose(1, 0, 2).reshape(d, MOE_EXPERTS)], axis=1)
    logits = jnp.dot(h.astype(jnp.float32), router_w, precision=lax.Precision.HIGHEST)
    group_logits = logits[:, :MOE_GROUPS] + group_b
    group = jnp.argmax(group_logits, axis=-1)
    p_group = jax.nn.softmax(group_logits, axis=-1)[rows, group][:, None]
    expert_logits = logits[:, MOE_GROUPS:].reshape(t, MOE_GROUPS, MOE_EXPERTS_PER_GROUP) + expert_b
    sel = expert_logits[rows, group]
    top_val, top_idx = lax.top_k(sel, MOE_TOP_K)
    weight = p_group * jax.nn.softmax(top_val, axis=-1)
    expert = (group[:, None] * MOE_EXPERTS_PER_GROUP + top_idx).reshape(-1)
    tk = t * MOE_TOP_K
    onehot = (expert[:, None] == jnp.arange(MOE_EXPERTS)[None, :]).astype(jnp.int32)
    counts = jnp.sum(onehot, axis=0)
    rank = jnp.sum((jnp.cumsum(onehot, axis=0) - onehot) * onehot, axis=1)
    bm = MOE_BLOCK_ROWS
    padded = (counts + bm - 1) // bm * bm
    pad_end = jnp.cumsum(padded)
    pad_start = pad_end - padded
    dest = pad_start[expert] + rank
    n_blocks = -(-tk // bm) + MOE_EXPERTS
    src = jnp.zeros((n_blocks * bm,), jnp.int32).at[dest].set(jnp.repeat(rows, MOE_TOP_K).astype(jnp.int32))
    block_expert = jnp.minimum(jnp.searchsorted(pad_end, jnp.arange(n_blocks) * bm, side='right'),
                               MOE_EXPERTS - 1).astype(jnp.int32)
    xbuf = h[src]
    ybuf = moe_expert_blocks(xbuf, block_expert, w1, w3, w2)
    y2 = ybuf[dest].reshape(t, MOE_TOP_K, d)
    return y2[:, 0] * weight[:, 0:1] + y2[:, 1] * weight[:, 1:2]


GLA_SUB = 32
GLA_STEP = 64


def _gla_prep_body(q_ref, k_ref, v_ref, lr_ref, wg_ref, gb_ref, qd_ref, ke_ref, at_ref, ge_ref, v16_ref):
    n = ROW_BLOCK
    c = GLA_STEP
    sb = GLA_SUB
    n_sb = n // sb
    q = q_ref[...] * (GLA_HEAD_K ** -0.5)
    k = k_ref[...]
    v16_ref[...] = _bf(v_ref[...])
    lr = lr_ref[...]
    ri = lax.broadcasted_iota(jnp.int32, (n, n), 0)
    ci = lax.broadcasted_iota(jnp.int32, (n, n), 1)
    cs = c.bit_length() - 1
    ss = sb.bit_length() - 1
    same_c = (ri >> cs) == (ci >> cs)
    same_s = (ri >> ss) == (ci >> ss)
    for d in range(2):
        x = jnp.dot(lr, wg_ref[d], preferred_element_type=jnp.float32, precision=lax.Precision.HIGHEST) + gb_ref[d]
        gk = (jnp.minimum(x, 0.0) - jnp.log1p(jnp.exp(-jnp.abs(x)))) * (1.0 / GLA_GATE_TEMP)
        run = _bf(jnp.where(same_c & ((ci <= ri) if d == 0 else (ci >= ri)), 1.0, 0.0))
        parts = _split3(gk)
        g = _dot(run, parts[0]) + _dot(run, parts[1]) + _dot(run, parts[2])
        off, tot_s, tot_c = [], [], []
        for b in range(n_sb):
            lo = b * sb
            inner = (b % 2 == 1) if d == 0 else (b % 2 == 0)
            edge = g[lo + sb - 1:lo + sb] if d == 0 else g[lo:lo + 1]
            if inner:
                prev = g[lo - 1:lo] if d == 0 else g[lo + sb:lo + sb + 1]
                off.append(jnp.broadcast_to(prev, (sb, prev.shape[1])))
                tot_s.append(jnp.broadcast_to(edge - prev, (sb, prev.shape[1])))
            else:
                off.append(jnp.zeros((sb, g.shape[1]), jnp.float32))
                tot_s.append(jnp.broadcast_to(edge, (sb, edge.shape[1])))
        for j in range(n // c):
            edge = g[j * c + c - 1:j * c + c] if d == 0 else g[j * c:j * c + 1]
            tot_c.append(jnp.broadcast_to(edge, (c, edge.shape[1])))
        off = jnp.concatenate(off, axis=0)
        tot_s = jnp.concatenate(tot_s, axis=0)
        tot_c = jnp.concatenate(tot_c, axis=0)
        loc = g - off
        qd_loc = _bf(q * jnp.exp(loc))
        m_diag = _dot_nt(qd_loc, _bf(k * jnp.exp(-loc)))
        m_cross = _dot_nt(qd_loc, _bf(k * jnp.exp(tot_s - loc)))
        tri = (ci <= ri) if d == 0 else (ci >= ri)
        nxt = ((ri >> ss) == (ci >> ss) + 1) if d == 0 else ((ri >> ss) + 1 == (ci >> ss))
        attn = jnp.where(same_s & tri, m_diag, 0.0) + jnp.where(same_c & nxt, m_cross, 0.0)
        qd_ref[d] = _bf(q * jnp.exp(g))
        ke_ref[d] = _bf(k * jnp.exp(tot_c - g))
        for j in range(n // c):
            at_ref[d, 0, j * c:(j + 1) * c, :] = _bf(attn[j * c:(j + 1) * c, j * c:(j + 1) * c])
            ge_ref[d, j * SUBLANES:(j + 1) * SUBLANES, :] = jnp.exp(tot_c[j * c:j * c + SUBLANES])


def gla_prepare(proj, lr_tail, gate_up, gate_b):
    t = proj.shape[0]
    hk, hv = GLA_HEAD_K, GLA_HEAD_V
    wg = jnp.zeros((2, LANES, GLA_K_DIM), jnp.float32)
    for d in range(2):
        wg = wg.at[d, d * GLA_GATE_RANK:(d + 1) * GLA_GATE_RANK].set(gate_up[d].astype(jnp.float32))
    gb = gate_b.astype(jnp.float32).reshape(2, 1, GLA_K_DIM)
    q0, k0, v0 = CD_SPLITS[0] // hk, CD_SPLITS[1] // hk, CD_SPLITS[2] // hv
    assert CD_SPLITS[0] % hk == 0 and CD_SPLITS[1] % hk == 0 and CD_SPLITS[2] % hv == 0
    return pl.pallas_call(
        _gla_prep_body,
        out_shape=(jax.ShapeDtypeStruct((2, t, GLA_K_DIM), jnp.bfloat16),
                   jax.ShapeDtypeStruct((2, t, GLA_K_DIM), jnp.bfloat16),
                   jax.ShapeDtypeStruct((2, GLA_HEADS, t, GLA_STEP), jnp.bfloat16),
                   jax.ShapeDtypeStruct((2, t // SUBLANES, GLA_K_DIM), jnp.float32),
                   jax.ShapeDtypeStruct((t, GLA_V_DIM), jnp.bfloat16)),
        grid=(t // ROW_BLOCK, GLA_HEADS),
        in_specs=[pl.BlockSpec((ROW_BLOCK, hk), lambda rb, h: (rb, q0 + h)),
                  pl.BlockSpec((ROW_BLOCK, hk), lambda rb, h: (rb, k0 + h)),
                  pl.BlockSpec((ROW_BLOCK, hv), lambda rb, h: (rb, v0 + h)),
                  pl.BlockSpec((ROW_BLOCK, LANES), lambda rb, h: (rb, 0)),
                  pl.BlockSpec((2, LANES, hk), lambda rb, h: (0, 0, h)),
                  pl.BlockSpec((2, 1, hk), lambda rb, h: (0, 0, h))],
        out_specs=(pl.BlockSpec((2, ROW_BLOCK, hk), lambda rb, h: (0, rb, h)),
                   pl.BlockSpec((2, ROW_BLOCK, hk), lambda rb, h: (0, rb, h)),
                   pl.BlockSpec((2, 1, ROW_BLOCK, GLA_STEP), lambda rb, h: (0, h, rb, 0)),
                   pl.BlockSpec((2, ROW_BLOCK // SUBLANES, hk), lambda rb, h: (0, rb, h)),
                   pl.BlockSpec((ROW_BLOCK, hv), lambda rb, h: (rb, h))),
        compiler_params=_cparams("parallel", "parallel"),
        name="gla_prepare",
    )(proj, proj, proj, lr_tail, wg, gb)


def _gla_scan_body(qf, kf, af, gf, vf, qb, kb, ab, gb, vb, of_ref, ob_ref, s_ref):
    t = pl.program_id(2)
    c = GLA_STEP
    n_ch = ROW_BLOCK // c

    @pl.when(t == 0)
    def _():
        s_ref[...] = jnp.zeros_like(s_ref)

    views = ((qf, kf, af, gf, vf, of_ref), (qb, kb, ab, gb, vb, ob_ref))
    for step in range(n_ch):
        for d in range(2):
            q_r, k_r, a_r, g_r, v_r, o_r = views[d]
            ch = step if d == 0 else n_ch - 1 - step
            rows = slice(ch * c, (ch + 1) * c)
            st = s_ref[d]
            v16 = v_r[rows, :]
            o_r[rows, :] = _dot_nt(q_r[0, rows, :], _bf(st)) + _dot(a_r[0, 0, rows, :], v16)
            ge = g_r[0, ch * SUBLANES:ch * SUBLANES + 1, :]
            s_ref[d] = st * ge + _dot_tn(v16, k_r[0, rows, :])


def gla_scan(qd, ke, at, ge, v16, n_batch, blocks_per_seq):
    hk, hv = GLA_HEAD_K, GLA_HEAD_V
    t = v16.shape[0]
    n_lat_blocks = n_batch * blocks_per_seq

    def fwd(b, s):
        return jnp.where(s == 0, n_lat_blocks + b, b * blocks_per_seq + s - 1)

    def bwd(b, s):
        return jnp.where(s == 0, n_lat_blocks + b, b * blocks_per_seq + blocks_per_seq - s)

    def out_blk(blk):
        return lambda b, h, s: (blk(b, s), h)

    def key_spec(d, blk, rows):
        return pl.BlockSpec((1, rows, hk), lambda b, h, s: (d, blk(b, s), h))

    def attn_spec(d, blk):
        return pl.BlockSpec((1, 1, ROW_BLOCK, GLA_STEP), lambda b, h, s: (d, h, blk(b, s), 0))

    def val_spec(blk):
        return pl.BlockSpec((ROW_BLOCK, hv), lambda b, h, s: (blk(b, s), h))

    in_specs = []
    for d, blk in ((0, fwd), (1, bwd)):
        in_specs += [key_spec(d, blk, ROW_BLOCK), key_spec(d, blk, ROW_BLOCK), attn_spec(d, blk),
                     key_spec(d, blk, ROW_BLOCK // SUBLANES), val_spec(blk)]
    o_f, o_b = pl.pallas_call(
        _gla_scan_body,
        out_shape=(jax.ShapeDtypeStruct((t, GLA_V_DIM), jnp.float32),) * 2,
        grid=(n_batch, GLA_HEADS, blocks_per_seq + 1),
        in_specs=in_specs,
        out_specs=(pl.BlockSpec((ROW_BLOCK, hv), out_blk(fwd)), pl.BlockSpec((ROW_BLOCK, hv), out_blk(bwd))),
        scratch_shapes=[pltpu.VMEM((2, hv, hk), jnp.float32)],
        compiler_params=_cparams("parallel", "parallel", "arbitrary"),
        name="gla_scan",
    )(qd, ke, at, ge, v16, qd, ke, at, ge, v16)
    return o_f, o_b


def _fourier_in_body(x_ref, w_ref, o_ref):
    y = _dot(_bf(x_ref[...]), w_ref[0])
    half = y.shape[1] // 2
    o_ref[0] = _bf(y[:, :half])
    o_ref[1] = _bf(y[:, half:])


def dft_tables(n, nc):
    r = 64
    assert n % r == 0
    kk = jnp.arange(n, dtype=jnp.int32)[None, :]
    j1 = jnp.arange(n // r, dtype=jnp.int32)[:, None]
    j0 = jnp.arange(r, dtype=jnp.int32)[:, None]
    a = ((j1 * kk) % (n // r)).astype(jnp.float32) * (2.0 * jnp.pi / (n // r))
    b = ((j0 * kk) % n).astype(jnp.float32) * (2.0 * jnp.pi / n)
    ca, sa, cb, sb = jnp.cos(a)[:, None, :], jnp.sin(a)[:, None, :], jnp.cos(b)[None, :, :], jnp.sin(b)[None, :, :]
    scale = n ** -0.5
    cos_n = ((ca * cb - sa * sb) * scale).reshape(n, n)
    sin_n = ((sa * cb + ca * sb) * scale).reshape(n, n)
    table = jnp.concatenate([cos_n, -sin_n], axis=1).astype(jnp.bfloat16)
    cc = jnp.arange(nc, dtype=jnp.int32)
    ang = ((cc[:, None] * cc[None, :]) % nc).astype(jnp.float32) * (2.0 * jnp.pi / nc)
    return table, jnp.cos(ang) * nc ** -0.5, jnp.sin(ang) * nc ** -0.5


def fourier_mix(proj, fourier_w, n_batch, n_lat):
    gd = FOURIER_GROUP_DIM
    ng = FOURIER_GROUPS
    table, cos_c, sin_c = dft_tables(n_lat, gd)
    w_all = fourier_w.astype(jnp.float32).transpose(1, 0, 2).reshape(gd, ng * gd)
    folded = dense_matmul([(jnp.concatenate([cos_c, sin_c], axis=0), w_all)], tm=2 * gd, tn=512)
    w2 = jnp.concatenate([folded[:gd].reshape(gd, ng, gd), folded[gd:].reshape(gd, ng, gd)], axis=2)
    w2 = w2.transpose(1, 0, 2).astype(jnp.bfloat16)
    tm = min(DENSE_TM, n_lat)
    tiles = n_lat // tm
    z = pl.pallas_call(
        _fourier_in_body,
        out_shape=jax.ShapeDtypeStruct((2, n_lat, n_batch * ng * gd), jnp.bfloat16),
        grid=(n_batch, ng, tiles),
        in_specs=[pl.BlockSpec((tm, gd), lambda b, g, i: (b * tiles + i, g)),
                  pl.BlockSpec((1, gd, 2 * gd), lambda b, g, i: (g, 0, 0))],
        out_specs=pl.BlockSpec((2, tm, gd), lambda b, g, i: (0, i, b * ng + g)),
        compiler_params=_cparams("parallel", "parallel", "parallel"),
        name="fourier_in",
    )(proj, w2)
    z = z.reshape(2 * n_lat, n_batch * ng * gd)
    tn = 512
    per_b = ng * gd // tn
    return dense_matmul([(table, z)], out_dtype=jnp.bfloat16, tm=512, tn=tn,
                        out_shape=(n_batch * n_lat, ng * gd),
                        out_index=lambda i, j: ((j // per_b) * (n_lat // 512) + i, j % per_b))


def modulation(c, c_ctx, mod_w, mod_b):
    b = c.shape[0]
    rows = jnp.concatenate([c, c_ctx[None, :], jnp.zeros((SUBLANES - b - 1, c.shape[1]), c.dtype)], axis=0)
    return dense_matmul([(jax.nn.silu(rows), mod_w)], tm=SUBLANES, tn=512) + mod_b


def kernel(x, c, ctx, c_ctx, mod_w, mod_b, norm1_w, norm2_w, ab_w_in, pool_w, pool_scale, gdn_conv_w,
           gdn_a_log, gdn_dt_bias, gdn_norm_w, ab_w_out, cd_w_in, fourier_w, gla_gate_up, gla_gate_b,
           gla_norm_w, cd_w_out, moe_group_w, moe_group_b, moe_expert_w, moe_expert_b, moe_w1, moe_w3,
           moe_w2, final_norm_w):
    assert DEPTH == 2
    n_batch, n_lat, d = x.shape
    n_ctx = ctx.shape[1]
    assert n_ctx == ROW_BLOCK and n_lat % DENSE_TM == 0 and (n_batch * n_ctx) % DENSE_TM == 0
    t_lat = n_batch * n_lat
    blocks_per_seq = n_lat // ROW_BLOCK
    n_lat_blocks = n_batch * blocks_per_seq
    xs = jnp.concatenate([x.reshape(t_lat, d), ctx.reshape(n_batch * n_ctx, d)], axis=0)

    def mod_row(rows_per_tile):
        tiles_per_seq = n_lat // rows_per_tile
        return lambda i: jnp.minimum(i // tiles_per_seq, n_batch)

    bf = jnp.bfloat16
    mods = jnp.split(modulation(c, c_ctx, mod_w[0], mod_b[0]), N_MOD, axis=-1)
    sh1, sc1, g1, sh2, sc2, g2 = (m[:, None, :] for m in mods)
    h = norm_modulate(xs, norm1_w[0], sh1, sc1, mod_row(512))
    w_in = ab_w_in[0].astype(bf)
    proj = dense_matmul([(h, w_in[:, :AB_MAIN])])
    tail = dense_matmul([(h, _pad_cols(w_in[:, AB_MAIN:]))])
    gates = gdn_gates(tail, gdn_a_log[0], gdn_dt_bias[0], GDN_HEADS)
    u, w, qd, ke, at, ge = gdn_prepare(proj, gates, gdn_conv_w[0], GDN_HEADS, POOL_DIM, blocks_per_seq, n_lat_blocks)
    o_f, o_b = gdn_scan(u, w, qd, ke, at, ge, GDN_HEADS, n_batch, blocks_per_seq)
    y_gdn = gated_head_norm(o_f, o_b, proj, POOL_DIM + 3 * GDN_DIM, gdn_norm_w[0], GDN_HEAD_DIM)
    y_pool = pool_mix(proj, pool_w[0], pool_scale[0], n_lat_blocks)
    w_out = ab_w_out[0].astype(bf)
    xs = dense_matmul([(y_pool, w_out[:POOL_DIM]), (y_gdn, w_out[POOL_DIM:])], residual=xs, gates=g1,
                      gate_index=mod_row(DENSE_TM))
    f = norm_modulate(xs, norm2_w[0], sh2, sc2, mod_row(512))
    y = hier_moe(f, moe_group_w[0], moe_group_b[0], moe_expert_w[0], moe_expert_b[0],
                 moe_w1[0].astype(bf), moe_w3[0].astype(bf), moe_w2[0].astype(bf))
    g2_rows = jnp.concatenate([jnp.repeat(g2[:n_batch, 0], n_lat, axis=0),
                               jnp.broadcast_to(g2[n_batch], (n_batch * n_ctx, d))], axis=0)
    xs = xs + g2_rows * y

    mods = jnp.split(modulation(c, c_ctx, mod_w[1], mod_b[1]), N_MOD, axis=-1)
    sh1, sc1, g1, sh2, sc2, g2 = (m[:, None, :] for m in mods)
    h = norm_modulate(xs, norm1_w[1], sh1, sc1, mod_row(512))
    w_in = cd_w_in[0].astype(bf)
    proj = dense_matmul([(h, w_in[:, :CD_MAIN])])
    lr_tail = dense_matmul([(h, _pad_cols(w_in[:, CD_MAIN:]))])
    qd, ke, at, ge, v16 = gla_prepare(proj, lr_tail, gla_gate_up[0], gla_gate_b[0])
    o_f, o_b = gla_scan(qd, ke, at, ge, v16, n_batch, blocks_per_seq)
    y_gla = gated_head_norm(o_f, o_b, proj, CD_SPLITS[3], gla_norm_w[0], GLA_HEAD_V, m=t_lat)
    y_four = fourier_mix(proj, fourier_w[0], n_batch, n_lat)
    w_out = cd_w_out[0].astype(bf)
    xl = dense_matmul([(y_four, w_out[:FOURIER_DIM]), (y_gla, w_out[FOURIER_DIM:])], m=t_lat, residual=xs,
                      gates=g1, gate_index=mod_row(DENSE_TM))
    f = norm_modulate(xl, norm2_w[1], sh2, sc2, mod_row(512))
    y = hier_moe(f, moe_group_w[1], moe_group_b[1], moe_expert_w[1], moe_expert_b[1],
                 moe_w1[1].astype(bf), moe_w3[1].astype(bf), moe_w2[1].astype(bf))
    xl = xl + jnp.repeat(g2[:n_batch, 0], n_lat, axis=0) * y
    x32 = xl * lax.rsqrt(jnp.mean(xl * xl, axis=-1, keepdims=True) + NORM_EPS) * final_norm_w
    return x32.reshape(n_batch, n_lat, d)
```

```python
import functools

import jax
import jax.numpy as jnp
from jax import lax
from jax.experimental import pallas as pl
from jax.experimental.pallas import tpu as pltpu

D_MODEL = 4096
DEPTH = 2
CTX_LEN = 256
GRID_W = 64
N_MOD = 6
NORM_EPS = 1e-6

POOL_WINDOWS = (2, 4, 8, 16)
POOL_GROUPS = 4
POOL_GROUP_DIM = D_MODEL // 16
POOL_DIM = POOL_GROUPS * POOL_GROUP_DIM

GDN_HEAD_DIM = 128
GDN_DIM = D_MODEL - POOL_DIM
GDN_HEADS = GDN_DIM // GDN_HEAD_DIM
GDN_CONV = 5
GDN_CHUNK = 64

FOURIER_GROUPS = 4
FOURIER_GROUP_DIM = D_MODEL // 16
FOURIER_DIM = FOURIER_GROUPS * FOURIER_GROUP_DIM

GLA_HEADS = 6
GLA_V_DIM = D_MODEL - FOURIER_DIM
GLA_K_DIM = GLA_V_DIM // 2
GLA_HEAD_K = GLA_K_DIM // GLA_HEADS
GLA_HEAD_V = GLA_V_DIM // GLA_HEADS
GLA_GATE_RANK = 16
GLA_GATE_TEMP = 16.0
GLA_CHUNK = 32

MOE_GROUPS = 4
MOE_EXPERTS_PER_GROUP = 8
MOE_EXPERTS = MOE_GROUPS * MOE_EXPERTS_PER_GROUP
MOE_TOP_K = 2
MOE_HIDDEN = D_MODEL // 8

AB_MAIN = POOL_DIM + 4 * GDN_DIM
CD_SPLITS = (FOURIER_DIM, FOURIER_DIM + GLA_K_DIM, FOURIER_DIM + 2 * GLA_K_DIM,
             FOURIER_DIM + 2 * GLA_K_DIM + GLA_V_DIM, FOURIER_DIM + 2 * GLA_K_DIM + 2 * GLA_V_DIM)
CD_MAIN = FOURIER_DIM + 2 * GLA_K_DIM + 2 * GLA_V_DIM

LANES = 128
SUBLANES = 8
VMEM_LIMIT = 48 * 1024 * 1024
MOE_UP_VMEM_LIMIT = 56 * 1024 * 1024
MOE_BLOCK_ROWS = 256
ROW_BLOCK = 256
DENSE_TM = 1024
GDN_PREP_HEADS = 4
GDN_SCAN_HEADS = 8


def _cparams(*sem):
    return pltpu.CompilerParams(dimension_semantics=sem, vmem_limit_bytes=VMEM_LIMIT)


def _dot(a, b):
    return jnp.dot(a, b, preferred_element_type=jnp.float32)


def _dot_nt(a, b):
    return lax.dot_general(a, b, (((1,), (1,)), ((), ())), preferred_element_type=jnp.float32)


def _dot_tn(a, b):
    return lax.dot_general(a, b, (((0,), (0,)), ((), ())), preferred_element_type=jnp.float32)


def _bf(x):
    return x.astype(jnp.bfloat16)


def _split3(x):
    hi = _bf(x)
    r1 = x - hi.astype(jnp.float32)
    mid = _bf(r1)
    lo = _bf(r1 - mid.astype(jnp.float32))
    return hi, mid, lo


def _mm_body(*refs, n_pairs, has_res):
    o_ref = refs[-1]
    acc = _dot(refs[0][...], refs[1][...])
    for p in range(1, n_pairs):
        acc = acc + _dot(refs[2 * p][...], refs[2 * p + 1][...])
    if has_res:
        res_ref, gate_ref = refs[2 * n_pairs], refs[2 * n_pairs + 1]
        acc = res_ref[...] + gate_ref[0] * acc
    o_ref[...] = acc.astype(o_ref.dtype)


def dense_matmul(pairs, out_dtype=jnp.float32, tm=DENSE_TM, tn=512, m=None, n=None, w_row_block=0, residual=None,
                 gates=None, gate_index=None, out_shape=None, out_index=None):
    m = pairs[0][0].shape[0] if m is None else m
    n = pairs[0][1].shape[1] if n is None else n
    tm = min(tm, m)
    tn = min(tn, n)
    assert m % tm == 0 and n % tn == 0, (m, n, tm, tn)
    in_specs, args = [], []
    for x, w in pairs:
        k = x.shape[1]
        assert w.shape[0] % k == 0
        in_specs += [pl.BlockSpec((tm, k), lambda i, j: (i, 0)),
                     pl.BlockSpec((k, tn), lambda i, j: (w_row_block, j))]
        args += [x, w]
    if residual is not None:
        in_specs += [pl.BlockSpec((tm, tn), lambda i, j: (i, j)),
                     pl.BlockSpec((1, 1, tn), lambda i, j: (gate_index(i), 0, j))]
        args += [residual, gates]
    return pl.pallas_call(
        functools.partial(_mm_body, n_pairs=len(pairs), has_res=residual is not None),
        out_shape=jax.ShapeDtypeStruct((m, n) if out_shape is None else out_shape, out_dtype),
        grid=(m // tm, n // tn),
        in_specs=in_specs,
        out_specs=pl.BlockSpec((tm, tn), (lambda i, j: (i, j)) if out_index is None else out_index),
        compiler_params=_cparams("parallel", "parallel"),
        name="dense_matmul",
    )(*args)


def _pad_cols(w, mult=LANES):
    pad = (-w.shape[1]) % mult
    return jnp.pad(w, ((0, 0), (0, pad))) if pad else w


def _norm_mod_body(x_ref, nw_ref, shift_ref, scale_ref, o_ref):
    x = x_ref[...]
    y = x * lax.rsqrt(jnp.mean(x * x, axis=-1, keepdims=True) + NORM_EPS) * nw_ref[...]
    o_ref[...] = (y * (1.0 + scale_ref[0]) + shift_ref[0]).astype(o_ref.dtype)


def norm_modulate(x, norm_w, shift, scale, row_index, m=None, tm=512):
    m = x.shape[0] if m is None else m
    d = x.shape[1]
    vec = pl.BlockSpec((1, 1, d), lambda i: (row_index(i), 0, 0))
    return pl.pallas_call(
        _norm_mod_body,
        out_shape=jax.ShapeDtypeStruct((m, d), jnp.bfloat16),
        grid=(m // tm,),
        in_specs=[pl.BlockSpec((tm, d), lambda i: (i, 0)), pl.BlockSpec((1, d), lambda i: (0, 0)), vec, vec],
        out_specs=pl.BlockSpec((tm, d), lambda i: (i, 0)),
        compiler_params=_cparams("parallel"),
        name="norm_modulate",
    )(x, norm_w.reshape(1, d), shift, scale)


def _gdn_gates_body(tail_ref, par_ref, o_ref, ot_ref, *, n_heads, chunk):
    t = tail_ref[...]
    rows = t.shape[0]
    lane = lax.broadcasted_iota(jnp.int32, t.shape, 1)
    a_row = par_ref[0:1, :]
    dtb_row = par_ref[1:2, :]
    beta = jax.nn.sigmoid(t)
    z = t + dtb_row
    g = a_row * (jnp.maximum(z, 0.0) + jnp.log1p(jnp.exp(-jnp.abs(z))))
    ri = lax.broadcasted_iota(jnp.int32, (rows, rows), 0)
    ci = lax.broadcasted_iota(jnp.int32, (rows, rows), 1)
    shift = chunk.bit_length() - 1
    same = (ri >> shift) == (ci >> shift)
    pre = _bf(jnp.where(same & (ci <= ri), 1.0, 0.0))
    suf = _bf(jnp.where(same & (ci >= ri), 1.0, 0.0))
    ones = _bf(jnp.where(same, 1.0, 0.0))
    parts = _split3(g)
    cf = _dot(pre, parts[0]) + _dot(pre, parts[1]) + _dot(pre, parts[2])
    cb = _dot(suf, parts[0]) + _dot(suf, parts[1]) + _dot(suf, parts[2])
    tot = _dot(ones, parts[0]) + _dot(ones, parts[1]) + _dot(ones, parts[2])
    main = jnp.where(lane < 2 * n_heads, beta,
                     jnp.where(lane < 3 * n_heads, cf, jnp.where(lane < 4 * n_heads, cb, 0.0)))
    o_ref[:, :LANES] = main
    o_ref[:, LANES:] = tot
    ot_ref[...] = main.T


def gdn_gates(tail, a_log, dt_bias, n_heads):
    t = tail.shape[0]
    par = jnp.zeros((SUBLANES, LANES), jnp.float32)
    par = par.at[0, 2 * n_heads:4 * n_heads].set(-jnp.exp(a_log.astype(jnp.float32)).reshape(-1))
    par = par.at[1, 2 * n_heads:4 * n_heads].set(dt_bias.astype(jnp.float32).reshape(-1))
    return pl.pallas_call(
        functools.partial(_gdn_gates_body, n_heads=n_heads, chunk=GDN_CHUNK),
        out_shape=(jax.ShapeDtypeStruct((t, 2 * LANES), jnp.float32),
                   jax.ShapeDtypeStruct((t // ROW_BLOCK * LANES, ROW_BLOCK), jnp.float32)),
        grid=(t // ROW_BLOCK,),
        in_specs=[pl.BlockSpec((ROW_BLOCK, LANES), lambda i: (i, 0)),
                  pl.BlockSpec((SUBLANES, LANES), lambda i: (0, 0))],
        out_specs=(pl.BlockSpec((ROW_BLOCK, 2 * LANES), lambda i: (i, 0)),
                   pl.BlockSpec((LANES, ROW_BLOCK), lambda i: (i, 0))),
        compiler_params=_cparams("parallel"),
        name="gdn_gates",
    )(tail, par)


def _gdn_prep_body(q_ref, k_ref, v_ref, qp_ref, kp_ref, vp_ref, qn_ref, kn_ref, vn_ref,
                   wq_ref, wk_ref, wv_ref, gt_ref, gtt_ref,
                   u_ref, w_ref, qd_ref, ke_ref, at_ref, ge_ref, *, heads, n_heads, blocks_per_seq, n_lat_blocks):
    rb = pl.program_id(0)
    hg = pl.program_id(1)
    n = ROW_BLOCK
    c = GDN_CHUNK
    n_ch = n // c
    cs = c.bit_length() - 1
    hd = GDN_HEAD_DIM
    is_ctx = rb >= n_lat_blocks
    first = jnp.logical_or(is_ctx, rb % blocks_per_seq == 0)
    last = jnp.logical_or(is_ctx, rb % blocks_per_seq == blocks_per_seq - 1)
    m_prev = jnp.where(first, 0.0, 1.0)
    m_next = jnp.where(last, 0.0, 1.0)

    def conv_silu(cur_ref, prev_ref, next_ref, cw_ref):
        ext = jnp.concatenate([prev_ref[...] * m_prev, cur_ref[...], next_ref[...] * m_next], axis=0)
        acc = None
        for j in range(GDN_CONV):
            lo = SUBLANES - GDN_CONV // 2 + j
            term = ext[lo:lo + ROW_BLOCK, :] * cw_ref[j:j + 1, :]
            acc = term if acc is None else acc + term
        return acc * jax.nn.sigmoid(acc)

    qc = conv_silu(q_ref, qp_ref, qn_ref, wq_ref)
    kc = conv_silu(k_ref, kp_ref, kn_ref, wk_ref)
    vc = conv_silu(v_ref, vp_ref, vn_ref, wv_ref)

    gt_lo = gt_ref[:, :LANES]
    gt_hi = gt_ref[:, LANES:]
    lane = lax.broadcasted_iota(jnp.int32, (n, LANES), 1)
    blk16 = _bf(jnp.where((lax.broadcasted_iota(jnp.int32, (n, n), 0) >> cs)
                          == (lax.broadcasted_iota(jnp.int32, (n, n), 1) >> cs), 1.0, 0.0))
    wr = lax.broadcasted_iota(jnp.int32, (c, n), 0)
    wc = lax.broadcasted_iota(jnp.int32, (c, n), 1)
    eye_wide = jnp.where(wr == (wc & (c - 1)), 1.0, 0.0)
    sr = lax.broadcasted_iota(jnp.int32, (c, LANES), 0)
    sc = lax.broadcasted_iota(jnp.int32, (c, LANES), 1)
    slab_masks = []
    for odd in range(LANES // c):
        rel = sc - odd * c
        own = (sc >> cs) == odd
        slab_masks.append(((own & (rel <= sr), own & (rel < sr)), (own & (rel >= sr), own & (rel > sr))))

    def tall(wide):
        return jnp.concatenate([_bf(wide)] * n_ch, axis=0) * blk16

    def column(src, idx):
        return jnp.sum(jnp.where(lane == idx, src, 0.0), axis=1, keepdims=True)

    per_tile = LANES // c
    for gi in range(heads):
        h = hg * heads + gi
        sl = slice(gi * hd, (gi + 1) * hd)
        qh, kh, v = qc[:, sl], kc[:, sl], vc[:, sl]
        q = qh * lax.rsqrt(jnp.sum(qh * qh, axis=-1, keepdims=True) + NORM_EPS) * (hd ** -0.5)
        k = kh * lax.rsqrt(jnp.sum(kh * kh, axis=-1, keepdims=True) + NORM_EPS)
        k16 = _bf(k)
        qk = _dot_nt(_bf(q), k16)
        kk = _dot_nt(k16, k16)
        for d in range(2):
            beta = column(gt_lo, d * n_heads + h)
            gc = column(gt_lo, (2 + d) * n_heads + h)
            tot = column(gt_hi, (2 + d) * n_heads + h)
            gc_row = gtt_ref[pl.ds((2 + d) * n_heads + h, 1), :]
            p_tiles = []
            for j in range(n_ch):
                rows = slice(j * c, (j + 1) * c)
                cols = slice((j // per_tile) * LANES, (j // per_tile + 1) * LANES)
                incl, strict = slab_masks[j % per_tile][d]
                decay = jnp.where(incl, jnp.exp(jnp.where(incl, gc[rows] - gc_row[:, cols], 0.0)), 0.0)
                a = jnp.where(strict, (beta[rows] * kk[rows, cols]) * decay, 0.0)
                if j % per_tile == 0:
                    p_tiles.append(-a)
                else:
                    p_tiles[-1] = p_tiles[-1] - a
                lo = (j % per_tile) * c
                at_ref[d, rows, gi * c:(gi + 1) * c] = _bf((qk[rows, cols] * decay)[:, lo:lo + c])
                ge_ref[d, j * SUBLANES:(j + 1) * SUBLANES, sl] = jnp.broadcast_to(
                    jnp.exp(tot[j * c:j * c + SUBLANES]), (SUBLANES, hd))
            p = jnp.concatenate(p_tiles, axis=1)
            t = eye_wide + p
            p_tall = tall(p)
            span = 2
            while span < c:
                p = _dot(_bf(p), p_tall)
                p_tall = tall(p)
                t = t + _dot(_bf(t), p_tall)
                span *= 2
            eg = jnp.exp(gc)
            kbeta = k * beta
            sol = _dot(tall(t), _bf(jnp.concatenate([v * beta, kbeta * eg], axis=1)))
            u_ref[d, :, sl] = sol[:, :hd]
            w_ref[d, :, sl] = _bf(sol[:, hd:])
            qd_ref[d, :, sl] = _bf(q * eg)
            ke_ref[d, :, sl] = _bf(k * jnp.exp(tot - gc))


def gdn_prepare(proj, gates, gates_t, conv_w, n_heads, qkv_col0, blocks_per_seq, n_lat_blocks):
    t = proj.shape[0]
    hd = GDN_HEAD_DIM
    g = GDN_PREP_HEADS
    gw = g * hd
    dim = n_heads * hd
    nb = t // ROW_BLOCK
    cb0 = qkv_col0 // gw
    per = dim // gw
    halo_per_block = ROW_BLOCK // SUBLANES

    def cur(off):
        return pl.BlockSpec((ROW_BLOCK, gw), lambda rb, hg: (rb, cb0 + off * per + hg))

    def prev(off):
        return pl.BlockSpec((SUBLANES, gw),
                            lambda rb, hg: (jnp.maximum(rb * halo_per_block - 1, 0), cb0 + off * per + hg))

    def nxt(off):
        return pl.BlockSpec((SUBLANES, gw),
                            lambda rb, hg: (jnp.minimum((rb + 1) * halo_per_block, nb * halo_per_block - 1),
                                            cb0 + off * per + hg))

    def cw(off):
        return pl.BlockSpec((GDN_CONV, gw), lambda rb, hg: (0, off * per + hg))

    out_tok = lambda width: pl.BlockSpec((2, ROW_BLOCK, width), lambda rb, hg: (0, rb, hg))
    outs = pl.pallas_call(
        functools.partial(_gdn_prep_body, heads=g, n_heads=n_heads, blocks_per_seq=blocks_per_seq,
                          n_lat_blocks=n_lat_blocks),
        out_shape=(jax.ShapeDtypeStruct((2, t, dim), jnp.float32),
                   jax.ShapeDtypeStruct((2, t, dim), jnp.bfloat16),
                   jax.ShapeDtypeStruct((2, t, dim), jnp.bfloat16),
                   jax.ShapeDtypeStruct((2, t, dim), jnp.bfloat16),
                   jax.ShapeDtypeStruct((2, t, n_heads * GDN_CHUNK), jnp.bfloat16),
                   jax.ShapeDtypeStruct((2, t // SUBLANES, dim), jnp.float32)),
        grid=(nb, n_heads // g),
        in_specs=[cur(0), cur(1), cur(2), prev(0), prev(1), prev(2), nxt(0), nxt(1), nxt(2),
                  cw(0), cw(1), cw(2), pl.BlockSpec((ROW_BLOCK, 2 * LANES), lambda rb, hg: (rb, 0)),
                  pl.BlockSpec((LANES, ROW_BLOCK), lambda rb, hg: (rb, 0))],
        out_specs=(out_tok(gw), out_tok(gw), out_tok(gw), out_tok(gw), out_tok(g * GDN_CHUNK),
                   pl.BlockSpec((2, ROW_BLOCK // SUBLANES, gw), lambda rb, hg: (0, rb, hg))),
        compiler_params=_cparams("parallel", "parallel"),
        name="gdn_prepare",
    )(proj, proj, proj, proj, proj, proj, proj, proj, proj, conv_w, conv_w, conv_w, gates, gates_t)
    return outs


def _gdn_scan_body(uf, wf, qf, kf, af, gf, ub, wb, qb, kb, ab, gb, of_ref, ob_ref, s_ref, *, heads):
    t = pl.program_id(2)
    c = GDN_CHUNK
    hd = GDN_HEAD_DIM
    n_ch = ROW_BLOCK // c

    @pl.when(t == 0)
    def _():
        s_ref[...] = jnp.zeros_like(s_ref)

    pw = 2 * hd
    s_mask = ((lax.broadcasted_iota(jnp.int32, (pw, pw), 0) >> (hd.bit_length() - 1))
              == (lax.broadcasted_iota(jnp.int32, (pw, pw), 1) >> (hd.bit_length() - 1)))
    v_mask = ((lax.broadcasted_iota(jnp.int32, (2 * c, pw), 0) >> (c.bit_length() - 1))
              == (lax.broadcasted_iota(jnp.int32, (2 * c, pw), 1) >> (hd.bit_length() - 1)))
    views = ((uf, wf, qf, kf, af, gf, of_ref), (ub, wb, qb, kb, ab, gb, ob_ref))
    for step in range(n_ch):
        for d in range(2):
            u_r, w_r, q_r, k_r, a_r, g_r, o_r = views[d]
            ch = step if d == 0 else n_ch - 1 - step
            rows = slice(ch * c, (ch + 1) * c)
            for p in range(heads // 2):
                sl = slice(p * pw, (p + 1) * pw)
                s = s_ref[d, p]
                s_diag = _bf(jnp.where(s_mask, jnp.concatenate([s, s], axis=0), 0.0))
                wq = jnp.concatenate([w_r[0, rows, sl], q_r[0, rows, sl]], axis=0)
                r = _dot(wq, s_diag)
                v_new = u_r[0, rows, sl] - r[:c]
                v_diag = _bf(jnp.where(v_mask, jnp.concatenate([v_new, v_new], axis=0), 0.0))
                o_r[0, rows, sl] = r[c:] + _dot(a_r[0, rows, p * 2 * c:(p + 1) * 2 * c], v_diag)
                k_stack = jnp.concatenate([k_r[0, rows, p * pw:p * pw + hd], k_r[0, rows, p * pw + hd:(p + 1) * pw]],
                                          axis=0)
                ge = g_r[0, ch * SUBLANES:ch * SUBLANES + 1, sl]
                s_ref[d, p] = s * ge + _dot_tn(k_stack, v_diag)


def gdn_scan(u, w, qd, ke, at, ge, n_heads, n_batch, blocks_per_seq):
    t = u.shape[1]
    hd = GDN_HEAD_DIM
    g = GDN_SCAN_HEADS
    gw = g * hd
    n_lat_blocks = n_batch * blocks_per_seq

    def fwd(b, s):
        return jnp.where(s == 0, n_lat_blocks + b, b * blocks_per_seq + s - 1)

    def bwd(b, s):
        return jnp.where(s == 0, n_lat_blocks + b, b * blocks_per_seq + blocks_per_seq - s)

    def tok(d, width, blk):
        return pl.BlockSpec((1, ROW_BLOCK, width), lambda b, hg, s: (d, blk(b, s), hg))

    def gsp(d, blk):
        return pl.BlockSpec((1, ROW_BLOCK // SUBLANES, gw), lambda b, hg, s: (d, blk(b, s), hg))

    in_specs = []
    for d, blk in ((0, fwd), (1, bwd)):
        in_specs += [tok(d, gw, blk), tok(d, gw, blk), tok(d, gw, blk), tok(d, gw, blk),
                     tok(d, g * GDN_CHUNK, blk), gsp(d, blk)]
    o_f, o_b = pl.pallas_call(
        functools.partial(_gdn_scan_body, heads=g),
        out_shape=(jax.ShapeDtypeStruct((1, t, n_heads * hd), jnp.float32),) * 2,
        grid=(n_batch, n_heads // g, blocks_per_seq + 1),
        in_specs=in_specs,
        out_specs=(pl.BlockSpec((1, ROW_BLOCK, gw), lambda b, hg, s: (0, fwd(b, s), hg)),
                   pl.BlockSpec((1, ROW_BLOCK, gw), lambda b, hg, s: (0, bwd(b, s), hg))),
        scratch_shapes=[pltpu.VMEM((2, g // 2, hd, 2 * hd), jnp.float32)],
        compiler_params=_cparams("parallel", "parallel", "arbitrary"),
        name="gdn_scan",
    )(u, w, qd, ke, at, ge, u, w, qd, ke, at, ge)
    return o_f[0], o_b[0]


GATED_NORM_COLS = 1024


def _gated_norm_body(of_ref, ob_ref, z_ref, nw_ref, y_in_ref, y_ref, *, hd):
    del y_in_ref
    for h in range(GATED_NORM_COLS // hd):
        sl = slice(h * hd, (h + 1) * hd)
        o = of_ref[:, sl] + ob_ref[:, sl]
        y = o * lax.rsqrt(jnp.mean(o * o, axis=-1, keepdims=True) + NORM_EPS) * nw_ref[...]
        z = z_ref[:, sl]
        y_ref[:, sl] = (y * (z * jax.nn.sigmoid(z))).astype(y_ref.dtype)


def gated_head_norm(o_f, o_b, proj, z_col0, norm_w, hd, y_init, y_col0):
    m = y_init.shape[0]
    dim = o_f.shape[1]
    cw = GATED_NORM_COLS
    assert z_col0 % cw == 0 and dim % cw == 0 and cw % hd == 0 and y_col0 % cw == 0
    return pl.pallas_call(
        functools.partial(_gated_norm_body, hd=hd),
        out_shape=jax.ShapeDtypeStruct(y_init.shape, y_init.dtype),
        grid=(m // ROW_BLOCK, dim // cw),
        in_specs=[pl.BlockSpec((ROW_BLOCK, cw), lambda i, j: (i, j)),
                  pl.BlockSpec((ROW_BLOCK, cw), lambda i, j: (i, j)),
                  pl.BlockSpec((ROW_BLOCK, cw), lambda i, j: (i, z_col0 // cw + j)),
                  pl.BlockSpec((1, hd), lambda i, j: (0, 0)),
                  pl.BlockSpec(memory_space=pl.ANY)],
        out_specs=pl.BlockSpec((ROW_BLOCK, cw), lambda i, j: (i, y_col0 // cw + j)),
        input_output_aliases={4: 0},
        compiler_params=_cparams("parallel", "parallel"),
        name="gated_head_norm",
    )(o_f, o_b, proj, norm_w.reshape(1, hd), y_init)


def _pool_body(u_ref, w_ref, sc_ref, y_ref, *, n_lat_blocks):
    rb = pl.program_id(0)
    is_ctx = rb >= n_lat_blocks
    row_len = jnp.where(is_ctx, CTX_LEN, GRID_W)
    shift = jnp.where(is_ctx, CTX_LEN.bit_length() - 1, GRID_W.bit_length() - 1)
    n = ROW_BLOCK
    ri = lax.broadcasted_iota(jnp.int32, (n, n), 0)
    ci = lax.broadcasted_iota(jnp.int32, (n, n), 1)
    same = (ri >> shift) == (ci >> shift)
    pos_r = ri & (row_len - 1)
    pos_c = ci & (row_len - 1)
    rcol = lax.broadcasted_iota(jnp.int32, (n, 1), 0) & (row_len - 1)
    gd = POOL_GROUP_DIM
    for g, win in enumerate(POOL_WINDOWS):
        half = win // 2
        lo = jnp.maximum(pos_r - half, 0)
        hi = jnp.minimum(pos_r + half - 1, row_len - 1)
        band = _bf(jnp.where(same & (pos_c >= lo) & (pos_c <= hi), 1.0, 0.0))
        cnt = (jnp.minimum(rcol + half - 1, row_len - 1) - jnp.maximum(rcol - half, 0) + 1).astype(jnp.float32)
        u = u_ref[:, g * gd:(g + 1) * gd]
        parts = _split3(u)
        win_sum = _dot(band, parts[0]) + _dot(band, parts[1]) + _dot(band, parts[2])
        dlt = win_sum / cnt - u
        y = _dot(_bf(dlt), _bf(w_ref[g])) * sc_ref[:, g * gd:(g + 1) * gd]
        y_ref[:, g * gd:(g + 1) * gd] = y.astype(y_ref.dtype)


def pool_mix(proj, pool_w, pool_scale, n_lat_blocks, out_cols=POOL_DIM):
    t = proj.shape[0]
    return pl.pallas_call(
        functools.partial(_pool_body, n_lat_blocks=n_lat_blocks),
        out_shape=jax.ShapeDtypeStruct((t, out_cols), jnp.bfloat16),
        grid=(t // ROW_BLOCK,),
        in_specs=[pl.BlockSpec((ROW_BLOCK, POOL_DIM), lambda i: (i, 0)),
                  pl.BlockSpec((POOL_GROUPS, POOL_GROUP_DIM, POOL_GROUP_DIM), lambda i: (0, 0, 0)),
                  pl.BlockSpec((1, POOL_DIM), lambda i: (0, 0))],
        out_specs=pl.BlockSpec((ROW_BLOCK, POOL_DIM), lambda i: (i, 0)),
        compiler_params=_cparams("parallel"),
        name="pool_mix",
    )(proj, pool_w, pool_scale.reshape(1, POOL_DIM))


def _moe_up_body(be_ref, first_ref, nused_ref, x_ref, w1_ref, w3_ref, h_ref, w1_bf, w3_bf):
    del be_ref
    i = pl.program_id(0)

    @pl.when(first_ref[i] == 1)
    def _():
        w1_bf[...] = _bf(w1_ref[0])
        w3_bf[...] = _bf(w3_ref[0])

    @pl.when(i < nused_ref[0])
    def _():
        x = x_ref[...]
        a = _dot(x, w1_bf[...])
        b = _dot(x, w3_bf[...])
        h_ref[...] = _bf(a * jax.nn.sigmoid(a) * b)


def _moe_down_body(be_ref, first_ref, nused_ref, h_ref, w2_ref, o_ref, w2_bf):
    del be_ref
    i = pl.program_id(0)

    @pl.when(first_ref[i] == 1)
    def _():
        w2_bf[...] = _bf(w2_ref[0])

    @pl.when(i < nused_ref[0])
    def _():
        o_ref[...] = _dot(h_ref[...], w2_bf[...])


def moe_expert_blocks(xbuf, block_expert, block_first, n_used, w1, w3, w2):
    p, d = xbuf.shape
    bm = MOE_BLOCK_ROWS
    nb = p // bm
    hid = w1.shape[-1]
    up_spec = pltpu.PrefetchScalarGridSpec(
        num_scalar_prefetch=3,
        grid=(nb,),
        in_specs=[pl.BlockSpec((bm, d), lambda i, be, fi, nu: (i, 0)),
                  pl.BlockSpec((1, d, hid), lambda i, be, fi, nu: (be[i], 0, 0)),
                  pl.BlockSpec((1, d, hid), lambda i, be, fi, nu: (be[i], 0, 0))],
        out_specs=pl.BlockSpec((bm, hid), lambda i, be, fi, nu: (i, 0)),
        scratch_shapes=[pltpu.VMEM((d, hid), jnp.bfloat16)] * 2,
    )
    hbuf = pl.pallas_call(
        _moe_up_body,
        out_shape=jax.ShapeDtypeStruct((p, hid), jnp.bfloat16),
        grid_spec=up_spec,
        compiler_params=pltpu.CompilerParams(dimension_semantics=("arbitrary",), vmem_limit_bytes=MOE_UP_VMEM_LIMIT),
        name="moe_up",
    )(block_expert, block_first, n_used, xbuf, w1, w3)
    down_spec = pltpu.PrefetchScalarGridSpec(
        num_scalar_prefetch=3,
        grid=(nb,),
        in_specs=[pl.BlockSpec((bm, hid), lambda i, be, fi, nu: (i, 0)),
                  pl.BlockSpec((1, hid, d), lambda i, be, fi, nu: (be[i], 0, 0))],
        out_specs=pl.BlockSpec((bm, d), lambda i, be, fi, nu: (i, 0)),
        scratch_shapes=[pltpu.VMEM((hid, d), jnp.bfloat16)],
    )
    return pl.pallas_call(
        _moe_down_body,
        out_shape=jax.ShapeDtypeStruct((p, d), jnp.float32),
        grid_spec=down_spec,
        compiler_params=_cparams("arbitrary"),
        name="moe_down",
    )(block_expert, block_first, n_used, hbuf, w2)


def hier_moe(h, group_w, group_b, expert_w, expert_b, w1, w3, w2, expert0):
    t, d = h.shape
    rows = jnp.arange(t)
    router_w = jnp.concatenate([group_w, expert_w.transpose(1, 0, 2).reshape(d, MOE_EXPERTS)], axis=1)
    logits = jnp.dot(h.astype(jnp.float32), router_w, precision=lax.Precision.HIGHEST)
    group_logits = logits[:, :MOE_GROUPS] + group_b
    group = jnp.argmax(group_logits, axis=-1)
    p_group = jax.nn.softmax(group_logits, axis=-1)[rows, group][:, None]
    expert_logits = logits[:, MOE_GROUPS:].reshape(t, MOE_GROUPS, MOE_EXPERTS_PER_GROUP) + expert_b
    sel = expert_logits[rows, group]
    top_val, top_idx = lax.top_k(sel, MOE_TOP_K)
    weight = p_group * jax.nn.softmax(top_val, axis=-1)
    expert = (group[:, None] * MOE_EXPERTS_PER_GROUP + top_idx).reshape(-1)
    tk = t * MOE_TOP_K
    onehot = (expert[:, None] == jnp.arange(MOE_EXPERTS)[None, :]).astype(jnp.int32)
    counts = jnp.sum(onehot, axis=0)
    rank = jnp.sum((jnp.cumsum(onehot, axis=0) - onehot) * onehot, axis=1)
    bm = MOE_BLOCK_ROWS
    padded = (counts + bm - 1) // bm * bm
    pad_end = jnp.cumsum(padded)
    pad_start = pad_end - padded
    dest = pad_start[expert] + rank
    n_blocks = -(-tk // bm) + MOE_EXPERTS
    src = jnp.zeros((n_blocks * bm,), jnp.int32).at[dest].set(jnp.repeat(rows, MOE_TOP_K).astype(jnp.int32))
    block_expert = jnp.minimum(jnp.searchsorted(pad_end, jnp.arange(n_blocks) * bm, side='right'),
                               MOE_EXPERTS - 1).astype(jnp.int32)
    block_first = jnp.concatenate([jnp.ones((1,), jnp.int32),
                                   (block_expert[1:] != block_expert[:-1]).astype(jnp.int32)])
    n_used = (pad_end[-1:] // bm).astype(jnp.int32)
    xbuf = h[src]
    ybuf = moe_expert_blocks(xbuf, block_expert + expert0, block_first, n_used, w1, w3, w2)
    return ybuf[dest].reshape(t, MOE_TOP_K * d), weight


GLA_SUB = 32
GLA_STEP = 64


def _gla_prep_body(q_ref, k_ref, v_ref, lr_ref, wg_ref, gb_ref, qd_ref, ke_ref, at_ref, ge_ref, v16_ref):
    n = ROW_BLOCK
    c = GLA_STEP
    sb = GLA_SUB
    n_sb = n // sb
    q = q_ref[...] * (GLA_HEAD_K ** -0.5)
    k = k_ref[...]
    v16_ref[...] = _bf(v_ref[...])
    lr = lr_ref[...]
    ri = lax.broadcasted_iota(jnp.int32, (n, n), 0)
    ci = lax.broadcasted_iota(jnp.int32, (n, n), 1)
    cs = c.bit_length() - 1
    ss = sb.bit_length() - 1
    same_c = (ri >> cs) == (ci >> cs)
    same_s = (ri >> ss) == (ci >> ss)
    for d in range(2):
        x = jnp.dot(lr, wg_ref[d], preferred_element_type=jnp.float32, precision=lax.Precision.HIGHEST) + gb_ref[d]
        gk = (jnp.minimum(x, 0.0) - jnp.log1p(jnp.exp(-jnp.abs(x)))) * (1.0 / GLA_GATE_TEMP)
        run = _bf(jnp.where(same_c & ((ci <= ri) if d == 0 else (ci >= ri)), 1.0, 0.0))
        parts = _split3(gk)
        g = _dot(run, parts[0]) + _dot(run, parts[1]) + _dot(run, parts[2])
        off, tot_s, tot_c = [], [], []
        for b in range(n_sb):
            lo = b * sb
            inner = (b % 2 == 1) if d == 0 else (b % 2 == 0)
            edge = g[lo + sb - 1:lo + sb] if d == 0 else g[lo:lo + 1]
            if inner:
                prev = g[lo - 1:lo] if d == 0 else g[lo + sb:lo + sb + 1]
                off.append(jnp.broadcast_to(prev, (sb, prev.shape[1])))
                tot_s.append(jnp.broadcast_to(edge - prev, (sb, prev.shape[1])))
            else:
                off.append(jnp.zeros((sb, g.shape[1]), jnp.float32))
                tot_s.append(jnp.broadcast_to(edge, (sb, edge.shape[1])))
        for j in range(n // c):
            edge = g[j * c + c - 1:j * c + c] if d == 0 else g[j * c:j * c + 1]
            tot_c.append(jnp.broadcast_to(edge, (c, edge.shape[1])))
        off = jnp.concatenate(off, axis=0)
        tot_s = jnp.concatenate(tot_s, axis=0)
        tot_c = jnp.concatenate(tot_c, axis=0)
        loc = g - off
        qd_loc = _bf(q * jnp.exp(loc))
        m_diag = _dot_nt(qd_loc, _bf(k * jnp.exp(-loc)))
        m_cross = _dot_nt(qd_loc, _bf(k * jnp.exp(tot_s - loc)))
        tri = (ci <= ri) if d == 0 else (ci >= ri)
        nxt = ((ri >> ss) == (ci >> ss) + 1) if d == 0 else ((ri >> ss) + 1 == (ci >> ss))
        attn = jnp.where(same_s & tri, m_diag, 0.0) + jnp.where(same_c & nxt, m_cross, 0.0)
        qd_ref[d] = _bf(q * jnp.exp(g))
        ke_ref[d] = _bf(k * jnp.exp(tot_c - g))
        for j in range(n // c):
            at_ref[d, 0, j * c:(j + 1) * c, :] = _bf(attn[j * c:(j + 1) * c, j * c:(j + 1) * c])
            ge_ref[d, j * SUBLANES:(j + 1) * SUBLANES, :] = jnp.exp(tot_c[j * c:j * c + SUBLANES])


def gla_prepare(proj, lr_tail, gate_up, gate_b):
    t = proj.shape[0]
    hk, hv = GLA_HEAD_K, GLA_HEAD_V
    wg = jnp.zeros((2, LANES, GLA_K_DIM), jnp.float32)
    for d in range(2):
        wg = wg.at[d, d * GLA_GATE_RANK:(d + 1) * GLA_GATE_RANK].set(gate_up[d].astype(jnp.float32))
    gb = gate_b.astype(jnp.float32).reshape(2, 1, GLA_K_DIM)
    q0, k0, v0 = CD_SPLITS[0] // hk, CD_SPLITS[1] // hk, CD_SPLITS[2] // hv
    assert CD_SPLITS[0] % hk == 0 and CD_SPLITS[1] % hk == 0 and CD_SPLITS[2] % hv == 0
    return pl.pallas_call(
        _gla_prep_body,
        out_shape=(jax.ShapeDtypeStruct((2, t, GLA_K_DIM), jnp.bfloat16),
                   jax.ShapeDtypeStruct((2, t, GLA_K_DIM), jnp.bfloat16),
                   jax.ShapeDtypeStruct((2, GLA_HEADS, t, GLA_STEP), jnp.bfloat16),
                   jax.ShapeDtypeStruct((2, t // SUBLANES, GLA_K_DIM), jnp.float32),
                   jax.ShapeDtypeStruct((t, GLA_V_DIM), jnp.bfloat16)),
        grid=(t // ROW_BLOCK, GLA_HEADS),
        in_specs=[pl.BlockSpec((ROW_BLOCK, hk), lambda rb, h: (rb, q0 + h)),
                  pl.BlockSpec((ROW_BLOCK, hk), lambda rb, h: (rb, k0 + h)),
                  pl.BlockSpec((ROW_BLOCK, hv), lambda rb, h: (rb, v0 + h)),
                  pl.BlockSpec((ROW_BLOCK, LANES), lambda rb, h: (rb, 0)),
                  pl.BlockSpec((2, LANES, hk), lambda rb, h: (0, 0, h)),
                  pl.BlockSpec((2, 1, hk), lambda rb, h: (0, 0, h))],
        out_specs=(pl.BlockSpec((2, ROW_BLOCK, hk), lambda rb, h: (0, rb, h)),
                   pl.BlockSpec((2, ROW_BLOCK, hk), lambda rb, h: (0, rb, h)),
                   pl.BlockSpec((2, 1, ROW_BLOCK, GLA_STEP), lambda rb, h: (0, h, rb, 0)),
                   pl.BlockSpec((2, ROW_BLOCK // SUBLANES, hk), lambda rb, h: (0, rb, h)),
                   pl.BlockSpec((ROW_BLOCK, hv), lambda rb, h: (rb, h))),
        compiler_params=_cparams("parallel", "parallel"),
        name="gla_prepare",
    )(proj, proj, proj, lr_tail, wg, gb)


def _gla_scan_body(qf, kf, af, gf, vf, qb, kb, ab, gb, vb, of_ref, ob_ref, s_ref):
    t = pl.program_id(2)
    c = GLA_STEP
    n_ch = ROW_BLOCK // c

    @pl.when(t == 0)
    def _():
        s_ref[...] = jnp.zeros_like(s_ref)

    views = ((qf, kf, af, gf, vf, of_ref), (qb, kb, ab, gb, vb, ob_ref))
    for step in range(n_ch):
        for d in range(2):
            q_r, k_r, a_r, g_r, v_r, o_r = views[d]
            ch = step if d == 0 else n_ch - 1 - step
            rows = slice(ch * c, (ch + 1) * c)
            st = s_ref[d]
            v16 = v_r[rows, :]
            o_r[rows, :] = _dot_nt(q_r[0, rows, :], _bf(st)) + _dot(a_r[0, 0, rows, :], v16)
            ge = g_r[0, ch * SUBLANES:ch * SUBLANES + 1, :]
            s_ref[d] = st * ge + _dot_tn(v16, k_r[0, rows, :])


def gla_scan(qd, ke, at, ge, v16, n_batch, blocks_per_seq):
    hk, hv = GLA_HEAD_K, GLA_HEAD_V
    t = v16.shape[0]
    n_lat_blocks = n_batch * blocks_per_seq

    def fwd(b, s):
        return jnp.where(s == 0, n_lat_blocks + b, b * blocks_per_seq + s - 1)

    def bwd(b, s):
        return jnp.where(s == 0, n_lat_blocks + b, b * blocks_per_seq + blocks_per_seq - s)

    def out_blk(blk):
        return lambda b, h, s: (blk(b, s), h)

    def key_spec(d, blk, rows):
        return pl.BlockSpec((1, rows, hk), lambda b, h, s: (d, blk(b, s), h))

    def attn_spec(d, blk):
        return pl.BlockSpec((1, 1, ROW_BLOCK, GLA_STEP), lambda b, h, s: (d, h, blk(b, s), 0))

    def val_spec(blk):
        return pl.BlockSpec((ROW_BLOCK, hv), lambda b, h, s: (blk(b, s), h))

    in_specs = []
    for d, blk in ((0, fwd), (1, bwd)):
        in_specs += [key_spec(d, blk, ROW_BLOCK), key_spec(d, blk, ROW_BLOCK), attn_spec(d, blk),
                     key_spec(d, blk, ROW_BLOCK // SUBLANES), val_spec(blk)]
    o_f, o_b = pl.pallas_call(
        _gla_scan_body,
        out_shape=(jax.ShapeDtypeStruct((t, GLA_V_DIM), jnp.float32),) * 2,
        grid=(n_batch, GLA_HEADS, blocks_per_seq + 1),
        in_specs=in_specs,
        out_specs=(pl.BlockSpec((ROW_BLOCK, hv), out_blk(fwd)), pl.BlockSpec((ROW_BLOCK, hv), out_blk(bwd))),
        scratch_shapes=[pltpu.VMEM((2, hv, hk), jnp.float32)],
        compiler_params=_cparams("parallel", "parallel", "arbitrary"),
        name="gla_scan",
    )(qd, ke, at, ge, v16, qd, ke, at, ge, v16)
    return o_f, o_b


def _fourier_in_body(x_ref, w_ref, o_ref):
    y = _dot(_bf(x_ref[...]), w_ref[0])
    half = y.shape[1] // 2
    o_ref[0] = _bf(y[:, :half])
    o_ref[1] = _bf(y[:, half:])


def dft_tables(n, nc):
    r = 64
    assert n % r == 0
    kk = jnp.arange(n, dtype=jnp.int32)[None, :]
    j1 = jnp.arange(n // r, dtype=jnp.int32)[:, None]
    j0 = jnp.arange(r, dtype=jnp.int32)[:, None]
    a = ((j1 * kk) % (n // r)).astype(jnp.float32) * (2.0 * jnp.pi / (n // r))
    b = ((j0 * kk) % n).astype(jnp.float32) * (2.0 * jnp.pi / n)
    ca, sa, cb, sb = jnp.cos(a)[:, None, :], jnp.sin(a)[:, None, :], jnp.cos(b)[None, :, :], jnp.sin(b)[None, :, :]
    scale = n ** -0.5
    cos_n = ((ca * cb - sa * sb) * scale).reshape(n, n)
    sin_n = ((sa * cb + ca * sb) * scale).reshape(n, n)
    table = jnp.concatenate([cos_n, -sin_n], axis=1).astype(jnp.bfloat16)
    cc = jnp.arange(nc, dtype=jnp.int32)
    ang = ((cc[:, None] * cc[None, :]) % nc).astype(jnp.float32) * (2.0 * jnp.pi / nc)
    return table, jnp.cos(ang) * nc ** -0.5, jnp.sin(ang) * nc ** -0.5


def fourier_mix(proj, fourier_w, n_batch, n_lat, out_cols=FOURIER_DIM):
    gd = FOURIER_GROUP_DIM
    ng = FOURIER_GROUPS
    table, cos_c, sin_c = dft_tables(n_lat, gd)
    w_all = fourier_w.astype(jnp.float32).transpose(1, 0, 2).reshape(gd, ng * gd)
    folded = dense_matmul([(jnp.concatenate([cos_c, sin_c], axis=0), w_all)], tm=2 * gd, tn=512)
    w2 = jnp.concatenate([folded[:gd].reshape(gd, ng, gd), folded[gd:].reshape(gd, ng, gd)], axis=2)
    w2 = w2.transpose(1, 0, 2).astype(jnp.bfloat16)
    tm = min(DENSE_TM, n_lat)
    tiles = n_lat // tm
    z = pl.pallas_call(
        _fourier_in_body,
        out_shape=jax.ShapeDtypeStruct((2, n_lat, n_batch * ng * gd), jnp.bfloat16),
        grid=(n_batch, ng, tiles),
        in_specs=[pl.BlockSpec((tm, gd), lambda b, g, i: (b * tiles + i, g)),
                  pl.BlockSpec((1, gd, 2 * gd), lambda b, g, i: (g, 0, 0))],
        out_specs=pl.BlockSpec((2, tm, gd), lambda b, g, i: (0, i, b * ng + g)),
        compiler_params=_cparams("parallel", "parallel", "parallel"),
        name="fourier_in",
    )(proj, w2)
    z = z.reshape(2 * n_lat, n_batch * ng * gd)
    tn = 512
    per_b = ng * gd // tn
    return dense_matmul([(table, z)], out_dtype=jnp.bfloat16, tm=512, tn=tn,
                        out_shape=(n_batch * n_lat, out_cols),
                        out_index=lambda i, j: ((j // per_b) * (n_lat // 512) + i, j % per_b))


def modulation(c, c_ctx, mod_w, mod_b, layer):
    b, d = c.shape
    rows = jnp.concatenate([c, c_ctx[None, :], jnp.zeros((SUBLANES - b - 1, d), c.dtype)], axis=0)
    out = dense_matmul([(jax.nn.silu(rows), mod_w.reshape(-1, mod_w.shape[-1]))], tm=SUBLANES, tn=512,
                       w_row_block=layer) + mod_b[layer]
    return tuple(m[:, None, :] for m in jnp.split(out, N_MOD, axis=-1))


def _combine_body(x_ref, y_ref, w_ref, gate_ref, nw_ref, o_ref, *, final_norm):
    d = x_ref.shape[1]
    y = w_ref[:, 0:1] * y_ref[:, :d] + w_ref[:, 1:2] * y_ref[:, d:]
    x = x_ref[...] + gate_ref[0] * y
    if final_norm:
        x = x * lax.rsqrt(jnp.mean(x * x, axis=-1, keepdims=True) + NORM_EPS) * nw_ref[...]
    o_ref[...] = x


def moe_combine(x, y2, weight, gates, gate_index, final_norm_w=None):
    m, d = y2.shape[0], x.shape[1]
    tm = ROW_BLOCK
    nw = jnp.ones((1, d), jnp.float32) if final_norm_w is None else final_norm_w.reshape(1, d).astype(jnp.float32)
    return pl.pallas_call(
        functools.partial(_combine_body, final_norm=final_norm_w is not None),
        out_shape=jax.ShapeDtypeStruct((m, d), jnp.float32),
        grid=(m // tm,),
        in_specs=[pl.BlockSpec((tm, d), lambda i: (i, 0)),
                  pl.BlockSpec((tm, MOE_TOP_K * d), lambda i: (i, 0)),
                  pl.BlockSpec((tm, MOE_TOP_K), lambda i: (i, 0)),
                  pl.BlockSpec((1, 1, d), lambda i: (gate_index(i), 0, 0)),
                  pl.BlockSpec((1, d), lambda i: (0, 0))],
        out_specs=pl.BlockSpec((tm, d), lambda i: (i, 0)),
        compiler_params=_cparams("parallel"),
        name="moe_combine",
    )(x, y2, weight, gates, nw)


def kernel(x, c, ctx, c_ctx, mod_w, mod_b, norm1_w, norm2_w, ab_w_in, pool_w, pool_scale, gdn_conv_w,
           gdn_a_log, gdn_dt_bias, gdn_norm_w, ab_w_out, cd_w_in, fourier_w, gla_gate_up, gla_gate_b,
           gla_norm_w, cd_w_out, moe_group_w, moe_group_b, moe_expert_w, moe_expert_b, moe_w1, moe_w3,
           moe_w2, final_norm_w):
    assert DEPTH == 2
    n_batch, n_lat, d = x.shape
    n_ctx = ctx.shape[1]
    assert n_ctx == ROW_BLOCK and n_lat % DENSE_TM == 0 and (n_batch * n_ctx) % DENSE_TM == 0
    t_lat = n_batch * n_lat
    blocks_per_seq = n_lat // ROW_BLOCK
    n_lat_blocks = n_batch * blocks_per_seq
    xs = jnp.concatenate([x.reshape(t_lat, d), ctx.reshape(n_batch * n_ctx, d)], axis=0)

    def mod_row(rows_per_tile):
        tiles_per_seq = n_lat // rows_per_tile
        return lambda i: jnp.minimum(i // tiles_per_seq, n_batch)

    bf = jnp.bfloat16
    w1_all, w3_all, w2_all = (w.reshape((-1,) + w.shape[2:]) for w in (moe_w1, moe_w3, moe_w2))
    sh1, sc1, g1, sh2, sc2, g2 = modulation(c, c_ctx, mod_w, mod_b, 0)
    h = norm_modulate(xs, norm1_w[0], sh1, sc1, mod_row(512))
    w_in = ab_w_in[0].astype(bf)
    proj = dense_matmul([(h, w_in)], n=AB_MAIN)
    tail = dense_matmul([(h, _pad_cols(w_in[:, AB_MAIN:]))])
    gates, gates_t = gdn_gates(tail, gdn_a_log[0], gdn_dt_bias[0], GDN_HEADS)
    u, w, qd, ke, at, ge = gdn_prepare(proj, gates, gates_t, gdn_conv_w[0], GDN_HEADS, POOL_DIM, blocks_per_seq,
                                       n_lat_blocks)
    o_f, o_b = gdn_scan(u, w, qd, ke, at, ge, GDN_HEADS, n_batch, blocks_per_seq)
    y = pool_mix(proj, pool_w[0], pool_scale[0], n_lat_blocks, out_cols=d)
    y = gated_head_norm(o_f, o_b, proj, POOL_DIM + 3 * GDN_DIM, gdn_norm_w[0], GDN_HEAD_DIM, y, POOL_DIM)
    xs = dense_matmul([(y, ab_w_out[0].astype(bf))], residual=xs, gates=g1, gate_index=mod_row(DENSE_TM))
    f = norm_modulate(xs, norm2_w[0], sh2, sc2, mod_row(512))
    y2, wgt = hier_moe(f, moe_group_w[0], moe_group_b[0], moe_expert_w[0], moe_expert_b[0], w1_all, w3_all, w2_all, 0)
    xs = moe_combine(xs, y2, wgt, g2, mod_row(ROW_BLOCK))

    sh1, sc1, g1, sh2, sc2, g2 = modulation(c, c_ctx, mod_w, mod_b, 1)
    h = norm_modulate(xs, norm1_w[1], sh1, sc1, mod_row(512))
    w_in = cd_w_in[0].astype(bf)
    proj = dense_matmul([(h, w_in)], n=CD_MAIN)
    lr_tail = dense_matmul([(h, _pad_cols(w_in[:, CD_MAIN:]))])
    qd, ke, at, ge, v16 = gla_prepare(proj, lr_tail, gla_gate_up[0], gla_gate_b[0])
    o_f, o_b = gla_scan(qd, ke, at, ge, v16, n_batch, blocks_per_seq)
    y = fourier_mix(proj, fourier_w[0], n_batch, n_lat, out_cols=d)
    y = gated_head_norm(o_f, o_b, proj, CD_SPLITS[3], gla_norm_w[0], GLA_HEAD_V, y, FOURIER_DIM)
    xl = dense_matmul([(y, cd_w_out[0].astype(bf))], residual=xs, gates=g1, gate_index=mod_row(DENSE_TM))
    f = norm_modulate(xl, norm2_w[1], sh2, sc2, mod_row(512))
    y2, wgt = hier_moe(f, moe_group_w[1], moe_group_b[1], moe_expert_w[1], moe_expert_b[1], w1_all, w3_all, w2_all,
                       MOE_EXPERTS)
    out = moe_combine(xl, y2, wgt, g2, mod_row(ROW_BLOCK), final_norm_w=final_norm_w)
    return out.reshape(n_batch, n_lat, d)
```

```python
import functools

import jax
import jax.numpy as jnp
from jax import lax
from jax.experimental import pallas as pl
from jax.experimental.pallas import tpu as pltpu

D_MODEL = 4096
DEPTH = 2
CTX_LEN = 256
GRID_W = 64
N_MOD = 6
NORM_EPS = 1e-6

POOL_WINDOWS = (2, 4, 8, 16)
POOL_GROUPS = 4
POOL_GROUP_DIM = D_MODEL // 16
POOL_DIM = POOL_GROUPS * POOL_GROUP_DIM

GDN_HEAD_DIM = 128
GDN_DIM = D_MODEL - POOL_DIM
GDN_HEADS = GDN_DIM // GDN_HEAD_DIM
GDN_CONV = 5
GDN_CHUNK = 64

FOURIER_GROUPS = 4
FOURIER_GROUP_DIM = D_MODEL // 16
FOURIER_DIM = FOURIER_GROUPS * FOURIER_GROUP_DIM

GLA_HEADS = 6
GLA_V_DIM = D_MODEL - FOURIER_DIM
GLA_K_DIM = GLA_V_DIM // 2
GLA_HEAD_K = GLA_K_DIM // GLA_HEADS
GLA_HEAD_V = GLA_V_DIM // GLA_HEADS
GLA_GATE_RANK = 16
GLA_GATE_TEMP = 16.0
GLA_CHUNK = 32

MOE_GROUPS = 4
MOE_EXPERTS_PER_GROUP = 8
MOE_EXPERTS = MOE_GROUPS * MOE_EXPERTS_PER_GROUP
MOE_TOP_K = 2
MOE_HIDDEN = D_MODEL // 8

AB_MAIN = POOL_DIM + 4 * GDN_DIM
CD_SPLITS = (FOURIER_DIM, FOURIER_DIM + GLA_K_DIM, FOURIER_DIM + 2 * GLA_K_DIM,
             FOURIER_DIM + 2 * GLA_K_DIM + GLA_V_DIM, FOURIER_DIM + 2 * GLA_K_DIM + 2 * GLA_V_DIM)
CD_MAIN = FOURIER_DIM + 2 * GLA_K_DIM + 2 * GLA_V_DIM

LANES = 128
SUBLANES = 8
VMEM_LIMIT = 48 * 1024 * 1024
MOE_UP_VMEM_LIMIT = 56 * 1024 * 1024
MOE_BLOCK_ROWS = 256
ROW_BLOCK = 256
DENSE_TM = 1024
DENSE_TN = 512
GDN_PREP_HEADS = 4
GDN_SCAN_HEADS = 8


def _cparams(*sem):
    return pltpu.CompilerParams(dimension_semantics=sem, vmem_limit_bytes=VMEM_LIMIT)


def _dot(a, b):
    return jnp.dot(a, b, preferred_element_type=jnp.float32)


def _dot_nt(a, b):
    return lax.dot_general(a, b, (((1,), (1,)), ((), ())), preferred_element_type=jnp.float32)


def _dot_tn(a, b):
    return lax.dot_general(a, b, (((0,), (0,)), ((), ())), preferred_element_type=jnp.float32)


def _bf(x):
    return x.astype(jnp.bfloat16)


def _split3(x):
    hi = _bf(x)
    r1 = x - hi.astype(jnp.float32)
    mid = _bf(r1)
    lo = _bf(r1 - mid.astype(jnp.float32))
    return hi, mid, lo


def _mm_body(*refs, n_pairs, has_res):
    o_ref = refs[-1]
    acc = _dot(refs[0][...], refs[1][...])
    for p in range(1, n_pairs):
        acc = acc + _dot(refs[2 * p][...], refs[2 * p + 1][...])
    if has_res:
        res_ref, gate_ref = refs[2 * n_pairs], refs[2 * n_pairs + 1]
        acc = res_ref[...] + gate_ref[0] * acc
    o_ref[...] = acc.astype(o_ref.dtype)


def dense_matmul(pairs, out_dtype=jnp.float32, tm=DENSE_TM, tn=512, m=None, n=None, w_row_block=0, residual=None,
                 gates=None, gate_index=None, out_shape=None, out_index=None):
    m = pairs[0][0].shape[0] if m is None else m
    n = pairs[0][1].shape[1] if n is None else n
    tm = min(tm, m)
    tn = min(tn, n)
    assert m % tm == 0 and n % tn == 0, (m, n, tm, tn)
    in_specs, args = [], []
    for x, w in pairs:
        k = x.shape[1]
        assert w.shape[0] % k == 0
        in_specs += [pl.BlockSpec((tm, k), lambda i, j: (i, 0)),
                     pl.BlockSpec((k, tn), lambda i, j: (w_row_block, j))]
        args += [x, w]
    if residual is not None:
        in_specs += [pl.BlockSpec((tm, tn), lambda i, j: (i, j)),
                     pl.BlockSpec((1, 1, tn), lambda i, j: (gate_index(i), 0, j))]
        args += [residual, gates]
    return pl.pallas_call(
        functools.partial(_mm_body, n_pairs=len(pairs), has_res=residual is not None),
        out_shape=jax.ShapeDtypeStruct((m, n) if out_shape is None else out_shape, out_dtype),
        grid=(m // tm, n // tn),
        in_specs=in_specs,
        out_specs=pl.BlockSpec((tm, tn), (lambda i, j: (i, j)) if out_index is None else out_index),
        compiler_params=_cparams("parallel", "parallel"),
        name="dense_matmul",
    )(*args)


def _pad_cols(w, mult=LANES):
    pad = (-w.shape[1]) % mult
    return jnp.pad(w, ((0, 0), (0, pad))) if pad else w


def _norm_mod_body(x_ref, nw_ref, shift_ref, scale_ref, o_ref):
    x = x_ref[...]
    y = x * lax.rsqrt(jnp.mean(x * x, axis=-1, keepdims=True) + NORM_EPS) * nw_ref[...]
    o_ref[...] = (y * (1.0 + scale_ref[0]) + shift_ref[0]).astype(o_ref.dtype)


def norm_modulate(x, norm_w, shift, scale, row_index, m=None, tm=512):
    m = x.shape[0] if m is None else m
    d = x.shape[1]
    vec = pl.BlockSpec((1, 1, d), lambda i: (row_index(i), 0, 0))
    return pl.pallas_call(
        _norm_mod_body,
        out_shape=jax.ShapeDtypeStruct((m, d), jnp.bfloat16),
        grid=(m // tm,),
        in_specs=[pl.BlockSpec((tm, d), lambda i: (i, 0)), pl.BlockSpec((1, d), lambda i: (0, 0)), vec, vec],
        out_specs=pl.BlockSpec((tm, d), lambda i: (i, 0)),
        compiler_params=_cparams("parallel"),
        name="norm_modulate",
    )(x, norm_w.reshape(1, d), shift, scale)


def _gdn_gates_body(tail_ref, par_ref, o_ref, ot_ref, *, n_heads, chunk):
    t = tail_ref[...]
    rows = t.shape[0]
    lane = lax.broadcasted_iota(jnp.int32, t.shape, 1)
    a_row = par_ref[0:1, :]
    dtb_row = par_ref[1:2, :]
    beta = jax.nn.sigmoid(t)
    z = t + dtb_row
    g = a_row * (jnp.maximum(z, 0.0) + jnp.log1p(jnp.exp(-jnp.abs(z))))
    ri = lax.broadcasted_iota(jnp.int32, (rows, rows), 0)
    ci = lax.broadcasted_iota(jnp.int32, (rows, rows), 1)
    shift = chunk.bit_length() - 1
    same = (ri >> shift) == (ci >> shift)
    pre = _bf(jnp.where(same & (ci <= ri), 1.0, 0.0))
    suf = _bf(jnp.where(same & (ci >= ri), 1.0, 0.0))
    ones = _bf(jnp.where(same, 1.0, 0.0))
    parts = _split3(g)
    cf = _dot(pre, parts[0]) + _dot(pre, parts[1]) + _dot(pre, parts[2])
    cb = _dot(suf, parts[0]) + _dot(suf, parts[1]) + _dot(suf, parts[2])
    tot = _dot(ones, parts[0]) + _dot(ones, parts[1]) + _dot(ones, parts[2])
    main = jnp.where(lane < 2 * n_heads, beta,
                     jnp.where(lane < 3 * n_heads, cf, jnp.where(lane < 4 * n_heads, cb, 0.0)))
    o_ref[:, :LANES] = main
    o_ref[:, LANES:] = tot
    ot_ref[...] = main.T


def gdn_gates(proj, tail_col0, a_log, dt_bias, n_heads):
    t = proj.shape[0]
    assert tail_col0 % LANES == 0
    par = jnp.zeros((SUBLANES, LANES), jnp.float32)
    par = par.at[0, 2 * n_heads:4 * n_heads].set(-jnp.exp(a_log.astype(jnp.float32)).reshape(-1))
    par = par.at[1, 2 * n_heads:4 * n_heads].set(dt_bias.astype(jnp.float32).reshape(-1))
    return pl.pallas_call(
        functools.partial(_gdn_gates_body, n_heads=n_heads, chunk=GDN_CHUNK),
        out_shape=(jax.ShapeDtypeStruct((t, 2 * LANES), jnp.float32),
                   jax.ShapeDtypeStruct((t // ROW_BLOCK * LANES, ROW_BLOCK), jnp.float32)),
        grid=(t // ROW_BLOCK,),
        in_specs=[pl.BlockSpec((ROW_BLOCK, LANES), lambda i: (i, tail_col0 // LANES)),
                  pl.BlockSpec((SUBLANES, LANES), lambda i: (0, 0))],
        out_specs=(pl.BlockSpec((ROW_BLOCK, 2 * LANES), lambda i: (i, 0)),
                   pl.BlockSpec((LANES, ROW_BLOCK), lambda i: (i, 0))),
        compiler_params=_cparams("parallel"),
        name="gdn_gates",
    )(proj, par)


def _gdn_prep_body(q_ref, k_ref, v_ref, qp_ref, kp_ref, vp_ref, qn_ref, kn_ref, vn_ref,
                   wq_ref, wk_ref, wv_ref, gt_ref, gtt_ref,
                   u_ref, w_ref, qd_ref, ke_ref, at_ref, ge_ref, *, heads, n_heads, blocks_per_seq, n_lat_blocks):
    rb = pl.program_id(0)
    hg = pl.program_id(1)
    n = ROW_BLOCK
    c = GDN_CHUNK
    n_ch = n // c
    cs = c.bit_length() - 1
    hd = GDN_HEAD_DIM
    is_ctx = rb >= n_lat_blocks
    first = jnp.logical_or(is_ctx, rb % blocks_per_seq == 0)
    last = jnp.logical_or(is_ctx, rb % blocks_per_seq == blocks_per_seq - 1)
    m_prev = jnp.where(first, 0.0, 1.0)
    m_next = jnp.where(last, 0.0, 1.0)

    def conv_silu(cur_ref, prev_ref, next_ref, cw_ref):
        ext = jnp.concatenate([prev_ref[...] * m_prev, cur_ref[...], next_ref[...] * m_next], axis=0)
        acc = None
        for j in range(GDN_CONV):
            lo = SUBLANES - GDN_CONV // 2 + j
            term = ext[lo:lo + ROW_BLOCK, :] * cw_ref[j:j + 1, :]
            acc = term if acc is None else acc + term
        return acc * jax.nn.sigmoid(acc)

    qc = conv_silu(q_ref, qp_ref, qn_ref, wq_ref)
    kc = conv_silu(k_ref, kp_ref, kn_ref, wk_ref)
    vc = conv_silu(v_ref, vp_ref, vn_ref, wv_ref)

    gt_lo = gt_ref[:, :LANES]
    gt_hi = gt_ref[:, LANES:]
    lane = lax.broadcasted_iota(jnp.int32, (n, LANES), 1)
    blk16 = _bf(jnp.where((lax.broadcasted_iota(jnp.int32, (n, n), 0) >> cs)
                          == (lax.broadcasted_iota(jnp.int32, (n, n), 1) >> cs), 1.0, 0.0))
    wr = lax.broadcasted_iota(jnp.int32, (c, n), 0)
    wc = lax.broadcasted_iota(jnp.int32, (c, n), 1)
    eye_wide = jnp.where(wr == (wc & (c - 1)), 1.0, 0.0)
    sr = lax.broadcasted_iota(jnp.int32, (c, LANES), 0)
    sc = lax.broadcasted_iota(jnp.int32, (c, LANES), 1)
    slab_masks = []
    for odd in range(LANES // c):
        rel = sc - odd * c
        own = (sc >> cs) == odd
        slab_masks.append(((own & (rel <= sr), own & (rel < sr)), (own & (rel >= sr), own & (rel > sr))))

    def tall(wide):
        return jnp.concatenate([_bf(wide)] * n_ch, axis=0) * blk16

    def column(src, idx):
        return jnp.sum(jnp.where(lane == idx, src, 0.0), axis=1, keepdims=True)

    per_tile = LANES // c
    chains = []
    for gi in range(heads):
        h = hg * heads + gi
        sl = slice(gi * hd, (gi + 1) * hd)
        qh, kh, v = qc[:, sl], kc[:, sl], vc[:, sl]
        q = qh * lax.rsqrt(jnp.sum(qh * qh, axis=-1, keepdims=True) + NORM_EPS) * (hd ** -0.5)
        k = kh * lax.rsqrt(jnp.sum(kh * kh, axis=-1, keepdims=True) + NORM_EPS)
        k16 = _bf(k)
        qk = _dot_nt(_bf(q), k16)
        kk = _dot_nt(k16, k16)
        for d in range(2):
            beta = column(gt_lo, d * n_heads + h)
            gc = column(gt_lo, (2 + d) * n_heads + h)
            tot = column(gt_hi, (2 + d) * n_heads + h)
            gc_row = gtt_ref[pl.ds((2 + d) * n_heads + h, 1), :]
            p_tiles = []
            for j in range(n_ch):
                rows = slice(j * c, (j + 1) * c)
                cols = slice((j // per_tile) * LANES, (j // per_tile + 1) * LANES)
                incl, strict = slab_masks[j % per_tile][d]
                decay = jnp.where(incl, jnp.exp(jnp.where(incl, gc[rows] - gc_row[:, cols], 0.0)), 0.0)
                a = jnp.where(strict, (beta[rows] * kk[rows, cols]) * decay, 0.0)
                if j % per_tile == 0:
                    p_tiles.append(-a)
                else:
                    p_tiles[-1] = p_tiles[-1] - a
                lo = (j % per_tile) * c
                at_ref[d, rows, gi * c:(gi + 1) * c] = _bf((qk[rows, cols] * decay)[:, lo:lo + c])
                ge_ref[d, j * SUBLANES:(j + 1) * SUBLANES, sl] = jnp.broadcast_to(
                    jnp.exp(tot[j * c:j * c + SUBLANES]), (SUBLANES, hd))
            p = jnp.concatenate(p_tiles, axis=1)
            eg = jnp.exp(gc)
            kbeta = k * beta
            rhs = _bf(jnp.concatenate([v * beta, kbeta * eg], axis=1))
            qd_ref[d, :, sl] = _bf(q * eg)
            ke_ref[d, :, sl] = _bf(k * jnp.exp(tot - gc))
            chains.append([d, sl, p, eye_wide + p, tall(p), rhs])

    span = 2
    while span < c:
        for ch in chains:
            ch[2] = _dot(_bf(ch[2]), ch[4])
            ch[4] = tall(ch[2])
            ch[3] = ch[3] + _dot(_bf(ch[3]), ch[4])
        span *= 2
    for d, sl, _, t, _, rhs in chains:
        sol = _dot(tall(t), rhs)
        u_ref[d, :, sl] = sol[:, :hd]
        w_ref[d, :, sl] = _bf(sol[:, hd:])


def gdn_prepare(proj, gates, gates_t, conv_w, n_heads, qkv_col0, blocks_per_seq, n_lat_blocks):
    t = proj.shape[0]
    hd = GDN_HEAD_DIM
    g = GDN_PREP_HEADS
    gw = g * hd
    dim = n_heads * hd
    nb = t // ROW_BLOCK
    cb0 = qkv_col0 // gw
    per = dim // gw
    halo_per_block = ROW_BLOCK // SUBLANES

    def cur(off):
        return pl.BlockSpec((ROW_BLOCK, gw), lambda rb, hg: (rb, cb0 + off * per + hg))

    def prev(off):
        return pl.BlockSpec((SUBLANES, gw),
                            lambda rb, hg: (jnp.maximum(rb * halo_per_block - 1, 0), cb0 + off * per + hg))

    def nxt(off):
        return pl.BlockSpec((SUBLANES, gw),
                            lambda rb, hg: (jnp.minimum((rb + 1) * halo_per_block, nb * halo_per_block - 1),
                                            cb0 + off * per + hg))

    def cw(off):
        return pl.BlockSpec((GDN_CONV, gw), lambda rb, hg: (0, off * per + hg))

    out_tok = lambda width: pl.BlockSpec((2, ROW_BLOCK, width), lambda rb, hg: (0, rb, hg))
    outs = pl.pallas_call(
        functools.partial(_gdn_prep_body, heads=g, n_heads=n_heads, blocks_per_seq=blocks_per_seq,
                          n_lat_blocks=n_lat_blocks),
        out_shape=(jax.ShapeDtypeStruct((2, t, dim), jnp.float32),
                   jax.ShapeDtypeStruct((2, t, dim), jnp.bfloat16),
                   jax.ShapeDtypeStruct((2, t, dim), jnp.bfloat16),
                   jax.ShapeDtypeStruct((2, t, dim), jnp.bfloat16),
                   jax.ShapeDtypeStruct((2, t, n_heads * GDN_CHUNK), jnp.bfloat16),
                   jax.ShapeDtypeStruct((2, t // SUBLANES, dim), jnp.float32)),
        grid=(nb, n_heads // g),
        in_specs=[cur(0), cur(1), cur(2), prev(0), prev(1), prev(2), nxt(0), nxt(1), nxt(2),
                  cw(0), cw(1), cw(2), pl.BlockSpec((ROW_BLOCK, 2 * LANES), lambda rb, hg: (rb, 0)),
                  pl.BlockSpec((LANES, ROW_BLOCK), lambda rb, hg: (rb, 0))],
        out_specs=(out_tok(gw), out_tok(gw), out_tok(gw), out_tok(gw), out_tok(g * GDN_CHUNK),
                   pl.BlockSpec((2, ROW_BLOCK // SUBLANES, gw), lambda rb, hg: (0, rb, hg))),
        compiler_params=_cparams("parallel", "parallel"),
        name="gdn_prepare",
    )(proj, proj, proj, proj, proj, proj, proj, proj, proj, conv_w, conv_w, conv_w, gates, gates_t)
    return outs


def _gdn_scan_body(uf, wf, qf, kf, af, gf, ub, wb, qb, kb, ab, gb, of_ref, ob_ref, s_ref, *, heads):
    t = pl.program_id(2)
    c = GDN_CHUNK
    hd = GDN_HEAD_DIM
    n_ch = ROW_BLOCK // c

    @pl.when(t == 0)
    def _():
        s_ref[...] = jnp.zeros_like(s_ref)

    pw = 2 * hd
    s_mask = ((lax.broadcasted_iota(jnp.int32, (pw, pw), 0) >> (hd.bit_length() - 1))
              == (lax.broadcasted_iota(jnp.int32, (pw, pw), 1) >> (hd.bit_length() - 1)))
    v_mask = ((lax.broadcasted_iota(jnp.int32, (2 * c, pw), 0) >> (c.bit_length() - 1))
              == (lax.broadcasted_iota(jnp.int32, (2 * c, pw), 1) >> (hd.bit_length() - 1)))
    views = ((uf, wf, qf, kf, af, gf, of_ref), (ub, wb, qb, kb, ab, gb, ob_ref))
    chains = [(d, p) for d in range(2) for p in range(heads // 2)]
    for step in range(n_ch):
        def locate(d, p):
            ch = step if d == 0 else n_ch - 1 - step
            return views[d], ch, slice(ch * c, (ch + 1) * c), slice(p * pw, (p + 1) * pw)

        state, r_all, v_all = {}, {}, {}
        for d, p in chains:
            (u_r, w_r, q_r, k_r, a_r, g_r, o_r), ch, rows, sl = locate(d, p)
            s = s_ref[d, p]
            state[d, p] = s
            s_diag = _bf(jnp.where(s_mask, jnp.concatenate([s, s], axis=0), 0.0))
            r_all[d, p] = _dot(jnp.concatenate([w_r[0, rows, sl], q_r[0, rows, sl]], axis=0), s_diag)
        for d, p in chains:
            (u_r, w_r, q_r, k_r, a_r, g_r, o_r), ch, rows, sl = locate(d, p)
            v_new = u_r[0, rows, sl] - r_all[d, p][:c]
            v_all[d, p] = _bf(jnp.where(v_mask, jnp.concatenate([v_new, v_new], axis=0), 0.0))
        for d, p in chains:
            (u_r, w_r, q_r, k_r, a_r, g_r, o_r), ch, rows, sl = locate(d, p)
            o_r[0, rows, sl] = r_all[d, p][c:] + _dot(a_r[0, rows, p * 2 * c:(p + 1) * 2 * c], v_all[d, p])
        for d, p in chains:
            (u_r, w_r, q_r, k_r, a_r, g_r, o_r), ch, rows, sl = locate(d, p)
            k_stack = jnp.concatenate([k_r[0, rows, p * pw:p * pw + hd], k_r[0, rows, p * pw + hd:(p + 1) * pw]],
                                      axis=0)
            ge = g_r[0, ch * SUBLANES:ch * SUBLANES + 1, sl]
            s_ref[d, p] = state[d, p] * ge + _dot_tn(k_stack, v_all[d, p])


def gdn_scan(u, w, qd, ke, at, ge, n_heads, n_batch, blocks_per_seq):
    t = u.shape[1]
    hd = GDN_HEAD_DIM
    g = GDN_SCAN_HEADS
    gw = g * hd
    n_lat_blocks = n_batch * blocks_per_seq

    def fwd(b, s):
        return jnp.where(s == 0, n_lat_blocks + b, b * blocks_per_seq + s - 1)

    def bwd(b, s):
        return jnp.where(s == 0, n_lat_blocks + b, b * blocks_per_seq + blocks_per_seq - s)

    def tok(d, width, blk):
        return pl.BlockSpec((1, ROW_BLOCK, width), lambda b, hg, s: (d, blk(b, s), hg))

    def gsp(d, blk):
        return pl.BlockSpec((1, ROW_BLOCK // SUBLANES, gw), lambda b, hg, s: (d, blk(b, s), hg))

    in_specs = []
    for d, blk in ((0, fwd), (1, bwd)):
        in_specs += [tok(d, gw, blk), tok(d, gw, blk), tok(d, gw, blk), tok(d, gw, blk),
                     tok(d, g * GDN_CHUNK, blk), gsp(d, blk)]
    o_f, o_b = pl.pallas_call(
        functools.partial(_gdn_scan_body, heads=g),
        out_shape=(jax.ShapeDtypeStruct((1, t, n_heads * hd), jnp.float32),) * 2,
        grid=(n_batch, n_heads // g, blocks_per_seq + 1),
        in_specs=in_specs,
        out_specs=(pl.BlockSpec((1, ROW_BLOCK, gw), lambda b, hg, s: (0, fwd(b, s), hg)),
                   pl.BlockSpec((1, ROW_BLOCK, gw), lambda b, hg, s: (0, bwd(b, s), hg))),
        scratch_shapes=[pltpu.VMEM((2, g // 2, hd, 2 * hd), jnp.float32)],
        compiler_params=_cparams("parallel", "parallel", "arbitrary"),
        name="gdn_scan",
    )(u, w, qd, ke, at, ge, u, w, qd, ke, at, ge)
    return o_f[0], o_b[0]


GATED_NORM_COLS = 1024


def _gated_norm_body(of_ref, ob_ref, z_ref, nw_ref, y_in_ref, y_ref, *, hd):
    del y_in_ref
    for h in range(GATED_NORM_COLS // hd):
        sl = slice(h * hd, (h + 1) * hd)
        o = of_ref[:, sl] + ob_ref[:, sl]
        y = o * lax.rsqrt(jnp.mean(o * o, axis=-1, keepdims=True) + NORM_EPS) * nw_ref[...]
        z = z_ref[:, sl]
        y_ref[:, sl] = (y * (z * jax.nn.sigmoid(z))).astype(y_ref.dtype)


def gated_head_norm(o_f, o_b, proj, z_col0, norm_w, hd, y_init, y_col0):
    m = y_init.shape[0]
    dim = o_f.shape[1]
    cw = GATED_NORM_COLS
    assert z_col0 % cw == 0 and dim % cw == 0 and cw % hd == 0 and y_col0 % cw == 0
    return pl.pallas_call(
        functools.partial(_gated_norm_body, hd=hd),
        out_shape=jax.ShapeDtypeStruct(y_init.shape, y_init.dtype),
        grid=(m // ROW_BLOCK, dim // cw),
        in_specs=[pl.BlockSpec((ROW_BLOCK, cw), lambda i, j: (i, j)),
                  pl.BlockSpec((ROW_BLOCK, cw), lambda i, j: (i, j)),
                  pl.BlockSpec((ROW_BLOCK, cw), lambda i, j: (i, z_col0 // cw + j)),
                  pl.BlockSpec((1, hd), lambda i, j: (0, 0)),
                  pl.BlockSpec(memory_space=pl.ANY)],
        out_specs=pl.BlockSpec((ROW_BLOCK, cw), lambda i, j: (i, y_col0 // cw + j)),
        input_output_aliases={4: 0},
        compiler_params=_cparams("parallel", "parallel"),
        name="gated_head_norm",
    )(o_f, o_b, proj, norm_w.reshape(1, hd), y_init)


def _pool_body(u_ref, w_ref, sc_ref, y_ref, *, n_lat_blocks):
    rb = pl.program_id(0)
    is_ctx = rb >= n_lat_blocks
    row_len = jnp.where(is_ctx, CTX_LEN, GRID_W)
    shift = jnp.where(is_ctx, CTX_LEN.bit_length() - 1, GRID_W.bit_length() - 1)
    n = ROW_BLOCK
    ri = lax.broadcasted_iota(jnp.int32, (n, n), 0)
    ci = lax.broadcasted_iota(jnp.int32, (n, n), 1)
    same = (ri >> shift) == (ci >> shift)
    pos_r = ri & (row_len - 1)
    pos_c = ci & (row_len - 1)
    rcol = lax.broadcasted_iota(jnp.int32, (n, 1), 0) & (row_len - 1)
    gd = POOL_GROUP_DIM
    for g, win in enumerate(POOL_WINDOWS):
        half = win // 2
        lo = jnp.maximum(pos_r - half, 0)
        hi = jnp.minimum(pos_r + half - 1, row_len - 1)
        band = _bf(jnp.where(same & (pos_c >= lo) & (pos_c <= hi), 1.0, 0.0))
        cnt = (jnp.minimum(rcol + half - 1, row_len - 1) - jnp.maximum(rcol - half, 0) + 1).astype(jnp.float32)
        u = u_ref[:, g * gd:(g + 1) * gd]
        parts = _split3(u)
        win_sum = _dot(band, parts[0]) + _dot(band, parts[1]) + _dot(band, parts[2])
        dlt = win_sum / cnt - u
        y = _dot(_bf(dlt), _bf(w_ref[g])) * sc_ref[:, g * gd:(g + 1) * gd]
        y_ref[:, g * gd:(g + 1) * gd] = y.astype(y_ref.dtype)


def pool_mix(proj, pool_w, pool_scale, n_lat_blocks, out_cols=POOL_DIM):
    t = proj.shape[0]
    return pl.pallas_call(
        functools.partial(_pool_body, n_lat_blocks=n_lat_blocks),
        out_shape=jax.ShapeDtypeStruct((t, out_cols), jnp.bfloat16),
        grid=(t // ROW_BLOCK,),
        in_specs=[pl.BlockSpec((ROW_BLOCK, POOL_DIM), lambda i: (i, 0)),
                  pl.BlockSpec((POOL_GROUPS, POOL_GROUP_DIM, POOL_GROUP_DIM), lambda i: (0, 0, 0)),
                  pl.BlockSpec((1, POOL_DIM), lambda i: (0, 0))],
        out_specs=pl.BlockSpec((ROW_BLOCK, POOL_DIM), lambda i: (i, 0)),
        compiler_params=_cparams("parallel"),
        name="pool_mix",
    )(proj, pool_w, pool_scale.reshape(1, POOL_DIM))


def _moe_up_body(be_ref, first_ref, nused_ref, x_ref, w1_ref, w3_ref, h_ref, w1_bf, w3_bf):
    del be_ref
    i = pl.program_id(0)

    @pl.when(first_ref[i] == 1)
    def _():
        w1_bf[...] = _bf(w1_ref[0])
        w3_bf[...] = _bf(w3_ref[0])

    @pl.when(i < nused_ref[0])
    def _():
        x = x_ref[...]
        a = _dot(x, w1_bf[...])
        b = _dot(x, w3_bf[...])
        h_ref[...] = _bf(a * jax.nn.sigmoid(a) * b)


def _moe_down_body(be_ref, first_ref, nused_ref, h_ref, w2_ref, o_ref, w2_bf):
    del be_ref
    i = pl.program_id(0)

    @pl.when(first_ref[i] == 1)
    def _():
        w2_bf[...] = _bf(w2_ref[0])

    @pl.when(i < nused_ref[0])
    def _():
        o_ref[...] = _dot(h_ref[...], w2_bf[...]).astype(o_ref.dtype)


def moe_expert_blocks(xbuf, block_expert, block_first, n_used, w1, w3, w2):
    p, d = xbuf.shape
    bm = MOE_BLOCK_ROWS
    nb = p // bm
    hid = w1.shape[-1]
    up_spec = pltpu.PrefetchScalarGridSpec(
        num_scalar_prefetch=3,
        grid=(nb,),
        in_specs=[pl.BlockSpec((bm, d), lambda i, be, fi, nu: (i, 0)),
                  pl.BlockSpec((1, d, hid), lambda i, be, fi, nu: (be[i], 0, 0)),
                  pl.BlockSpec((1, d, hid), lambda i, be, fi, nu: (be[i], 0, 0))],
        out_specs=pl.BlockSpec((bm, hid), lambda i, be, fi, nu: (i, 0)),
        scratch_shapes=[pltpu.VMEM((d, hid), jnp.bfloat16)] * 2,
    )
    hbuf = pl.pallas_call(
        _moe_up_body,
        out_shape=jax.ShapeDtypeStruct((p, hid), jnp.bfloat16),
        grid_spec=up_spec,
        compiler_params=pltpu.CompilerParams(dimension_semantics=("arbitrary",), vmem_limit_bytes=MOE_UP_VMEM_LIMIT),
        name="moe_up",
    )(block_expert, block_first, n_used, xbuf, w1, w3)
    down_spec = pltpu.PrefetchScalarGridSpec(
        num_scalar_prefetch=3,
        grid=(nb,),
        in_specs=[pl.BlockSpec((bm, hid), lambda i, be, fi, nu: (i, 0)),
                  pl.BlockSpec((1, hid, d), lambda i, be, fi, nu: (be[i], 0, 0))],
        out_specs=pl.BlockSpec((bm, d), lambda i, be, fi, nu: (i, 0)),
        scratch_shapes=[pltpu.VMEM((hid, d), jnp.bfloat16)],
    )
    return pl.pallas_call(
        _moe_down_body,
        out_shape=jax.ShapeDtypeStruct((p, d), jnp.bfloat16),
        grid_spec=down_spec,
        compiler_params=_cparams("arbitrary"),
        name="moe_down",
    )(block_expert, block_first, n_used, hbuf, w2)


def _moe_route_body(h_ref, w_ref, b_ref, ids_ref, wgt_ref, cnt_ref, run_ref):
    i = pl.program_id(0)

    @pl.when(i == 0)
    def _():
        run_ref[...] = jnp.zeros_like(run_ref)

    n = h_ref.shape[0]
    ng, ne = MOE_GROUPS, MOE_EXPERTS_PER_GROUP
    logits = jnp.dot(h_ref[...].astype(jnp.float32), w_ref[...], preferred_element_type=jnp.float32,
                     precision=lax.Precision.HIGHEST) + b_ref[...]
    lane = lax.broadcasted_iota(jnp.int32, logits.shape, 1)
    neg = jnp.float32(-jnp.inf)

    lane_f = lane.astype(jnp.float32)

    def first_argmax(vals):
        top = jnp.max(vals, axis=1, keepdims=True)
        return top, jnp.min(jnp.where(vals == top, lane_f, float(LANES)), axis=1, keepdims=True).astype(jnp.int32)

    gl = jnp.where(lane < ng, logits, neg)
    g_top, group = first_argmax(gl)
    p_group = 1.0 / jnp.sum(jnp.exp(gl - g_top), axis=1, keepdims=True)
    lo = ng + group * ne
    sel = jnp.where((lane >= lo) & (lane < lo + ne), logits, neg)
    v1, i1 = first_argmax(sel)
    v2, i2 = first_argmax(jnp.where(lane == i1, neg, sel))
    e21 = jnp.exp(v2 - v1)
    w1 = p_group / (1.0 + e21)
    w2 = p_group * e21 / (1.0 + e21)
    e1 = i1 - ng
    e2 = i2 - ng
    oh1 = jnp.where(lane == e1, 1.0, 0.0)
    oh2 = jnp.where(lane == e2, 1.0, 0.0)
    both = oh1 + oh2
    ri = lax.broadcasted_iota(jnp.int32, (n, n), 0)
    ci = lax.broadcasted_iota(jnp.int32, (n, n), 1)
    before = _dot(_bf(jnp.where(ci < ri, 1.0, 0.0)), _bf(both)) + run_ref[0:1, :]
    r1 = jnp.sum(oh1 * before, axis=1, keepdims=True)
    r2 = jnp.sum(oh2 * before, axis=1, keepdims=True)
    run_ref[...] = run_ref[...] + jnp.sum(both, axis=0, keepdims=True)
    ids_ref[...] = jnp.where(lane == 0, e1, jnp.where(lane == 1, e2, jnp.where(
        lane == 2, r1.astype(jnp.int32), jnp.where(lane == 3, r2.astype(jnp.int32), 0))))
    wgt_ref[...] = jnp.where(lane == 0, w1, jnp.where(lane == 1, w2, 0.0))
    cnt_ref[...] = run_ref[...].astype(jnp.int32)


def moe_route(h, group_w, group_b, expert_w, expert_b):
    t, d = h.shape
    w = jnp.concatenate([group_w, expert_w.transpose(1, 0, 2).reshape(d, MOE_EXPERTS)], axis=1).astype(jnp.float32)
    w = _pad_cols(w)
    b = _pad_cols(jnp.concatenate([group_b, expert_b.reshape(-1)])[None, :].astype(jnp.float32))
    tm = ROW_BLOCK
    return pl.pallas_call(
        _moe_route_body,
        out_shape=(jax.ShapeDtypeStruct((t, LANES), jnp.int32), jax.ShapeDtypeStruct((t, LANES), jnp.float32),
                   jax.ShapeDtypeStruct((SUBLANES, LANES), jnp.int32)),
        grid=(t // tm,),
        in_specs=[pl.BlockSpec((tm, d), lambda i: (i, 0)), pl.BlockSpec((d, LANES), lambda i: (0, 0)),
                  pl.BlockSpec((1, LANES), lambda i: (0, 0))],
        out_specs=(pl.BlockSpec((tm, LANES), lambda i: (i, 0)), pl.BlockSpec((tm, LANES), lambda i: (i, 0)),
                   pl.BlockSpec((SUBLANES, LANES), lambda i: (0, 0))),
        scratch_shapes=[pltpu.VMEM((SUBLANES, LANES), jnp.float32)],
        compiler_params=_cparams("arbitrary"),
        name="moe_route",
    )(h, w, b)


def hier_moe(h, group_w, group_b, expert_w, expert_b, w1, w3, w2, expert0):
    t, d = h.shape
    ids, weight, counts = moe_route(h, group_w, group_b, expert_w, expert_b)
    expert = ids[:, :MOE_TOP_K].T.reshape(-1)
    rank = ids[:, MOE_TOP_K:2 * MOE_TOP_K].T.reshape(-1)
    counts = counts[0, :MOE_EXPERTS]
    tk = t * MOE_TOP_K
    bm = MOE_BLOCK_ROWS
    padded = (counts + bm - 1) // bm * bm
    pad_end = jnp.cumsum(padded)
    pad_start = pad_end - padded
    dest = pad_start[expert] + rank
    n_blocks = -(-tk // bm) + MOE_EXPERTS
    token = jnp.tile(jnp.arange(t, dtype=jnp.int32), MOE_TOP_K)
    src = jnp.zeros((n_blocks * bm,), jnp.int32).at[dest].set(token)
    block_expert = jnp.minimum(jnp.searchsorted(pad_end, jnp.arange(n_blocks) * bm, side='right'),
                               MOE_EXPERTS - 1).astype(jnp.int32)
    block_first = jnp.concatenate([jnp.ones((1,), jnp.int32),
                                   (block_expert[1:] != block_expert[:-1]).astype(jnp.int32)])
    n_used = (pad_end[-1:] // bm).astype(jnp.int32)
    xbuf = h[src]
    ybuf = moe_expert_blocks(xbuf, block_expert + expert0, block_first, n_used, w1, w3, w2)
    return ybuf[dest], weight


GLA_SUB = 32
GLA_STEP = 64


def _gla_prep_body(q_ref, k_ref, v_ref, lr_ref, wg_ref, gb_ref, qd_ref, ke_ref, at_ref, ge_ref, v16_ref):
    n = ROW_BLOCK
    c = GLA_STEP
    sb = GLA_SUB
    n_sb = n // sb
    q = q_ref[...] * (GLA_HEAD_K ** -0.5)
    k = k_ref[...]
    v16_ref[...] = _bf(v_ref[...])
    lr = lr_ref[...]
    ri = lax.broadcasted_iota(jnp.int32, (n, n), 0)
    ci = lax.broadcasted_iota(jnp.int32, (n, n), 1)
    cs = c.bit_length() - 1
    ss = sb.bit_length() - 1
    same_c = (ri >> cs) == (ci >> cs)
    same_s = (ri >> ss) == (ci >> ss)
    dirs = range(2)
    xs = [jnp.dot(lr, wg_ref[d], preferred_element_type=jnp.float32, precision=lax.Precision.HIGHEST) + gb_ref[d]
          for d in dirs]
    gks = [_split3((jnp.minimum(x, 0.0) - jnp.log1p(jnp.exp(-jnp.abs(x)))) * (1.0 / GLA_GATE_TEMP)) for x in xs]
    runs = [_bf(jnp.where(same_c & ((ci <= ri) if d == 0 else (ci >= ri)), 1.0, 0.0)) for d in dirs]
    gs = [_dot(runs[d], gks[d][0]) + _dot(runs[d], gks[d][1]) + _dot(runs[d], gks[d][2]) for d in dirs]
    locs, tots, totc = [], [], []
    for d in dirs:
        g = gs[d]
        off, tot_s, tot_c = [], [], []
        for b in range(n_sb):
            lo = b * sb
            inner = (b % 2 == 1) if d == 0 else (b % 2 == 0)
            edge = g[lo + sb - 1:lo + sb] if d == 0 else g[lo:lo + 1]
            if inner:
                prev = g[lo - 1:lo] if d == 0 else g[lo + sb:lo + sb + 1]
                off.append(jnp.broadcast_to(prev, (sb, prev.shape[1])))
                tot_s.append(jnp.broadcast_to(edge - prev, (sb, prev.shape[1])))
            else:
                off.append(jnp.zeros((sb, g.shape[1]), jnp.float32))
                tot_s.append(jnp.broadcast_to(edge, (sb, edge.shape[1])))
        for j in range(n // c):
            edge = g[j * c + c - 1:j * c + c] if d == 0 else g[j * c:j * c + 1]
            tot_c.append(jnp.broadcast_to(edge, (c, edge.shape[1])))
        locs.append(g - jnp.concatenate(off, axis=0))
        tots.append(jnp.concatenate(tot_s, axis=0))
        totc.append(jnp.concatenate(tot_c, axis=0))
    qd_loc = [_bf(q * jnp.exp(locs[d])) for d in dirs]
    m_diag = [_dot_nt(qd_loc[d], _bf(k * jnp.exp(-locs[d]))) for d in dirs]
    m_cross = [_dot_nt(qd_loc[d], _bf(k * jnp.exp(tots[d] - locs[d]))) for d in dirs]
    for d in dirs:
        tri = (ci <= ri) if d == 0 else (ci >= ri)
        nxt = ((ri >> ss) == (ci >> ss) + 1) if d == 0 else ((ri >> ss) + 1 == (ci >> ss))
        attn = jnp.where(same_s & tri, m_diag[d], 0.0) + jnp.where(same_c & nxt, m_cross[d], 0.0)
        qd_ref[d] = _bf(q * jnp.exp(gs[d]))
        ke_ref[d] = _bf(k * jnp.exp(totc[d] - gs[d]))
        for j in range(n // c):
            at_ref[d, 0, j * c:(j + 1) * c, :] = _bf(attn[j * c:(j + 1) * c, j * c:(j + 1) * c])
            ge_ref[d, j * SUBLANES:(j + 1) * SUBLANES, :] = jnp.exp(totc[d][j * c:j * c + SUBLANES])


def gla_prepare(proj, gate_up, gate_b):
    t = proj.shape[0]
    assert CD_MAIN % LANES == 0
    hk, hv = GLA_HEAD_K, GLA_HEAD_V
    wg = jnp.zeros((2, LANES, GLA_K_DIM), jnp.float32)
    for d in range(2):
        wg = wg.at[d, d * GLA_GATE_RANK:(d + 1) * GLA_GATE_RANK].set(gate_up[d].astype(jnp.float32))
    gb = gate_b.astype(jnp.float32).reshape(2, 1, GLA_K_DIM)
    q0, k0, v0 = CD_SPLITS[0] // hk, CD_SPLITS[1] // hk, CD_SPLITS[2] // hv
    assert CD_SPLITS[0] % hk == 0 and CD_SPLITS[1] % hk == 0 and CD_SPLITS[2] % hv == 0
    return pl.pallas_call(
        _gla_prep_body,
        out_shape=(jax.ShapeDtypeStruct((2, t, GLA_K_DIM), jnp.bfloat16),
                   jax.ShapeDtypeStruct((2, t, GLA_K_DIM), jnp.bfloat16),
                   jax.ShapeDtypeStruct((2, GLA_HEADS, t, GLA_STEP), jnp.bfloat16),
                   jax.ShapeDtypeStruct((2, t // SUBLANES, GLA_K_DIM), jnp.float32),
                   jax.ShapeDtypeStruct((t, GLA_V_DIM), jnp.bfloat16)),
        grid=(t // ROW_BLOCK, GLA_HEADS),
        in_specs=[pl.BlockSpec((ROW_BLOCK, hk), lambda rb, h: (rb, q0 + h)),
                  pl.BlockSpec((ROW_BLOCK, hk), lambda rb, h: (rb, k0 + h)),
                  pl.BlockSpec((ROW_BLOCK, hv), lambda rb, h: (rb, v0 + h)),
                  pl.BlockSpec((ROW_BLOCK, LANES), lambda rb, h: (rb, CD_MAIN // LANES)),
                  pl.BlockSpec((2, LANES, hk), lambda rb, h: (0, 0, h)),
                  pl.BlockSpec((2, 1, hk), lambda rb, h: (0, 0, h))],
        out_specs=(pl.BlockSpec((2, ROW_BLOCK, hk), lambda rb, h: (0, rb, h)),
                   pl.BlockSpec((2, ROW_BLOCK, hk), lambda rb, h: (0, rb, h)),
                   pl.BlockSpec((2, 1, ROW_BLOCK, GLA_STEP), lambda rb, h: (0, h, rb, 0)),
                   pl.BlockSpec((2, ROW_BLOCK // SUBLANES, hk), lambda rb, h: (0, rb, h)),
                   pl.BlockSpec((ROW_BLOCK, hv), lambda rb, h: (rb, h))),
        compiler_params=_cparams("parallel", "parallel"),
        name="gla_prepare",
    )(proj, proj, proj, proj, wg, gb)


def _gla_scan_body(qf, kf, af, gf, vf, qb, kb, ab, gb, vb, of_ref, ob_ref, s_ref):
    t = pl.program_id(2)
    c = GLA_STEP
    n_ch = ROW_BLOCK // c

    @pl.when(t == 0)
    def _():
        s_ref[...] = jnp.zeros_like(s_ref)

    views = ((qf, kf, af, gf, vf, of_ref), (qb, kb, ab, gb, vb, ob_ref))
    for step in range(n_ch):
        chunk = [step, n_ch - 1 - step]
        rows = [slice(ch * c, (ch + 1) * c) for ch in chunk]
        st = [s_ref[d] for d in range(2)]
        inter = [_dot_nt(views[d][0][0, rows[d], :], _bf(st[d])) for d in range(2)]
        intra = [_dot(views[d][2][0, 0, rows[d], :], views[d][4][rows[d], :]) for d in range(2)]
        upd = [_dot_tn(views[d][4][rows[d], :], views[d][1][0, rows[d], :]) for d in range(2)]
        for d in range(2):
            views[d][5][rows[d], :] = inter[d] + intra[d]
            ge = views[d][3][0, chunk[d] * SUBLANES:chunk[d] * SUBLANES + 1, :]
            s_ref[d] = st[d] * ge + upd[d]


def gla_scan(qd, ke, at, ge, v16, n_batch, blocks_per_seq):
    hk, hv = GLA_HEAD_K, GLA_HEAD_V
    t = v16.shape[0]
    n_lat_blocks = n_batch * blocks_per_seq

    def fwd(b, s):
        return jnp.where(s == 0, n_lat_blocks + b, b * blocks_per_seq + s - 1)

    def bwd(b, s):
        return jnp.where(s == 0, n_lat_blocks + b, b * blocks_per_seq + blocks_per_seq - s)

    def out_blk(blk):
        return lambda b, h, s: (blk(b, s), h)

    def key_spec(d, blk, rows):
        return pl.BlockSpec((1, rows, hk), lambda b, h, s: (d, blk(b, s), h))

    def attn_spec(d, blk):
        return pl.BlockSpec((1, 1, ROW_BLOCK, GLA_STEP), lambda b, h, s: (d, h, blk(b, s), 0))

    def val_spec(blk):
        return pl.BlockSpec((ROW_BLOCK, hv), lambda b, h, s: (blk(b, s), h))

    in_specs = []
    for d, blk in ((0, fwd), (1, bwd)):
        in_specs += [key_spec(d, blk, ROW_BLOCK), key_spec(d, blk, ROW_BLOCK), attn_spec(d, blk),
                     key_spec(d, blk, ROW_BLOCK // SUBLANES), val_spec(blk)]
    o_f, o_b = pl.pallas_call(
        _gla_scan_body,
        out_shape=(jax.ShapeDtypeStruct((t, GLA_V_DIM), jnp.float32),) * 2,
        grid=(n_batch, GLA_HEADS, blocks_per_seq + 1),
        in_specs=in_specs,
        out_specs=(pl.BlockSpec((ROW_BLOCK, hv), out_blk(fwd)), pl.BlockSpec((ROW_BLOCK, hv), out_blk(bwd))),
        scratch_shapes=[pltpu.VMEM((2, hv, hk), jnp.float32)],
        compiler_params=_cparams("parallel", "parallel", "arbitrary"),
        name="gla_scan",
    )(qd, ke, at, ge, v16, qd, ke, at, ge, v16)
    return o_f, o_b


def _fourier_in_body(x_ref, w_ref, o_ref):
    y = _dot(_bf(x_ref[...]), w_ref[0])
    half = y.shape[1] // 2
    o_ref[0] = _bf(y[:, :half])
    o_ref[1] = _bf(y[:, half:])


def dft_tables(n, nc):
    r = 64
    assert n % r == 0
    kk = jnp.arange(n, dtype=jnp.int32)[None, :]
    j1 = jnp.arange(n // r, dtype=jnp.int32)[:, None]
    j0 = jnp.arange(r, dtype=jnp.int32)[:, None]
    a = ((j1 * kk) % (n // r)).astype(jnp.float32) * (2.0 * jnp.pi / (n // r))
    b = ((j0 * kk) % n).astype(jnp.float32) * (2.0 * jnp.pi / n)
    ca, sa, cb, sb = jnp.cos(a)[:, None, :], jnp.sin(a)[:, None, :], jnp.cos(b)[None, :, :], jnp.sin(b)[None, :, :]
    scale = n ** -0.5
    cos_n = ((ca * cb - sa * sb) * scale).reshape(n, n)
    sin_n = ((sa * cb + ca * sb) * scale).reshape(n, n)
    table = jnp.concatenate([cos_n, -sin_n], axis=1).astype(jnp.bfloat16)
    cc = jnp.arange(nc, dtype=jnp.int32)
    ang = ((cc[:, None] * cc[None, :]) % nc).astype(jnp.float32) * (2.0 * jnp.pi / nc)
    return table, jnp.cos(ang) * nc ** -0.5, jnp.sin(ang) * nc ** -0.5


def fourier_mix(proj, fourier_w, n_batch, n_lat, out_cols=FOURIER_DIM):
    gd = FOURIER_GROUP_DIM
    ng = FOURIER_GROUPS
    table, cos_c, sin_c = dft_tables(n_lat, gd)
    w_all = fourier_w.astype(jnp.float32).transpose(1, 0, 2).reshape(gd, ng * gd)
    folded = dense_matmul([(jnp.concatenate([cos_c, sin_c], axis=0), w_all)], tm=2 * gd, tn=512)
    w2 = jnp.concatenate([folded[:gd].reshape(gd, ng, gd), folded[gd:].reshape(gd, ng, gd)], axis=2)
    w2 = w2.transpose(1, 0, 2).astype(jnp.bfloat16)
    tm = min(DENSE_TM, n_lat)
    tiles = n_lat // tm
    z = pl.pallas_call(
        _fourier_in_body,
        out_shape=jax.ShapeDtypeStruct((2, n_lat, n_batch * ng * gd), jnp.bfloat16),
        grid=(n_batch, ng, tiles),
        in_specs=[pl.BlockSpec((tm, gd), lambda b, g, i: (b * tiles + i, g)),
                  pl.BlockSpec((1, gd, 2 * gd), lambda b, g, i: (g, 0, 0))],
        out_specs=pl.BlockSpec((2, tm, gd), lambda b, g, i: (0, i, b * ng + g)),
        compiler_params=_cparams("parallel", "parallel", "parallel"),
        name="fourier_in",
    )(proj, w2)
    z = z.reshape(2 * n_lat, n_batch * ng * gd)
    tn = 512
    per_b = ng * gd // tn
    return dense_matmul([(table, z)], out_dtype=jnp.bfloat16, tm=512, tn=tn,
                        out_shape=(n_batch * n_lat, out_cols),
                        out_index=lambda i, j: ((j // per_b) * (n_lat // 512) + i, j % per_b))


def modulation(c, c_ctx, mod_w, mod_b, layer):
    b, d = c.shape
    rows = jnp.concatenate([c, c_ctx[None, :], jnp.zeros((SUBLANES - b - 1, d), c.dtype)], axis=0)
    out = dense_matmul([(jax.nn.silu(rows), mod_w.reshape(-1, mod_w.shape[-1]))], tm=SUBLANES, tn=512,
                       w_row_block=layer) + mod_b[layer]
    return tuple(m[:, None, :] for m in jnp.split(out, N_MOD, axis=-1))


def _combine_body(x_ref, y0_ref, y1_ref, w_ref, gate_ref, nw_ref, o_ref, *, final_norm):
    y = w_ref[:, 0:1] * y0_ref[...].astype(jnp.float32) + w_ref[:, 1:2] * y1_ref[...].astype(jnp.float32)
    x = x_ref[...] + gate_ref[0] * y
    if final_norm:
        x = x * lax.rsqrt(jnp.mean(x * x, axis=-1, keepdims=True) + NORM_EPS) * nw_ref[...]
    o_ref[...] = x


def moe_combine(x, y2, weight, gates, gate_index, final_norm_w=None):
    m, d = y2.shape[0] // MOE_TOP_K, x.shape[1]
    tm = ROW_BLOCK
    nw = jnp.ones((1, d), jnp.float32) if final_norm_w is None else final_norm_w.reshape(1, d).astype(jnp.float32)
    return pl.pallas_call(
        functools.partial(_combine_body, final_norm=final_norm_w is not None),
        out_shape=jax.ShapeDtypeStruct((m, d), jnp.float32),
        grid=(m // tm,),
        in_specs=[pl.BlockSpec((tm, d), lambda i: (i, 0)),
                  pl.BlockSpec((tm, d), lambda i: (i, 0)),
                  pl.BlockSpec((tm, d), lambda i: (m // tm + i, 0)),
                  pl.BlockSpec((tm, LANES), lambda i: (i, 0)),
                  pl.BlockSpec((1, 1, d), lambda i: (gate_index(i), 0, 0)),
                  pl.BlockSpec((1, d), lambda i: (0, 0))],
        out_specs=pl.BlockSpec((tm, d), lambda i: (i, 0)),
        compiler_params=_cparams("parallel"),
        name="moe_combine",
    )(x, y2, y2, weight, gates, nw)


def kernel(x, c, ctx, c_ctx, mod_w, mod_b, norm1_w, norm2_w, ab_w_in, pool_w, pool_scale, gdn_conv_w,
           gdn_a_log, gdn_dt_bias, gdn_norm_w, ab_w_out, cd_w_in, fourier_w, gla_gate_up, gla_gate_b,
           gla_norm_w, cd_w_out, moe_group_w, moe_group_b, moe_expert_w, moe_expert_b, moe_w1, moe_w3,
           moe_w2, final_norm_w):
    assert DEPTH == 2
    n_batch, n_lat, d = x.shape
    n_ctx = ctx.shape[1]
    assert n_ctx == ROW_BLOCK and n_lat % DENSE_TM == 0 and (n_batch * n_ctx) % DENSE_TM == 0
    t_lat = n_batch * n_lat
    blocks_per_seq = n_lat // ROW_BLOCK
    n_lat_blocks = n_batch * blocks_per_seq
    xs = jnp.concatenate([x.reshape(t_lat, d), ctx.reshape(n_batch * n_ctx, d)], axis=0)

    def mod_row(rows_per_tile):
        tiles_per_seq = n_lat // rows_per_tile
        return lambda i: jnp.minimum(i // tiles_per_seq, n_batch)

    bf = jnp.bfloat16
    w1_all, w3_all, w2_all = (w.reshape((-1,) + w.shape[2:]) for w in (moe_w1, moe_w3, moe_w2))
    sh1, sc1, g1, sh2, sc2, g2 = modulation(c, c_ctx, mod_w, mod_b, 0)
    h = norm_modulate(xs, norm1_w[0], sh1, sc1, mod_row(512))
    proj = dense_matmul([(h, _pad_cols(ab_w_in[0].astype(bf), DENSE_TN))])
    gates, gates_t = gdn_gates(proj, AB_MAIN, gdn_a_log[0], gdn_dt_bias[0], GDN_HEADS)
    u, w, qd, ke, at, ge = gdn_prepare(proj, gates, gates_t, gdn_conv_w[0], GDN_HEADS, POOL_DIM, blocks_per_seq,
                                       n_lat_blocks)
    o_f, o_b = gdn_scan(u, w, qd, ke, at, ge, GDN_HEADS, n_batch, blocks_per_seq)
    y = pool_mix(proj, pool_w[0], pool_scale[0], n_lat_blocks, out_cols=d)
    y = gated_head_norm(o_f, o_b, proj, POOL_DIM + 3 * GDN_DIM, gdn_norm_w[0], GDN_HEAD_DIM, y, POOL_DIM)
    xs = dense_matmul([(y, ab_w_out[0].astype(bf))], residual=xs, gates=g1, gate_index=mod_row(DENSE_TM))
    f = norm_modulate(xs, norm2_w[0], sh2, sc2, mod_row(512))
    y2, wgt = hier_moe(f, moe_group_w[0], moe_group_b[0], moe_expert_w[0], moe_expert_b[0], w1_all, w3_all, w2_all, 0)
    xs = moe_combine(xs, y2, wgt, g2, mod_row(ROW_BLOCK))

    sh1, sc1, g1, sh2, sc2, g2 = modulation(c, c_ctx, mod_w, mod_b, 1)
    h = norm_modulate(xs, norm1_w[1], sh1, sc1, mod_row(512))
    proj = dense_matmul([(h, _pad_cols(cd_w_in[0].astype(bf), DENSE_TN))])
    qd, ke, at, ge, v16 = gla_prepare(proj, gla_gate_up[0], gla_gate_b[0])
    o_f, o_b = gla_scan(qd, ke, at, ge, v16, n_batch, blocks_per_seq)
    y = fourier_mix(proj, fourier_w[0], n_batch, n_lat, out_cols=d)
    y = gated_head_norm(o_f, o_b, proj, CD_SPLITS[3], gla_norm_w[0], GLA_HEAD_V, y, FOURIER_DIM)
    xl = dense_matmul([(y, cd_w_out[0].astype(bf))], residual=xs, gates=g1, gate_index=mod_row(DENSE_TM))
    f = norm_modulate(xl, norm2_w[1], sh2, sc2, mod_row(512))
    y2, wgt = hier_moe(f, moe_group_w[1], moe_group_b[1], moe_expert_w[1], moe_expert_b[1], w1_all, w3_all, w2_all,
                       MOE_EXPERTS)
    out = moe_combine(xl, y2, wgt, g2, mod_row(ROW_BLOCK), final_norm_w=final_norm_w)
    return out.reshape(n_batch, n_lat, d)
```

```python
import functools

import jax
import jax.numpy as jnp
from jax import lax
from jax.experimental import pallas as pl
from jax.experimental.pallas import tpu as pltpu

D_MODEL = 4096
DEPTH = 2
CTX_LEN = 256
GRID_W = 64
N_MOD = 6
NORM_EPS = 1e-6

POOL_WINDOWS = (2, 4, 8, 16)
POOL_GROUPS = 4
POOL_GROUP_DIM = D_MODEL // 16
POOL_DIM = POOL_GROUPS * POOL_GROUP_DIM

GDN_HEAD_DIM = 128
GDN_DIM = D_MODEL - POOL_DIM
GDN_HEADS = GDN_DIM // GDN_HEAD_DIM
GDN_CONV = 5
GDN_CHUNK = 64

FOURIER_GROUPS = 4
FOURIER_GROUP_DIM = D_MODEL // 16
FOURIER_DIM = FOURIER_GROUPS * FOURIER_GROUP_DIM

GLA_HEADS = 6
GLA_V_DIM = D_MODEL - FOURIER_DIM
GLA_K_DIM = GLA_V_DIM // 2
GLA_HEAD_K = GLA_K_DIM // GLA_HEADS
GLA_HEAD_V = GLA_V_DIM // GLA_HEADS
GLA_GATE_RANK = 16
GLA_GATE_TEMP = 16.0
GLA_CHUNK = 32

MOE_GROUPS = 4
MOE_EXPERTS_PER_GROUP = 8
MOE_EXPERTS = MOE_GROUPS * MOE_EXPERTS_PER_GROUP
MOE_TOP_K = 2
MOE_HIDDEN = D_MODEL // 8

AB_MAIN = POOL_DIM + 4 * GDN_DIM
CD_SPLITS = (FOURIER_DIM, FOURIER_DIM + GLA_K_DIM, FOURIER_DIM + 2 * GLA_K_DIM,
             FOURIER_DIM + 2 * GLA_K_DIM + GLA_V_DIM, FOURIER_DIM + 2 * GLA_K_DIM + 2 * GLA_V_DIM)
CD_MAIN = FOURIER_DIM + 2 * GLA_K_DIM + 2 * GLA_V_DIM

LANES = 128
SUBLANES = 8
VMEM_LIMIT = 48 * 1024 * 1024
MOE_UP_VMEM_LIMIT = 56 * 1024 * 1024
MOE_BLOCK_ROWS = 512
ROW_BLOCK = 256
DENSE_TM = 1024
DENSE_TN = 512
GDN_PREP_HEADS = 4
GDN_SCAN_HEADS = 8


def _cparams(*sem):
    return pltpu.CompilerParams(dimension_semantics=sem, vmem_limit_bytes=VMEM_LIMIT)


def _dot(a, b):
    return jnp.dot(a, b, preferred_element_type=jnp.float32)


def _dot_nt(a, b):
    return lax.dot_general(a, b, (((1,), (1,)), ((), ())), preferred_element_type=jnp.float32)


def _dot_tn(a, b):
    return lax.dot_general(a, b, (((0,), (0,)), ((), ())), preferred_element_type=jnp.float32)


def _bf(x):
    return x.astype(jnp.bfloat16)


def _split3(x):
    hi = _bf(x)
    r1 = x - hi.astype(jnp.float32)
    mid = _bf(r1)
    lo = _bf(r1 - mid.astype(jnp.float32))
    return hi, mid, lo


def _mm_body(*refs, n_pairs, has_res, has_init):
    del has_init
    o_ref = refs[-1]
    acc = _dot(refs[0][...], refs[1][...])
    for p in range(1, n_pairs):
        acc = acc + _dot(refs[2 * p][...], refs[2 * p + 1][...])
    if has_res:
        res_ref, gate_ref = refs[2 * n_pairs], refs[2 * n_pairs + 1]
        acc = res_ref[...] + gate_ref[0] * acc
    o_ref[...] = acc.astype(o_ref.dtype)


def dense_matmul(pairs, out_dtype=jnp.float32, tm=DENSE_TM, tn=512, m=None, n=None, w_row_block=0, residual=None,
                 gates=None, gate_index=None, out_shape=None, out_index=None, out_init=None):
    m = pairs[0][0].shape[0] if m is None else m
    n = pairs[0][1].shape[1] if n is None else n
    tm = min(tm, m)
    tn = min(tn, n)
    assert m % tm == 0 and n % tn == 0, (m, n, tm, tn)
    in_specs, args = [], []
    for x, w in pairs:
        k = x.shape[1]
        assert w.shape[0] % k == 0
        in_specs += [pl.BlockSpec((tm, k), lambda i, j: (i, 0)),
                     pl.BlockSpec((k, tn), lambda i, j: (w_row_block, j))]
        args += [x, w]
    if residual is not None:
        in_specs += [pl.BlockSpec((tm, tn), lambda i, j: (i, j)),
                     pl.BlockSpec((1, 1, tn), lambda i, j: (gate_index(i), 0, j))]
        args += [residual, gates]
    aliases = {}
    if out_init is not None:
        assert out_init.dtype == out_dtype
        aliases = {len(args): 0}
        in_specs += [pl.BlockSpec(memory_space=pl.ANY)]
        args += [out_init]
        out_shape = out_init.shape
    return pl.pallas_call(
        functools.partial(_mm_body, n_pairs=len(pairs), has_res=residual is not None, has_init=out_init is not None),
        out_shape=jax.ShapeDtypeStruct((m, n) if out_shape is None else out_shape, out_dtype),
        grid=(m // tm, n // tn),
        in_specs=in_specs,
        out_specs=pl.BlockSpec((tm, tn), (lambda i, j: (i, j)) if out_index is None else out_index),
        input_output_aliases=aliases,
        compiler_params=_cparams("parallel", "parallel"),
        name="dense_matmul",
    )(*args)


def _pad_cols(w, mult=LANES):
    pad = (-w.shape[1]) % mult
    return jnp.pad(w, ((0, 0), (0, pad))) if pad else w


def _norm_mod_body(x_ref, nw_ref, shift_ref, scale_ref, o_ref):
    x = x_ref[...]
    y = x * lax.rsqrt(jnp.mean(x * x, axis=-1, keepdims=True) + NORM_EPS) * nw_ref[...]
    o_ref[...] = (y * (1.0 + scale_ref[0]) + shift_ref[0]).astype(o_ref.dtype)


def norm_modulate(x, norm_w, shift, scale, row_index, m=None, tm=512):
    m = x.shape[0] if m is None else m
    d = x.shape[1]
    vec = pl.BlockSpec((1, 1, d), lambda i: (row_index(i), 0, 0))
    return pl.pallas_call(
        _norm_mod_body,
        out_shape=jax.ShapeDtypeStruct((m, d), jnp.bfloat16),
        grid=(m // tm,),
        in_specs=[pl.BlockSpec((tm, d), lambda i: (i, 0)), pl.BlockSpec((1, d), lambda i: (0, 0)), vec, vec],
        out_specs=pl.BlockSpec((tm, d), lambda i: (i, 0)),
        compiler_params=_cparams("parallel"),
        name="norm_modulate",
    )(x, norm_w.reshape(1, d), shift, scale)


def _gdn_gates_body(tail_ref, par_ref, o_ref, ot_ref, *, n_heads, chunk):
    t = tail_ref[...]
    rows = t.shape[0]
    lane = lax.broadcasted_iota(jnp.int32, t.shape, 1)
    a_row = par_ref[0:1, :]
    dtb_row = par_ref[1:2, :]
    beta = jax.nn.sigmoid(t)
    z = t + dtb_row
    g = a_row * (jnp.maximum(z, 0.0) + jnp.log1p(jnp.exp(-jnp.abs(z))))
    ri = lax.broadcasted_iota(jnp.int32, (rows, rows), 0)
    ci = lax.broadcasted_iota(jnp.int32, (rows, rows), 1)
    shift = chunk.bit_length() - 1
    same = (ri >> shift) == (ci >> shift)
    pre = _bf(jnp.where(same & (ci <= ri), 1.0, 0.0))
    suf = _bf(jnp.where(same & (ci >= ri), 1.0, 0.0))
    ones = _bf(jnp.where(same, 1.0, 0.0))
    parts = _split3(g)
    cf = _dot(pre, parts[0]) + _dot(pre, parts[1]) + _dot(pre, parts[2])
    cb = _dot(suf, parts[0]) + _dot(suf, parts[1]) + _dot(suf, parts[2])
    tot = _dot(ones, parts[0]) + _dot(ones, parts[1]) + _dot(ones, parts[2])
    main = jnp.where(lane < 2 * n_heads, beta,
                     jnp.where(lane < 3 * n_heads, cf, jnp.where(lane < 4 * n_heads, cb, 0.0)))
    o_ref[:, :LANES] = main
    o_ref[:, LANES:] = tot
    ot_ref[...] = main.T


def gdn_gates(proj, tail_col0, a_log, dt_bias, n_heads):
    t = proj.shape[0]
    assert tail_col0 % LANES == 0
    par = jnp.zeros((SUBLANES, LANES), jnp.float32)
    par = par.at[0, 2 * n_heads:4 * n_heads].set(-jnp.exp(a_log.astype(jnp.float32)).reshape(-1))
    par = par.at[1, 2 * n_heads:4 * n_heads].set(dt_bias.astype(jnp.float32).reshape(-1))
    return pl.pallas_call(
        functools.partial(_gdn_gates_body, n_heads=n_heads, chunk=GDN_CHUNK),
        out_shape=(jax.ShapeDtypeStruct((t, 2 * LANES), jnp.float32),
                   jax.ShapeDtypeStruct((t // ROW_BLOCK * LANES, ROW_BLOCK), jnp.float32)),
        grid=(t // ROW_BLOCK,),
        in_specs=[pl.BlockSpec((ROW_BLOCK, LANES), lambda i: (i, tail_col0 // LANES)),
                  pl.BlockSpec((SUBLANES, LANES), lambda i: (0, 0))],
        out_specs=(pl.BlockSpec((ROW_BLOCK, 2 * LANES), lambda i: (i, 0)),
                   pl.BlockSpec((LANES, ROW_BLOCK), lambda i: (i, 0))),
        compiler_params=_cparams("parallel"),
        name="gdn_gates",
    )(proj, par)


def _gdn_prep_body(q_ref, k_ref, v_ref, qp_ref, kp_ref, vp_ref, qn_ref, kn_ref, vn_ref,
                   wq_ref, wk_ref, wv_ref, gt_ref, gtt_ref,
                   u_ref, w_ref, qd_ref, ke_ref, at_ref, ge_ref, *, heads, n_heads, blocks_per_seq, n_lat_blocks):
    rb = pl.program_id(0)
    hg = pl.program_id(1)
    n = ROW_BLOCK
    c = GDN_CHUNK
    n_ch = n // c
    cs = c.bit_length() - 1
    hd = GDN_HEAD_DIM
    is_ctx = rb >= n_lat_blocks
    first = jnp.logical_or(is_ctx, rb % blocks_per_seq == 0)
    last = jnp.logical_or(is_ctx, rb % blocks_per_seq == blocks_per_seq - 1)
    m_prev = jnp.where(first, 0.0, 1.0)
    m_next = jnp.where(last, 0.0, 1.0)

    def conv_silu(cur_ref, prev_ref, next_ref, cw_ref):
        ext = jnp.concatenate([prev_ref[...] * m_prev, cur_ref[...], next_ref[...] * m_next], axis=0)
        acc = None
        for j in range(GDN_CONV):
            lo = SUBLANES - GDN_CONV // 2 + j
            term = ext[lo:lo + ROW_BLOCK, :] * cw_ref[j:j + 1, :]
            acc = term if acc is None else acc + term
        return acc * jax.nn.sigmoid(acc)

    qc = conv_silu(q_ref, qp_ref, qn_ref, wq_ref)
    kc = conv_silu(k_ref, kp_ref, kn_ref, wk_ref)
    vc = conv_silu(v_ref, vp_ref, vn_ref, wv_ref)

    gt_lo = gt_ref[:, :LANES]
    gt_hi = gt_ref[:, LANES:]
    lane = lax.broadcasted_iota(jnp.int32, (n, LANES), 1)
    blk16 = _bf(jnp.where((lax.broadcasted_iota(jnp.int32, (n, n), 0) >> cs)
                          == (lax.broadcasted_iota(jnp.int32, (n, n), 1) >> cs), 1.0, 0.0))
    wr = lax.broadcasted_iota(jnp.int32, (c, n), 0)
    wc = lax.broadcasted_iota(jnp.int32, (c, n), 1)
    eye_wide = jnp.where(wr == (wc & (c - 1)), 1.0, 0.0)
    sr = lax.broadcasted_iota(jnp.int32, (c, LANES), 0)
    sc = lax.broadcasted_iota(jnp.int32, (c, LANES), 1)
    slab_masks = []
    for odd in range(LANES // c):
        rel = sc - odd * c
        own = (sc >> cs) == odd
        slab_masks.append(((own & (rel <= sr), own & (rel < sr)), (own & (rel >= sr), own & (rel > sr))))

    def tall(wide):
        return jnp.concatenate([_bf(wide)] * n_ch, axis=0) * blk16

    def column(src, idx):
        return jnp.sum(jnp.where(lane == idx, src, 0.0), axis=1, keepdims=True)

    per_tile = LANES // c
    chains = []
    for gi in range(heads):
        h = hg * heads + gi
        sl = slice(gi * hd, (gi + 1) * hd)
        qh, kh, v = qc[:, sl], kc[:, sl], vc[:, sl]
        q = qh * lax.rsqrt(jnp.sum(qh * qh, axis=-1, keepdims=True) + NORM_EPS) * (hd ** -0.5)
        k = kh * lax.rsqrt(jnp.sum(kh * kh, axis=-1, keepdims=True) + NORM_EPS)
        k16 = _bf(k)
        qk = _dot_nt(_bf(q), k16)
        kk = _dot_nt(k16, k16)
        for d in range(2):
            beta = column(gt_lo, d * n_heads + h)
            gc = column(gt_lo, (2 + d) * n_heads + h)
            tot = column(gt_hi, (2 + d) * n_heads + h)
            gc_row = gtt_ref[pl.ds((2 + d) * n_heads + h, 1), :]
            p_tiles = []
            for j in range(n_ch):
                rows = slice(j * c, (j + 1) * c)
                cols = slice((j // per_tile) * LANES, (j // per_tile + 1) * LANES)
                incl, strict = slab_masks[j % per_tile][d]
                decay = jnp.where(incl, jnp.exp(jnp.where(incl, gc[rows] - gc_row[:, cols], 0.0)), 0.0)
                a = jnp.where(strict, (beta[rows] * kk[rows, cols]) * decay, 0.0)
                if j % per_tile == 0:
                    p_tiles.append(-a)
                else:
                    p_tiles[-1] = p_tiles[-1] - a
                lo = (j % per_tile) * c
                at_ref[d, rows, gi * c:(gi + 1) * c] = _bf((qk[rows, cols] * decay)[:, lo:lo + c])
                ge_ref[d, j * SUBLANES:(j + 1) * SUBLANES, sl] = jnp.broadcast_to(
                    jnp.exp(tot[j * c:j * c + SUBLANES]), (SUBLANES, hd))
            p = jnp.concatenate(p_tiles, axis=1)
            eg = jnp.exp(gc)
            kbeta = k * beta
            rhs = _bf(jnp.concatenate([v * beta, kbeta * eg], axis=1))
            qd_ref[d, :, sl] = _bf(q * eg)
            ke_ref[d, :, sl] = _bf(k * jnp.exp(tot - gc))
            chains.append([d, sl, p, eye_wide + p, tall(p), rhs])

    span = 2
    while span < c:
        for ch in chains:
            ch[2] = _dot(_bf(ch[2]), ch[4])
            ch[4] = tall(ch[2])
            ch[3] = ch[3] + _dot(_bf(ch[3]), ch[4])
        span *= 2
    for d, sl, _, t, _, rhs in chains:
        sol = _dot(tall(t), rhs)
        u_ref[d, :, sl] = sol[:, :hd]
        w_ref[d, :, sl] = _bf(sol[:, hd:])


def gdn_prepare(proj, gates, gates_t, conv_w, n_heads, qkv_col0, blocks_per_seq, n_lat_blocks):
    t = proj.shape[0]
    hd = GDN_HEAD_DIM
    g = GDN_PREP_HEADS
    gw = g * hd
    dim = n_heads * hd
    nb = t // ROW_BLOCK
    cb0 = qkv_col0 // gw
    per = dim // gw
    halo_per_block = ROW_BLOCK // SUBLANES

    def cur(off):
        return pl.BlockSpec((ROW_BLOCK, gw), lambda rb, hg: (rb, cb0 + off * per + hg))

    def prev(off):
        return pl.BlockSpec((SUBLANES, gw),
                            lambda rb, hg: (jnp.maximum(rb * halo_per_block - 1, 0), cb0 + off * per + hg))

    def nxt(off):
        return pl.BlockSpec((SUBLANES, gw),
                            lambda rb, hg: (jnp.minimum((rb + 1) * halo_per_block, nb * halo_per_block - 1),
                                            cb0 + off * per + hg))

    def cw(off):
        return pl.BlockSpec((GDN_CONV, gw), lambda rb, hg: (0, off * per + hg))

    out_tok = lambda width: pl.BlockSpec((2, ROW_BLOCK, width), lambda rb, hg: (0, rb, hg))
    outs = pl.pallas_call(
        functools.partial(_gdn_prep_body, heads=g, n_heads=n_heads, blocks_per_seq=blocks_per_seq,
                          n_lat_blocks=n_lat_blocks),
        out_shape=(jax.ShapeDtypeStruct((2, t, dim), jnp.float32),
                   jax.ShapeDtypeStruct((2, t, dim), jnp.bfloat16),
                   jax.ShapeDtypeStruct((2, t, dim), jnp.bfloat16),
                   jax.ShapeDtypeStruct((2, t, dim), jnp.bfloat16),
                   jax.ShapeDtypeStruct((2, t, n_heads * GDN_CHUNK), jnp.bfloat16),
                   jax.ShapeDtypeStruct((2, t // SUBLANES, dim), jnp.float32)),
        grid=(nb, n_heads // g),
        in_specs=[cur(0), cur(1), cur(2), prev(0), prev(1), prev(2), nxt(0), nxt(1), nxt(2),
                  cw(0), cw(1), cw(2), pl.BlockSpec((ROW_BLOCK, 2 * LANES), lambda rb, hg: (rb, 0)),
                  pl.BlockSpec((LANES, ROW_BLOCK), lambda rb, hg: (rb, 0))],
        out_specs=(out_tok(gw), out_tok(gw), out_tok(gw), out_tok(gw), out_tok(g * GDN_CHUNK),
                   pl.BlockSpec((2, ROW_BLOCK // SUBLANES, gw), lambda rb, hg: (0, rb, hg))),
        compiler_params=_cparams("parallel", "parallel"),
        name="gdn_prepare",
    )(proj, proj, proj, proj, proj, proj, proj, proj, proj, conv_w, conv_w, conv_w, gates, gates_t)
    return outs


def _gdn_scan_body(uf, wf, qf, kf, af, gf, ub, wb, qb, kb, ab, gb, of_ref, ob_ref, s_ref, *, heads):
    t = pl.program_id(2)
    c = GDN_CHUNK
    hd = GDN_HEAD_DIM
    n_ch = ROW_BLOCK // c

    @pl.when(t == 0)
    def _():
        s_ref[...] = jnp.zeros_like(s_ref)

    pw = 2 * hd
    s_mask = ((lax.broadcasted_iota(jnp.int32, (pw, pw), 0) >> (hd.bit_length() - 1))
              == (lax.broadcasted_iota(jnp.int32, (pw, pw), 1) >> (hd.bit_length() - 1)))
    v_mask = ((lax.broadcasted_iota(jnp.int32, (2 * c, pw), 0) >> (c.bit_length() - 1))
              == (lax.broadcasted_iota(jnp.int32, (2 * c, pw), 1) >> (hd.bit_length() - 1)))
    views = ((uf, wf, qf, kf, af, gf, of_ref), (ub, wb, qb, kb, ab, gb, ob_ref))
    chains = [(d, p) for d in range(2) for p in range(heads // 2)]
    for step in range(n_ch):
        def locate(d, p):
            ch = step if d == 0 else n_ch - 1 - step
            return views[d], ch, slice(ch * c, (ch + 1) * c), slice(p * pw, (p + 1) * pw)

        state, r_all, v_all = {}, {}, {}
        for d, p in chains:
            (u_r, w_r, q_r, k_r, a_r, g_r, o_r), ch, rows, sl = locate(d, p)
            s = s_ref[d, p]
            state[d, p] = s
            s_diag = _bf(jnp.where(s_mask, jnp.concatenate([s, s], axis=0), 0.0))
            r_all[d, p] = _dot(jnp.concatenate([w_r[0, rows, sl], q_r[0, rows, sl]], axis=0), s_diag)
        for d, p in chains:
            (u_r, w_r, q_r, k_r, a_r, g_r, o_r), ch, rows, sl = locate(d, p)
            v_new = u_r[0, rows, sl] - r_all[d, p][:c]
            v_all[d, p] = _bf(jnp.where(v_mask, jnp.concatenate([v_new, v_new], axis=0), 0.0))
        for d, p in chains:
            (u_r, w_r, q_r, k_r, a_r, g_r, o_r), ch, rows, sl = locate(d, p)
            o_r[0, rows, sl] = r_all[d, p][c:] + _dot(a_r[0, rows, p * 2 * c:(p + 1) * 2 * c], v_all[d, p])
        for d, p in chains:
            (u_r, w_r, q_r, k_r, a_r, g_r, o_r), ch, rows, sl = locate(d, p)
            k_stack = jnp.concatenate([k_r[0, rows, p * pw:p * pw + hd], k_r[0, rows, p * pw + hd:(p + 1) * pw]],
                                      axis=0)
            ge = g_r[0, ch * SUBLANES:ch * SUBLANES + 1, sl]
            s_ref[d, p] = state[d, p] * ge + _dot_tn(k_stack, v_all[d, p])


def gdn_scan(u, w, qd, ke, at, ge, n_heads, n_batch, blocks_per_seq):
    t = u.shape[1]
    hd = GDN_HEAD_DIM
    g = GDN_SCAN_HEADS
    gw = g * hd
    n_lat_blocks = n_batch * blocks_per_seq

    def fwd(b, s):
        return jnp.where(s == 0, n_lat_blocks + b, b * blocks_per_seq + s - 1)

    def bwd(b, s):
        return jnp.where(s == 0, n_lat_blocks + b, b * blocks_per_seq + blocks_per_seq - s)

    def tok(d, width, blk):
        return pl.BlockSpec((1, ROW_BLOCK, width), lambda b, hg, s: (d, blk(b, s), hg))

    def gsp(d, blk):
        return pl.BlockSpec((1, ROW_BLOCK // SUBLANES, gw), lambda b, hg, s: (d, blk(b, s), hg))

    in_specs = []
    for d, blk in ((0, fwd), (1, bwd)):
        in_specs += [tok(d, gw, blk), tok(d, gw, blk), tok(d, gw, blk), tok(d, gw, blk),
                     tok(d, g * GDN_CHUNK, blk), gsp(d, blk)]
    o_f, o_b = pl.pallas_call(
        functools.partial(_gdn_scan_body, heads=g),
        out_shape=(jax.ShapeDtypeStruct((1, t, n_heads * hd), jnp.float32),) * 2,
        grid=(n_batch, n_heads // g, blocks_per_seq + 1),
        in_specs=in_specs,
        out_specs=(pl.BlockSpec((1, ROW_BLOCK, gw), lambda b, hg, s: (0, fwd(b, s), hg)),
                   pl.BlockSpec((1, ROW_BLOCK, gw), lambda b, hg, s: (0, bwd(b, s), hg))),
        scratch_shapes=[pltpu.VMEM((2, g // 2, hd, 2 * hd), jnp.float32)],
        compiler_params=_cparams("parallel", "parallel", "arbitrary"),
        name="gdn_scan",
    )(u, w, qd, ke, at, ge, u, w, qd, ke, at, ge)
    return o_f[0], o_b[0]


GATED_NORM_COLS = 1024


def _gated_norm_body(of_ref, ob_ref, z_ref, nw_ref, y_in_ref, y_ref, *, hd):
    del y_in_ref
    for h in range(GATED_NORM_COLS // hd):
        sl = slice(h * hd, (h + 1) * hd)
        o = of_ref[:, sl] + ob_ref[:, sl]
        y = o * lax.rsqrt(jnp.mean(o * o, axis=-1, keepdims=True) + NORM_EPS) * nw_ref[...]
        z = z_ref[:, sl]
        y_ref[:, sl] = (y * (z * jax.nn.sigmoid(z))).astype(y_ref.dtype)


def gated_head_norm(o_f, o_b, proj, z_col0, norm_w, hd, y_init, y_col0):
    m = y_init.shape[0]
    dim = o_f.shape[1]
    cw = GATED_NORM_COLS
    assert z_col0 % cw == 0 and dim % cw == 0 and cw % hd == 0 and y_col0 % cw == 0
    return pl.pallas_call(
        functools.partial(_gated_norm_body, hd=hd),
        out_shape=jax.ShapeDtypeStruct(y_init.shape, y_init.dtype),
        grid=(m // ROW_BLOCK, dim // cw),
        in_specs=[pl.BlockSpec((ROW_BLOCK, cw), lambda i, j: (i, j)),
                  pl.BlockSpec((ROW_BLOCK, cw), lambda i, j: (i, j)),
                  pl.BlockSpec((ROW_BLOCK, cw), lambda i, j: (i, z_col0 // cw + j)),
                  pl.BlockSpec((1, hd), lambda i, j: (0, 0)),
                  pl.BlockSpec(memory_space=pl.ANY)],
        out_specs=pl.BlockSpec((ROW_BLOCK, cw), lambda i, j: (i, y_col0 // cw + j)),
        input_output_aliases={4: 0},
        compiler_params=_cparams("parallel", "parallel"),
        name="gated_head_norm",
    )(o_f, o_b, proj, norm_w.reshape(1, hd), y_init)


def _pool_body(u_ref, w_ref, sc_ref, y_in_ref, y_ref, *, n_lat_blocks):
    del y_in_ref
    rb = pl.program_id(0)
    is_ctx = rb >= n_lat_blocks
    row_len = jnp.where(is_ctx, CTX_LEN, GRID_W)
    shift = jnp.where(is_ctx, CTX_LEN.bit_length() - 1, GRID_W.bit_length() - 1)
    n = ROW_BLOCK
    ri = lax.broadcasted_iota(jnp.int32, (n, n), 0)
    ci = lax.broadcasted_iota(jnp.int32, (n, n), 1)
    same = (ri >> shift) == (ci >> shift)
    pos_r = ri & (row_len - 1)
    pos_c = ci & (row_len - 1)
    rcol = lax.broadcasted_iota(jnp.int32, (n, 1), 0) & (row_len - 1)
    gd = POOL_GROUP_DIM
    for g, win in enumerate(POOL_WINDOWS):
        half = win // 2
        lo = jnp.maximum(pos_r - half, 0)
        hi = jnp.minimum(pos_r + half - 1, row_len - 1)
        band = _bf(jnp.where(same & (pos_c >= lo) & (pos_c <= hi), 1.0, 0.0))
        cnt = (jnp.minimum(rcol + half - 1, row_len - 1) - jnp.maximum(rcol - half, 0) + 1).astype(jnp.float32)
        u = u_ref[:, g * gd:(g + 1) * gd]
        parts = _split3(u)
        win_sum = _dot(band, parts[0]) + _dot(band, parts[1]) + _dot(band, parts[2])
        dlt = win_sum / cnt - u
        y = _dot(_bf(dlt), _bf(w_ref[g])) * sc_ref[:, g * gd:(g + 1) * gd]
        y_ref[:, g * gd:(g + 1) * gd] = y.astype(y_ref.dtype)


def pool_mix(proj, pool_w, pool_scale, n_lat_blocks, y_init):
    t = proj.shape[0]
    return pl.pallas_call(
        functools.partial(_pool_body, n_lat_blocks=n_lat_blocks),
        out_shape=jax.ShapeDtypeStruct(y_init.shape, y_init.dtype),
        grid=(t // ROW_BLOCK,),
        in_specs=[pl.BlockSpec((ROW_BLOCK, POOL_DIM), lambda i: (i, 0)),
                  pl.BlockSpec((POOL_GROUPS, POOL_GROUP_DIM, POOL_GROUP_DIM), lambda i: (0, 0, 0)),
                  pl.BlockSpec((1, POOL_DIM), lambda i: (0, 0)),
                  pl.BlockSpec(memory_space=pl.ANY)],
        out_specs=pl.BlockSpec((ROW_BLOCK, POOL_DIM), lambda i: (i, 0)),
        input_output_aliases={3: 0},
        compiler_params=_cparams("parallel"),
        name="pool_mix",
    )(proj, pool_w, pool_scale.reshape(1, POOL_DIM), y_init)


def _moe_up_body(be_ref, first_ref, nused_ref, x_ref, w1_ref, w3_ref, h_ref, w1_bf, w3_bf):
    del be_ref
    i = pl.program_id(0)

    @pl.when(first_ref[i] == 1)
    def _():
        w1_bf[...] = _bf(w1_ref[0])
        w3_bf[...] = _bf(w3_ref[0])

    @pl.when(i < nused_ref[0])
    def _():
        x = x_ref[...]
        a = _dot(x, w1_bf[...])
        b = _dot(x, w3_bf[...])
        h_ref[...] = _bf(a * jax.nn.sigmoid(a) * b)

    @pl.when(i >= nused_ref[0])
    def _():
        h_ref[...] = jnp.zeros_like(h_ref)


def _moe_down_body(be_ref, first_ref, nused_ref, h_ref, w2_ref, o_ref, w2_bf):
    del be_ref
    i = pl.program_id(0)

    @pl.when(first_ref[i] == 1)
    def _():
        w2_bf[...] = _bf(w2_ref[0])

    @pl.when(i < nused_ref[0])
    def _():
        o_ref[...] = _dot(h_ref[...], w2_bf[...]).astype(o_ref.dtype)

    @pl.when(i >= nused_ref[0])
    def _():
        o_ref[...] = jnp.zeros_like(o_ref)


def moe_expert_blocks(xbuf, block_expert, block_first, n_used, w1, w3, w2):
    p, d = xbuf.shape
    bm = MOE_BLOCK_ROWS
    nb = p // bm
    hid = w1.shape[-1]
    up_spec = pltpu.PrefetchScalarGridSpec(
        num_scalar_prefetch=3,
        grid=(nb,),
        in_specs=[pl.BlockSpec((bm, d), lambda i, be, fi, nu: (i, 0)),
                  pl.BlockSpec((1, d, hid), lambda i, be, fi, nu: (be[i], 0, 0)),
                  pl.BlockSpec((1, d, hid), lambda i, be, fi, nu: (be[i], 0, 0))],
        out_specs=pl.BlockSpec((bm, hid), lambda i, be, fi, nu: (i, 0)),
        scratch_shapes=[pltpu.VMEM((d, hid), jnp.bfloat16)] * 2,
    )
    hbuf = pl.pallas_call(
        _moe_up_body,
        out_shape=jax.ShapeDtypeStruct((p, hid), jnp.bfloat16),
        grid_spec=up_spec,
        compiler_params=pltpu.CompilerParams(dimension_semantics=("arbitrary",), vmem_limit_bytes=MOE_UP_VMEM_LIMIT),
        name="moe_up",
    )(block_expert, block_first, n_used, xbuf, w1, w3)
    down_spec = pltpu.PrefetchScalarGridSpec(
        num_scalar_prefetch=3,
        grid=(nb,),
        in_specs=[pl.BlockSpec((bm, hid), lambda i, be, fi, nu: (i, 0)),
                  pl.BlockSpec((1, hid, d), lambda i, be, fi, nu: (be[i], 0, 0))],
        out_specs=pl.BlockSpec((bm, d), lambda i, be, fi, nu: (i, 0)),
        scratch_shapes=[pltpu.VMEM((hid, d), jnp.bfloat16)],
    )
    return pl.pallas_call(
        _moe_down_body,
        out_shape=jax.ShapeDtypeStruct((p, d), jnp.float32),
        grid_spec=down_spec,
        compiler_params=_cparams("arbitrary"),
        name="moe_down",
    )(block_expert, block_first, n_used, hbuf, w2)


def _moe_route_body(h_ref, w_ref, b_ref, ids_ref, wgt_ref, cnt_ref, run_ref):
    i = pl.program_id(0)

    @pl.when(i == 0)
    def _():
        run_ref[...] = jnp.zeros_like(run_ref)

    n = h_ref.shape[0]
    ng, ne = MOE_GROUPS, MOE_EXPERTS_PER_GROUP
    h = h_ref[...]
    logits = _dot(h, w_ref[0]) + _dot(h, w_ref[1]) + _dot(h, w_ref[2]) + b_ref[...]
    lane = lax.broadcasted_iota(jnp.int32, logits.shape, 1)
    neg = jnp.float32(-jnp.inf)

    lane_f = lane.astype(jnp.float32)

    def first_argmax(vals):
        top = jnp.max(vals, axis=1, keepdims=True)
        return top, jnp.min(jnp.where(vals == top, lane_f, float(LANES)), axis=1, keepdims=True).astype(jnp.int32)

    gl = jnp.where(lane < ng, logits, neg)
    g_top, group = first_argmax(gl)
    p_group = 1.0 / jnp.sum(jnp.exp(gl - g_top), axis=1, keepdims=True)
    lo = ng + group * ne
    sel = jnp.where((lane >= lo) & (lane < lo + ne), logits, neg)
    v1, i1 = first_argmax(sel)
    v2, i2 = first_argmax(jnp.where(lane == i1, neg, sel))
    e21 = jnp.exp(v2 - v1)
    w1 = p_group / (1.0 + e21)
    w2 = p_group * e21 / (1.0 + e21)
    e1 = i1 - ng
    e2 = i2 - ng
    oh1 = jnp.where(lane == e1, 1.0, 0.0)
    oh2 = jnp.where(lane == e2, 1.0, 0.0)
    both = oh1 + oh2
    ri = lax.broadcasted_iota(jnp.int32, (n, n), 0)
    ci = lax.broadcasted_iota(jnp.int32, (n, n), 1)
    before = _dot(_bf(jnp.where(ci < ri, 1.0, 0.0)), _bf(both)) + run_ref[0:1, :]
    r1 = jnp.sum(oh1 * before, axis=1, keepdims=True)
    r2 = jnp.sum(oh2 * before, axis=1, keepdims=True)
    run_ref[...] = run_ref[...] + jnp.sum(both, axis=0, keepdims=True)
    ids_ref[...] = jnp.where(lane == 0, e1, jnp.where(lane == 1, e2, jnp.where(
        lane == 2, r1.astype(jnp.int32), jnp.where(lane == 3, r2.astype(jnp.int32), 0))))
    wgt_ref[...] = jnp.where(lane == 0, w1, jnp.where(lane == 1, w2, 0.0))
    cnt_ref[...] = run_ref[...].astype(jnp.int32)


def moe_route(h, group_w, group_b, expert_w, expert_b):
    t, d = h.shape
    w = jnp.concatenate([group_w, expert_w.transpose(1, 0, 2).reshape(d, MOE_EXPERTS)], axis=1).astype(jnp.float32)
    w = jnp.stack(_split3(_pad_cols(w)))
    b = _pad_cols(jnp.concatenate([group_b, expert_b.reshape(-1)])[None, :].astype(jnp.float32))
    tm = ROW_BLOCK
    return pl.pallas_call(
        _moe_route_body,
        out_shape=(jax.ShapeDtypeStruct((t, LANES), jnp.int32), jax.ShapeDtypeStruct((t, LANES), jnp.float32),
                   jax.ShapeDtypeStruct((SUBLANES, LANES), jnp.int32)),
        grid=(t // tm,),
        in_specs=[pl.BlockSpec((tm, d), lambda i: (i, 0)), pl.BlockSpec((3, d, LANES), lambda i: (0, 0, 0)),
                  pl.BlockSpec((1, LANES), lambda i: (0, 0))],
        out_specs=(pl.BlockSpec((tm, LANES), lambda i: (i, 0)), pl.BlockSpec((tm, LANES), lambda i: (i, 0)),
                   pl.BlockSpec((SUBLANES, LANES), lambda i: (0, 0))),
        scratch_shapes=[pltpu.VMEM((SUBLANES, LANES), jnp.float32)],
        compiler_params=_cparams("arbitrary"),
        name="moe_route",
    )(h, w, b)


def hier_moe(h, group_w, group_b, expert_w, expert_b, w1, w3, w2, expert0):
    t, d = h.shape
    ids, weight, counts = moe_route(h, group_w, group_b, expert_w, expert_b)
    expert = ids[:, :MOE_TOP_K].T.reshape(-1)
    rank = ids[:, MOE_TOP_K:2 * MOE_TOP_K].T.reshape(-1)
    counts = counts[0, :MOE_EXPERTS]
    tk = t * MOE_TOP_K
    bm = MOE_BLOCK_ROWS
    padded = (counts + bm - 1) // bm * bm
    e_idx = jnp.arange(MOE_EXPERTS)
    pad_end = jnp.sum(jnp.where(e_idx[None, :] <= e_idx[:, None], padded[None, :], 0), axis=1)
    pad_start = pad_end - padded
    dest = pad_start[expert] + rank
    n_blocks = -(-tk // bm) + MOE_EXPERTS
    token = jnp.tile(jnp.arange(t, dtype=jnp.int32), MOE_TOP_K)
    src = jnp.zeros((n_blocks * bm,), jnp.int32).at[dest].set(token)
    block_start = jnp.arange(n_blocks, dtype=jnp.int32) * bm
    block_expert = jnp.minimum(jnp.sum((pad_end[None, :] <= block_start[:, None]).astype(jnp.int32), axis=1),
                               MOE_EXPERTS - 1)
    block_first = jnp.concatenate([jnp.ones((1,), jnp.int32),
                                   (block_expert[1:] != block_expert[:-1]).astype(jnp.int32)])
    n_used = (pad_end[-1:] // bm).astype(jnp.int32)
    xbuf = h[src]
    ybuf = moe_expert_blocks(xbuf, block_expert + expert0, block_first, n_used, w1, w3, w2)
    return ybuf, dest.astype(jnp.int32), weight


GLA_SUB = 32
GLA_STEP = 64


def _gla_prep_body(q_ref, k_ref, v_ref, lr_ref, wg_ref, gb_ref, qd_ref, ke_ref, at_ref, ge_ref, v16_ref):
    n = ROW_BLOCK
    c = GLA_STEP
    sb = GLA_SUB
    n_sb = n // sb
    q = q_ref[...] * (GLA_HEAD_K ** -0.5)
    k = k_ref[...]
    v16_ref[...] = _bf(v_ref[...])
    lr = lr_ref[...]
    ri = lax.broadcasted_iota(jnp.int32, (n, n), 0)
    ci = lax.broadcasted_iota(jnp.int32, (n, n), 1)
    cs = c.bit_length() - 1
    ss = sb.bit_length() - 1
    same_c = (ri >> cs) == (ci >> cs)
    same_s = (ri >> ss) == (ci >> ss)
    dirs = range(2)
    xs = [jnp.dot(lr, wg_ref[d], preferred_element_type=jnp.float32, precision=lax.Precision.HIGHEST) + gb_ref[d]
          for d in dirs]
    gks = [_split3((jnp.minimum(x, 0.0) - jnp.log1p(jnp.exp(-jnp.abs(x)))) * (1.0 / GLA_GATE_TEMP)) for x in xs]
    runs = [_bf(jnp.where(same_c & ((ci <= ri) if d == 0 else (ci >= ri)), 1.0, 0.0)) for d in dirs]
    gs = [_dot(runs[d], gks[d][0]) + _dot(runs[d], gks[d][1]) + _dot(runs[d], gks[d][2]) for d in dirs]
    locs, tots, totc = [], [], []
    for d in dirs:
        g = gs[d]
        off, tot_s, tot_c = [], [], []
        for b in range(n_sb):
            lo = b * sb
            inner = (b % 2 == 1) if d == 0 else (b % 2 == 0)
            edge = g[lo + sb - 1:lo + sb] if d == 0 else g[lo:lo + 1]
            if inner:
                prev = g[lo - 1:lo] if d == 0 else g[lo + sb:lo + sb + 1]
                off.append(jnp.broadcast_to(prev, (sb, prev.shape[1])))
                tot_s.append(jnp.broadcast_to(edge - prev, (sb, prev.shape[1])))
            else:
                off.append(jnp.zeros((sb, g.shape[1]), jnp.float32))
                tot_s.append(jnp.broadcast_to(edge, (sb, edge.shape[1])))
        for j in range(n // c):
            edge = g[j * c + c - 1:j * c + c] if d == 0 else g[j * c:j * c + 1]
            tot_c.append(jnp.broadcast_to(edge, (c, edge.shape[1])))
        locs.append(g - jnp.concatenate(off, axis=0))
        tots.append(jnp.concatenate(tot_s, axis=0))
        totc.append(jnp.concatenate(tot_c, axis=0))
    qd_loc = [_bf(q * jnp.exp(locs[d])) for d in dirs]
    m_diag = [_dot_nt(qd_loc[d], _bf(k * jnp.exp(-locs[d]))) for d in dirs]
    m_cross = [_dot_nt(qd_loc[d], _bf(k * jnp.exp(tots[d] - locs[d]))) for d in dirs]
    for d in dirs:
        tri = (ci <= ri) if d == 0 else (ci >= ri)
        nxt = ((ri >> ss) == (ci >> ss) + 1) if d == 0 else ((ri >> ss) + 1 == (ci >> ss))
        attn = jnp.where(same_s & tri, m_diag[d], 0.0) + jnp.where(same_c & nxt, m_cross[d], 0.0)
        qd_ref[d] = _bf(q * jnp.exp(gs[d]))
        ke_ref[d] = _bf(k * jnp.exp(totc[d] - gs[d]))
        for j in range(n // c):
            at_ref[d, 0, j * c:(j + 1) * c, :] = _bf(attn[j * c:(j + 1) * c, j * c:(j + 1) * c])
            ge_ref[d, j * SUBLANES:(j + 1) * SUBLANES, :] = jnp.exp(totc[d][j * c:j * c + SUBLANES])


def gla_prepare(proj, gate_up, gate_b):
    t = proj.shape[0]
    assert CD_MAIN % LANES == 0
    hk, hv = GLA_HEAD_K, GLA_HEAD_V
    wg = jnp.zeros((2, LANES, GLA_K_DIM), jnp.float32)
    for d in range(2):
        wg = wg.at[d, d * GLA_GATE_RANK:(d + 1) * GLA_GATE_RANK].set(gate_up[d].astype(jnp.float32))
    gb = gate_b.astype(jnp.float32).reshape(2, 1, GLA_K_DIM)
    q0, k0, v0 = CD_SPLITS[0] // hk, CD_SPLITS[1] // hk, CD_SPLITS[2] // hv
    assert CD_SPLITS[0] % hk == 0 and CD_SPLITS[1] % hk == 0 and CD_SPLITS[2] % hv == 0
    return pl.pallas_call(
        _gla_prep_body,
        out_shape=(jax.ShapeDtypeStruct((2, t, GLA_K_DIM), jnp.bfloat16),
                   jax.ShapeDtypeStruct((2, t, GLA_K_DIM), jnp.bfloat16),
                   jax.ShapeDtypeStruct((2, GLA_HEADS, t, GLA_STEP), jnp.bfloat16),
                   jax.ShapeDtypeStruct((2, t // SUBLANES, GLA_K_DIM), jnp.float32),
                   jax.ShapeDtypeStruct((t, GLA_V_DIM), jnp.bfloat16)),
        grid=(t // ROW_BLOCK, GLA_HEADS),
        in_specs=[pl.BlockSpec((ROW_BLOCK, hk), lambda rb, h: (rb, q0 + h)),
                  pl.BlockSpec((ROW_BLOCK, hk), lambda rb, h: (rb, k0 + h)),
                  pl.BlockSpec((ROW_BLOCK, hv), lambda rb, h: (rb, v0 + h)),
                  pl.BlockSpec((ROW_BLOCK, LANES), lambda rb, h: (rb, CD_MAIN // LANES)),
                  pl.BlockSpec((2, LANES, hk), lambda rb, h: (0, 0, h)),
                  pl.BlockSpec((2, 1, hk), lambda rb, h: (0, 0, h))],
        out_specs=(pl.BlockSpec((2, ROW_BLOCK, hk), lambda rb, h: (0, rb, h)),
                   pl.BlockSpec((2, ROW_BLOCK, hk), lambda rb, h: (0, rb, h)),
                   pl.BlockSpec((2, 1, ROW_BLOCK, GLA_STEP), lambda rb, h: (0, h, rb, 0)),
                   pl.BlockSpec((2, ROW_BLOCK // SUBLANES, hk), lambda rb, h: (0, rb, h)),
                   pl.BlockSpec((ROW_BLOCK, hv), lambda rb, h: (rb, h))),
        compiler_params=_cparams("parallel", "parallel"),
        name="gla_prepare",
    )(proj, proj, proj, proj, wg, gb)


def _gla_scan_body(qf, kf, af, gf, vf, qb, kb, ab, gb, vb, of_ref, ob_ref, s_ref):
    t = pl.program_id(2)
    c = GLA_STEP
    n_ch = ROW_BLOCK // c

    @pl.when(t == 0)
    def _():
        s_ref[...] = jnp.zeros_like(s_ref)

    views = ((qf, kf, af, gf, vf, of_ref), (qb, kb, ab, gb, vb, ob_ref))
    for step in range(n_ch):
        chunk = [step, n_ch - 1 - step]
        rows = [slice(ch * c, (ch + 1) * c) for ch in chunk]
        st = [s_ref[d] for d in range(2)]
        inter = [_dot_nt(views[d][0][0, rows[d], :], _bf(st[d])) for d in range(2)]
        intra = [_dot(views[d][2][0, 0, rows[d], :], views[d][4][rows[d], :]) for d in range(2)]
        upd = [_dot_tn(views[d][4][rows[d], :], views[d][1][0, rows[d], :]) for d in range(2)]
        for d in range(2):
            views[d][5][rows[d], :] = inter[d] + intra[d]
            ge = views[d][3][0, chunk[d] * SUBLANES:chunk[d] * SUBLANES + 1, :]
            s_ref[d] = st[d] * ge + upd[d]


def gla_scan(qd, ke, at, ge, v16, n_batch, blocks_per_seq):
    hk, hv = GLA_HEAD_K, GLA_HEAD_V
    t = v16.shape[0]
    n_lat_blocks = n_batch * blocks_per_seq

    def fwd(b, s):
        return jnp.where(s == 0, n_lat_blocks + b, b * blocks_per_seq + s - 1)

    def bwd(b, s):
        return jnp.where(s == 0, n_lat_blocks + b, b * blocks_per_seq + blocks_per_seq - s)

    def out_blk(blk):
        return lambda b, h, s: (blk(b, s), h)

    def key_spec(d, blk, rows):
        return pl.BlockSpec((1, rows, hk), lambda b, h, s: (d, blk(b, s), h))

    def attn_spec(d, blk):
        return pl.BlockSpec((1, 1, ROW_BLOCK, GLA_STEP), lambda b, h, s: (d, h, blk(b, s), 0))

    def val_spec(blk):
        return pl.BlockSpec((ROW_BLOCK, hv), lambda b, h, s: (blk(b, s), h))

    in_specs = []
    for d, blk in ((0, fwd), (1, bwd)):
        in_specs += [key_spec(d, blk, ROW_BLOCK), key_spec(d, blk, ROW_BLOCK), attn_spec(d, blk),
                     key_spec(d, blk, ROW_BLOCK // SUBLANES), val_spec(blk)]
    o_f, o_b = pl.pallas_call(
        _gla_scan_body,
        out_shape=(jax.ShapeDtypeStruct((t, GLA_V_DIM), jnp.float32),) * 2,
        grid=(n_batch, GLA_HEADS, blocks_per_seq + 1),
        in_specs=in_specs,
        out_specs=(pl.BlockSpec((ROW_BLOCK, hv), out_blk(fwd)), pl.BlockSpec((ROW_BLOCK, hv), out_blk(bwd))),
        scratch_shapes=[pltpu.VMEM((2, hv, hk), jnp.float32)],
        compiler_params=_cparams("parallel", "parallel", "arbitrary"),
        name="gla_scan",
    )(qd, ke, at, ge, v16, qd, ke, at, ge, v16)
    return o_f, o_b


def _fourier_in_body(x_ref, w_ref, o_ref):
    y = _dot(_bf(x_ref[...]), w_ref[0])
    half = y.shape[1] // 2
    o_ref[0] = _bf(y[:, :half])
    o_ref[1] = _bf(y[:, half:])


def dft_tables(n, nc):
    r = 64
    assert n % r == 0
    kk = jnp.arange(n, dtype=jnp.int32)[None, :]
    j1 = jnp.arange(n // r, dtype=jnp.int32)[:, None]
    j0 = jnp.arange(r, dtype=jnp.int32)[:, None]
    a = ((j1 * kk) % (n // r)).astype(jnp.float32) * (2.0 * jnp.pi / (n // r))
    b = ((j0 * kk) % n).astype(jnp.float32) * (2.0 * jnp.pi / n)
    ca, sa, cb, sb = jnp.cos(a)[:, None, :], jnp.sin(a)[:, None, :], jnp.cos(b)[None, :, :], jnp.sin(b)[None, :, :]
    scale = n ** -0.5
    cos_n = ((ca * cb - sa * sb) * scale).reshape(n, n)
    sin_n = ((sa * cb + ca * sb) * scale).reshape(n, n)
    table = jnp.concatenate([cos_n, -sin_n], axis=1).astype(jnp.bfloat16)
    cc = jnp.arange(nc, dtype=jnp.int32)
    ang = ((cc[:, None] * cc[None, :]) % nc).astype(jnp.float32) * (2.0 * jnp.pi / nc)
    return table, jnp.cos(ang) * nc ** -0.5, jnp.sin(ang) * nc ** -0.5


def fourier_mix(proj, fourier_w, n_batch, n_lat, y_init):
    gd = FOURIER_GROUP_DIM
    ng = FOURIER_GROUPS
    table, cos_c, sin_c = dft_tables(n_lat, gd)
    w_all = fourier_w.astype(jnp.float32).transpose(1, 0, 2).reshape(gd, ng * gd)
    folded = dense_matmul([(jnp.concatenate([cos_c, sin_c], axis=0), w_all)], tm=2 * gd, tn=512)
    w2 = jnp.concatenate([folded[:gd].reshape(gd, ng, gd), folded[gd:].reshape(gd, ng, gd)], axis=2)
    w2 = w2.transpose(1, 0, 2).astype(jnp.bfloat16)
    tm = min(DENSE_TM, n_lat)
    tiles = n_lat // tm
    z = pl.pallas_call(
        _fourier_in_body,
        out_shape=jax.ShapeDtypeStruct((2, n_lat, n_batch * ng * gd), jnp.bfloat16),
        grid=(n_batch, ng, tiles),
        in_specs=[pl.BlockSpec((tm, gd), lambda b, g, i: (b * tiles + i, g)),
                  pl.BlockSpec((1, gd, 2 * gd), lambda b, g, i: (g, 0, 0))],
        out_specs=pl.BlockSpec((2, tm, gd), lambda b, g, i: (0, i, b * ng + g)),
        compiler_params=_cparams("parallel", "parallel", "parallel"),
        name="fourier_in",
    )(proj, w2)
    z = z.reshape(2 * n_lat, n_batch * ng * gd)
    tn = 512
    per_b = ng * gd // tn
    return dense_matmul([(table, z)], out_dtype=jnp.bfloat16, tm=512, tn=tn,
                        out_init=y_init,
                        out_index=lambda i, j: ((j // per_b) * (n_lat // 512) + i, j % per_b))


def modulation(c, c_ctx, mod_w, mod_b, layer):
    b, d = c.shape
    rows = jnp.concatenate([c, c_ctx[None, :], jnp.zeros((SUBLANES - b - 1, d), c.dtype)], axis=0)
    out = dense_matmul([(jax.nn.silu(rows), mod_w.reshape(-1, mod_w.shape[-1]))], tm=SUBLANES, tn=512,
                       w_row_block=layer) + mod_b[layer]
    return tuple(m[:, None, :] for m in jnp.split(out, N_MOD, axis=-1))


def _combine_body(dest_ref, x_ref, ybuf_ref, w_ref, gate_ref, nw_ref, o_ref, rows_ref, sem, *, final_norm):
    tm = x_ref.shape[0]

    def row_copy(r, k, src_row):
        return pltpu.make_async_copy(ybuf_ref.at[pl.ds(src_row, 1), :], rows_ref.at[k, pl.ds(r, 1), :], sem.at[k])

    def issue(r, carry):
        for k in range(MOE_TOP_K):
            row_copy(r, k, dest_ref[0, 0, k * tm + r]).start()
        return carry

    def drain(r, carry):
        for k in range(MOE_TOP_K):
            row_copy(r, k, 0).wait()
        return carry

    lax.fori_loop(0, tm, issue, 0, unroll=8)
    lax.fori_loop(0, tm, drain, 0, unroll=8)
    y = w_ref[:, 0:1] * rows_ref[0] + w_ref[:, 1:2] * rows_ref[1]
    x = x_ref[...] + gate_ref[0] * y
    if final_norm:
        x = x * lax.rsqrt(jnp.mean(x * x, axis=-1, keepdims=True) + NORM_EPS) * nw_ref[...]
    o_ref[...] = x


def moe_combine(x, ybuf, dest, weight, gates, gate_index, final_norm_w=None):
    m, d = dest.shape[0] // MOE_TOP_K, x.shape[1]
    tm = ROW_BLOCK
    dest_tiles = dest.reshape(MOE_TOP_K, m // tm, tm).transpose(1, 0, 2).reshape(m // tm, 1, MOE_TOP_K * tm)
    nw = jnp.ones((1, d), jnp.float32) if final_norm_w is None else final_norm_w.reshape(1, d).astype(jnp.float32)
    return pl.pallas_call(
        functools.partial(_combine_body, final_norm=final_norm_w is not None),
        out_shape=jax.ShapeDtypeStruct((m, d), jnp.float32),
        grid=(m // tm,),
        in_specs=[pl.BlockSpec((1, 1, MOE_TOP_K * tm), lambda i: (i, 0, 0), memory_space=pltpu.SMEM),
                  pl.BlockSpec((tm, d), lambda i: (i, 0)),
                  pl.BlockSpec(memory_space=pl.ANY),
                  pl.BlockSpec((tm, LANES), lambda i: (i, 0)),
                  pl.BlockSpec((1, 1, d), lambda i: (gate_index(i), 0, 0)),
                  pl.BlockSpec((1, d), lambda i: (0, 0))],
        out_specs=pl.BlockSpec((tm, d), lambda i: (i, 0)),
        scratch_shapes=[pltpu.VMEM((MOE_TOP_K, tm, d), jnp.float32), pltpu.SemaphoreType.DMA((MOE_TOP_K,))],
        compiler_params=_cparams("parallel"),
        name="moe_combine",
    )(dest_tiles, x, ybuf, weight, gates, nw)


def kernel(x, c, ctx, c_ctx, mod_w, mod_b, norm1_w, norm2_w, ab_w_in, pool_w, pool_scale, gdn_conv_w,
           gdn_a_log, gdn_dt_bias, gdn_norm_w, ab_w_out, cd_w_in, fourier_w, gla_gate_up, gla_gate_b,
           gla_norm_w, cd_w_out, moe_group_w, moe_group_b, moe_expert_w, moe_expert_b, moe_w1, moe_w3,
           moe_w2, final_norm_w):
    assert DEPTH == 2
    n_batch, n_lat, d = x.shape
    n_ctx = ctx.shape[1]
    assert n_ctx == ROW_BLOCK and n_lat % DENSE_TM == 0 and (n_batch * n_ctx) % DENSE_TM == 0
    t_lat = n_batch * n_lat
    blocks_per_seq = n_lat // ROW_BLOCK
    n_lat_blocks = n_batch * blocks_per_seq
    xs = jnp.concatenate([x.reshape(t_lat, d), ctx.reshape(n_batch * n_ctx, d)], axis=0)

    def mod_row(rows_per_tile):
        tiles_per_seq = n_lat // rows_per_tile
        return lambda i: jnp.minimum(i // tiles_per_seq, n_batch)

    bf = jnp.bfloat16
    w1_all, w3_all, w2_all = (w.reshape((-1,) + w.shape[2:]) for w in (moe_w1, moe_w3, moe_w2))
    sh1, sc1, g1, sh2, sc2, g2 = modulation(c, c_ctx, mod_w, mod_b, 0)
    h = norm_modulate(xs, norm1_w[0], sh1, sc1, mod_row(512))
    proj = dense_matmul([(h, _pad_cols(ab_w_in[0].astype(bf), DENSE_TN))])
    gates, gates_t = gdn_gates(proj, AB_MAIN, gdn_a_log[0], gdn_dt_bias[0], GDN_HEADS)
    u, w, qd, ke, at, ge = gdn_prepare(proj, gates, gates_t, gdn_conv_w[0], GDN_HEADS, POOL_DIM, blocks_per_seq,
                                       n_lat_blocks)
    o_f, o_b = gdn_scan(u, w, qd, ke, at, ge, GDN_HEADS, n_batch, blocks_per_seq)
    y = pool_mix(proj, pool_w[0], pool_scale[0], n_lat_blocks, jnp.zeros((xs.shape[0], d), bf))
    y = gated_head_norm(o_f, o_b, proj, POOL_DIM + 3 * GDN_DIM, gdn_norm_w[0], GDN_HEAD_DIM, y, POOL_DIM)
    xs = dense_matmul([(y, ab_w_out[0].astype(bf))], residual=xs, gates=g1, gate_index=mod_row(DENSE_TM))
    f = norm_modulate(xs, norm2_w[0], sh2, sc2, mod_row(512))
    ybuf, dest, wgt = hier_moe(f, moe_group_w[0], moe_group_b[0], moe_expert_w[0], moe_expert_b[0], w1_all, w3_all,
                               w2_all, 0)
    xs = moe_combine(xs, ybuf, dest, wgt, g2, mod_row(ROW_BLOCK))

    sh1, sc1, g1, sh2, sc2, g2 = modulation(c, c_ctx, mod_w, mod_b, 1)
    h = norm_modulate(xs, norm1_w[1], sh1, sc1, mod_row(512))
    proj = dense_matmul([(h, _pad_cols(cd_w_in[0].astype(bf), DENSE_TN))])
    qd, ke, at, ge, v16 = gla_prepare(proj, gla_gate_up[0], gla_gate_b[0])
    o_f, o_b = gla_scan(qd, ke, at, ge, v16, n_batch, blocks_per_seq)
    y = fourier_mix(proj, fourier_w[0], n_batch, n_lat, jnp.zeros((t_lat, d), bf))
    y = gated_head_norm(o_f, o_b, proj, CD_SPLITS[3], gla_norm_w[0], GLA_HEAD_V, y, FOURIER_DIM)
    xl = dense_matmul([(y, cd_w_out[0].astype(bf))], residual=xs, gates=g1, gate_index=mod_row(DENSE_TM))
    f = norm_modulate(xl, norm2_w[1], sh2, sc2, mod_row(512))
    ybuf, dest, wgt = hier_moe(f, moe_group_w[1], moe_group_b[1], moe_expert_w[1], moe_expert_b[1], w1_all, w3_all,
                               w2_all, MOE_EXPERTS)
    out = moe_combine(xl, ybuf, dest, wgt, g2, mod_row(ROW_BLOCK), final_norm_w=final_norm_w)
    return out.reshape(n_batch, n_lat, d)
```

```python
import functools

import jax
import jax.numpy as jnp
from jax import lax
from jax.experimental import pallas as pl
from jax.experimental.pallas import tpu as pltpu

D_MODEL = 4096
DEPTH = 2
CTX_LEN = 256
GRID_W = 64
N_MOD = 6
NORM_EPS = 1e-6

POOL_WINDOWS = (2, 4, 8, 16)
POOL_GROUPS = 4
POOL_GROUP_DIM = D_MODEL // 16
POOL_DIM = POOL_GROUPS * POOL_GROUP_DIM

GDN_HEAD_DIM = 128
GDN_DIM = D_MODEL - POOL_DIM
GDN_HEADS = GDN_DIM // GDN_HEAD_DIM
GDN_CONV = 5
GDN_CHUNK = 64

FOURIER_GROUPS = 4
FOURIER_GROUP_DIM = D_MODEL // 16
FOURIER_DIM = FOURIER_GROUPS * FOURIER_GROUP_DIM

GLA_HEADS = 6
GLA_V_DIM = D_MODEL - FOURIER_DIM
GLA_K_DIM = GLA_V_DIM // 2
GLA_HEAD_K = GLA_K_DIM // GLA_HEADS
GLA_HEAD_V = GLA_V_DIM // GLA_HEADS
GLA_GATE_RANK = 16
GLA_GATE_TEMP = 16.0
GLA_CHUNK = 32

MOE_GROUPS = 4
MOE_EXPERTS_PER_GROUP = 8
MOE_EXPERTS = MOE_GROUPS * MOE_EXPERTS_PER_GROUP
MOE_TOP_K = 2
MOE_HIDDEN = D_MODEL // 8

AB_MAIN = POOL_DIM + 4 * GDN_DIM
CD_SPLITS = (FOURIER_DIM, FOURIER_DIM + GLA_K_DIM, FOURIER_DIM + 2 * GLA_K_DIM,
             FOURIER_DIM + 2 * GLA_K_DIM + GLA_V_DIM, FOURIER_DIM + 2 * GLA_K_DIM + 2 * GLA_V_DIM)
CD_MAIN = FOURIER_DIM + 2 * GLA_K_DIM + 2 * GLA_V_DIM

LANES = 128
SUBLANES = 8
VMEM_LIMIT = 48 * 1024 * 1024
MOE_UP_VMEM_LIMIT = 56 * 1024 * 1024
MOE_BLOCK_ROWS = 256
ROW_BLOCK = 256
DENSE_TM = 1024
DENSE_TN = 512
GDN_PREP_HEADS = 4
GDN_SCAN_HEADS = 8


def _cparams(*sem):
    return pltpu.CompilerParams(dimension_semantics=sem, vmem_limit_bytes=VMEM_LIMIT)


def _dot(a, b):
    return jnp.dot(a, b, preferred_element_type=jnp.float32)


def _dot_nt(a, b):
    return lax.dot_general(a, b, (((1,), (1,)), ((), ())), preferred_element_type=jnp.float32)


def _dot_tn(a, b):
    return lax.dot_general(a, b, (((0,), (0,)), ((), ())), preferred_element_type=jnp.float32)


def _bf(x):
    return x.astype(jnp.bfloat16)


def _split3(x):
    hi = _bf(x)
    r1 = x - hi.astype(jnp.float32)
    mid = _bf(r1)
    lo = _bf(r1 - mid.astype(jnp.float32))
    return hi, mid, lo


def _mm_body(*refs, n_pairs, has_res):
    o_ref = refs[-1]
    acc = _dot(refs[0][...], refs[1][...])
    for p in range(1, n_pairs):
        acc = acc + _dot(refs[2 * p][...], refs[2 * p + 1][...])
    if has_res:
        res_ref, gate_ref = refs[2 * n_pairs], refs[2 * n_pairs + 1]
        acc = res_ref[...] + gate_ref[0] * acc
    o_ref[...] = acc.astype(o_ref.dtype)


def dense_matmul(pairs, out_dtype=jnp.float32, tm=DENSE_TM, tn=512, m=None, n=None, w_row_block=0, residual=None,
                 gates=None, gate_index=None, out_shape=None, out_index=None):
    m = pairs[0][0].shape[0] if m is None else m
    n = pairs[0][1].shape[1] if n is None else n
    tm = min(tm, m)
    tn = min(tn, n)
    assert m % tm == 0 and n % tn == 0, (m, n, tm, tn)
    in_specs, args = [], []
    for x, w in pairs:
        k = x.shape[1]
        assert w.shape[0] % k == 0
        in_specs += [pl.BlockSpec((tm, k), lambda i, j: (i, 0)),
                     pl.BlockSpec((k, tn), lambda i, j: (w_row_block, j))]
        args += [x, w]
    if residual is not None:
        in_specs += [pl.BlockSpec((tm, tn), lambda i, j: (i, j)),
                     pl.BlockSpec((1, 1, tn), lambda i, j: (gate_index(i), 0, j))]
        args += [residual, gates]
    return pl.pallas_call(
        functools.partial(_mm_body, n_pairs=len(pairs), has_res=residual is not None),
        out_shape=jax.ShapeDtypeStruct((m, n) if out_shape is None else out_shape, out_dtype),
        grid=(m // tm, n // tn),
        in_specs=in_specs,
        out_specs=pl.BlockSpec((tm, tn), (lambda i, j: (i, j)) if out_index is None else out_index),
        compiler_params=_cparams("parallel", "parallel"),
        name="dense_matmul",
    )(*args)


def _pad_cols(w, mult=LANES):
    pad = (-w.shape[1]) % mult
    return jnp.pad(w, ((0, 0), (0, pad))) if pad else w


def _norm_mod_body(x_ref, nw_ref, shift_ref, scale_ref, o_ref):
    x = x_ref[...]
    y = x * lax.rsqrt(jnp.mean(x * x, axis=-1, keepdims=True) + NORM_EPS) * nw_ref[...]
    o_ref[...] = (y * (1.0 + scale_ref[0]) + shift_ref[0]).astype(o_ref.dtype)


def norm_modulate(x, norm_w, shift, scale, row_index, m=None, tm=512, out_dtype=jnp.bfloat16):
    m = x.shape[0] if m is None else m
    d = x.shape[1]
    vec = pl.BlockSpec((1, 1, d), lambda i: (row_index(i), 0, 0))
    return pl.pallas_call(
        _norm_mod_body,
        out_shape=jax.ShapeDtypeStruct((m, d), out_dtype),
        grid=(m // tm,),
        in_specs=[pl.BlockSpec((tm, d), lambda i: (i, 0)), pl.BlockSpec((1, d), lambda i: (0, 0)), vec, vec],
        out_specs=pl.BlockSpec((tm, d), lambda i: (i, 0)),
        compiler_params=_cparams("parallel"),
        name="norm_modulate",
    )(x, norm_w.reshape(1, d), shift, scale)


def _gdn_gates_body(tail_ref, par_ref, o_ref, ot_ref, *, n_heads, chunk):
    t = tail_ref[...]
    rows = t.shape[0]
    lane = lax.broadcasted_iota(jnp.int32, t.shape, 1)
    a_row = par_ref[0:1, :]
    dtb_row = par_ref[1:2, :]
    beta = jax.nn.sigmoid(t)
    z = t + dtb_row
    g = a_row * (jnp.maximum(z, 0.0) + jnp.log1p(jnp.exp(-jnp.abs(z))))
    ri = lax.broadcasted_iota(jnp.int32, (rows, rows), 0)
    ci = lax.broadcasted_iota(jnp.int32, (rows, rows), 1)
    shift = chunk.bit_length() - 1
    same = (ri >> shift) == (ci >> shift)
    pre = _bf(jnp.where(same & (ci <= ri), 1.0, 0.0))
    suf = _bf(jnp.where(same & (ci >= ri), 1.0, 0.0))
    ones = _bf(jnp.where(same, 1.0, 0.0))
    parts = _split3(g)
    cf = _dot(pre, parts[0]) + _dot(pre, parts[1]) + _dot(pre, parts[2])
    cb = _dot(suf, parts[0]) + _dot(suf, parts[1]) + _dot(suf, parts[2])
    tot = _dot(ones, parts[0]) + _dot(ones, parts[1]) + _dot(ones, parts[2])
    main = jnp.where(lane < 2 * n_heads, beta,
                     jnp.where(lane < 3 * n_heads, cf, jnp.where(lane < 4 * n_heads, cb, 0.0)))
    o_ref[:, :LANES] = main
    o_ref[:, LANES:] = tot
    ot_ref[...] = main.T


def gdn_gates(proj, tail_col0, a_log, dt_bias, n_heads):
    t = proj.shape[0]
    assert tail_col0 % LANES == 0
    par = jnp.zeros((SUBLANES, LANES), jnp.float32)
    par = par.at[0, 2 * n_heads:4 * n_heads].set(-jnp.exp(a_log.astype(jnp.float32)).reshape(-1))
    par = par.at[1, 2 * n_heads:4 * n_heads].set(dt_bias.astype(jnp.float32).reshape(-1))
    return pl.pallas_call(
        functools.partial(_gdn_gates_body, n_heads=n_heads, chunk=GDN_CHUNK),
        out_shape=(jax.ShapeDtypeStruct((t, 2 * LANES), jnp.float32),
                   jax.ShapeDtypeStruct((t // ROW_BLOCK * LANES, ROW_BLOCK), jnp.float32)),
        grid=(t // ROW_BLOCK,),
        in_specs=[pl.BlockSpec((ROW_BLOCK, LANES), lambda i: (i, tail_col0 // LANES)),
                  pl.BlockSpec((SUBLANES, LANES), lambda i: (0, 0))],
        out_specs=(pl.BlockSpec((ROW_BLOCK, 2 * LANES), lambda i: (i, 0)),
                   pl.BlockSpec((LANES, ROW_BLOCK), lambda i: (i, 0))),
        compiler_params=_cparams("parallel"),
        name="gdn_gates",
    )(proj, par)


def _gdn_prep_body(q_ref, k_ref, v_ref, qp_ref, kp_ref, vp_ref, qn_ref, kn_ref, vn_ref,
                   wq_ref, wk_ref, wv_ref, gt_ref, gtt_ref,
                   u_ref, w_ref, qd_ref, ke_ref, at_ref, ge_ref, *, heads, n_heads, blocks_per_seq, n_lat_blocks):
    rb = pl.program_id(0)
    hg = pl.program_id(1)
    n = ROW_BLOCK
    c = GDN_CHUNK
    n_ch = n // c
    cs = c.bit_length() - 1
    hd = GDN_HEAD_DIM
    is_ctx = rb >= n_lat_blocks
    first = jnp.logical_or(is_ctx, rb % blocks_per_seq == 0)
    last = jnp.logical_or(is_ctx, rb % blocks_per_seq == blocks_per_seq - 1)
    m_prev = jnp.where(first, 0.0, 1.0)
    m_next = jnp.where(last, 0.0, 1.0)

    def conv_silu(cur_ref, prev_ref, next_ref, cw_ref):
        ext = jnp.concatenate([prev_ref[...] * m_prev, cur_ref[...], next_ref[...] * m_next], axis=0)
        acc = None
        for j in range(GDN_CONV):
            lo = SUBLANES - GDN_CONV // 2 + j
            term = ext[lo:lo + ROW_BLOCK, :] * cw_ref[j:j + 1, :]
            acc = term if acc is None else acc + term
        return acc * jax.nn.sigmoid(acc)

    qc = conv_silu(q_ref, qp_ref, qn_ref, wq_ref)
    kc = conv_silu(k_ref, kp_ref, kn_ref, wk_ref)
    vc = conv_silu(v_ref, vp_ref, vn_ref, wv_ref)

    gt_lo = gt_ref[:, :LANES]
    gt_hi = gt_ref[:, LANES:]
    lane = lax.broadcasted_iota(jnp.int32, (n, LANES), 1)
    blk16 = _bf(jnp.where((lax.broadcasted_iota(jnp.int32, (n, n), 0) >> cs)
                          == (lax.broadcasted_iota(jnp.int32, (n, n), 1) >> cs), 1.0, 0.0))
    wr = lax.broadcasted_iota(jnp.int32, (c, n), 0)
    wc = lax.broadcasted_iota(jnp.int32, (c, n), 1)
    eye_wide = jnp.where(wr == (wc & (c - 1)), 1.0, 0.0)
    sr = lax.broadcasted_iota(jnp.int32, (c, LANES), 0)
    sc = lax.broadcasted_iota(jnp.int32, (c, LANES), 1)
    slab_masks = []
    for odd in range(LANES // c):
        rel = sc - odd * c
        own = (sc >> cs) == odd
        slab_masks.append(((own & (rel <= sr), own & (rel < sr)), (own & (rel >= sr), own & (rel > sr))))

    def tall(wide):
        return jnp.concatenate([_bf(wide)] * n_ch, axis=0) * blk16

    def column(src, idx):
        return jnp.sum(jnp.where(lane == idx, src, 0.0), axis=1, keepdims=True)

    per_tile = LANES // c
    chains = []
    for gi in range(heads):
        h = hg * heads + gi
        sl = slice(gi * hd, (gi + 1) * hd)
        qh, kh, v = qc[:, sl], kc[:, sl], vc[:, sl]
        q = qh * lax.rsqrt(jnp.sum(qh * qh, axis=-1, keepdims=True) + NORM_EPS) * (hd ** -0.5)
        k = kh * lax.rsqrt(jnp.sum(kh * kh, axis=-1, keepdims=True) + NORM_EPS)
        k16 = _bf(k)
        qk = _dot_nt(_bf(q), k16)
        kk = _dot_nt(k16, k16)
        for d in range(2):
            beta = column(gt_lo, d * n_heads + h)
            gc = column(gt_lo, (2 + d) * n_heads + h)
            tot = column(gt_hi, (2 + d) * n_heads + h)
            gc_row = gtt_ref[pl.ds((2 + d) * n_heads + h, 1), :]
            p_tiles = []
            for j in range(n_ch):
                rows = slice(j * c, (j + 1) * c)
                cols = slice((j // per_tile) * LANES, (j // per_tile + 1) * LANES)
                incl, strict = slab_masks[j % per_tile][d]
                decay = jnp.where(incl, jnp.exp(jnp.where(incl, gc[rows] - gc_row[:, cols], 0.0)), 0.0)
                a = jnp.where(strict, (beta[rows] * kk[rows, cols]) * decay, 0.0)
                if j % per_tile == 0:
                    p_tiles.append(-a)
                else:
                    p_tiles[-1] = p_tiles[-1] - a
                lo = (j % per_tile) * c
                at_ref[d, rows, gi * c:(gi + 1) * c] = _bf((qk[rows, cols] * decay)[:, lo:lo + c])
                ge_ref[d, j * SUBLANES:(j + 1) * SUBLANES, sl] = jnp.broadcast_to(
                    jnp.exp(tot[j * c:j * c + SUBLANES]), (SUBLANES, hd))
            p = jnp.concatenate(p_tiles, axis=1)
            eg = jnp.exp(gc)
            kbeta = k * beta
            rhs = _bf(jnp.concatenate([v * beta, kbeta * eg], axis=1))
            qd_ref[d, :, sl] = _bf(q * eg)
            ke_ref[d, :, sl] = _bf(k * jnp.exp(tot - gc))
            chains.append([d, sl, p, eye_wide + p, tall(p), rhs])

    span = 2
    while span < c:
        for ch in chains:
            ch[2] = _dot(_bf(ch[2]), ch[4])
            ch[4] = tall(ch[2])
            ch[3] = ch[3] + _dot(_bf(ch[3]), ch[4])
        span *= 2
    for d, sl, _, t, _, rhs in chains:
        sol = _dot(tall(t), rhs)
        u_ref[d, :, sl] = sol[:, :hd]
        w_ref[d, :, sl] = _bf(sol[:, hd:])


def gdn_prepare(proj, gates, gates_t, conv_w, n_heads, qkv_col0, blocks_per_seq, n_lat_blocks):
    t = proj.shape[0]
    hd = GDN_HEAD_DIM
    g = GDN_PREP_HEADS
    gw = g * hd
    dim = n_heads * hd
    nb = t // ROW_BLOCK
    cb0 = qkv_col0 // gw
    per = dim // gw
    halo_per_block = ROW_BLOCK // SUBLANES

    def cur(off):
        return pl.BlockSpec((ROW_BLOCK, gw), lambda rb, hg: (rb, cb0 + off * per + hg))

    def prev(off):
        return pl.BlockSpec((SUBLANES, gw),
                            lambda rb, hg: (jnp.maximum(rb * halo_per_block - 1, 0), cb0 + off * per + hg))

    def nxt(off):
        return pl.BlockSpec((SUBLANES, gw),
                            lambda rb, hg: (jnp.minimum((rb + 1) * halo_per_block, nb * halo_per_block - 1),
                                            cb0 + off * per + hg))

    def cw(off):
        return pl.BlockSpec((GDN_CONV, gw), lambda rb, hg: (0, off * per + hg))

    out_tok = lambda width: pl.BlockSpec((2, ROW_BLOCK, width), lambda rb, hg: (0, rb, hg))
    outs = pl.pallas_call(
        functools.partial(_gdn_prep_body, heads=g, n_heads=n_heads, blocks_per_seq=blocks_per_seq,
                          n_lat_blocks=n_lat_blocks),
        out_shape=(jax.ShapeDtypeStruct((2, t, dim), jnp.float32),
                   jax.ShapeDtypeStruct((2, t, dim), jnp.bfloat16),
                   jax.ShapeDtypeStruct((2, t, dim), jnp.bfloat16),
                   jax.ShapeDtypeStruct((2, t, dim), jnp.bfloat16),
                   jax.ShapeDtypeStruct((2, t, n_heads * GDN_CHUNK), jnp.bfloat16),
                   jax.ShapeDtypeStruct((2, t // SUBLANES, dim), jnp.float32)),
        grid=(nb, n_heads // g),
        in_specs=[cur(0), cur(1), cur(2), prev(0), prev(1), prev(2), nxt(0), nxt(1), nxt(2),
                  cw(0), cw(1), cw(2), pl.BlockSpec((ROW_BLOCK, 2 * LANES), lambda rb, hg: (rb, 0)),
                  pl.BlockSpec((LANES, ROW_BLOCK), lambda rb, hg: (rb, 0))],
        out_specs=(out_tok(gw), out_tok(gw), out_tok(gw), out_tok(gw), out_tok(g * GDN_CHUNK),
                   pl.BlockSpec((2, ROW_BLOCK // SUBLANES, gw), lambda rb, hg: (0, rb, hg))),
        compiler_params=_cparams("parallel", "parallel"),
        name="gdn_prepare",
    )(proj, proj, proj, proj, proj, proj, proj, proj, proj, conv_w, conv_w, conv_w, gates, gates_t)
    return outs


def _gdn_scan_body(uf, wf, qf, kf, af, gf, ub, wb, qb, kb, ab, gb, of_ref, ob_ref, s_ref, *, heads):
    t = pl.program_id(2)
    c = GDN_CHUNK
    hd = GDN_HEAD_DIM
    n_ch = ROW_BLOCK // c

    @pl.when(t == 0)
    def _():
        s_ref[...] = jnp.zeros_like(s_ref)

    pw = 2 * hd
    s_mask = ((lax.broadcasted_iota(jnp.int32, (pw, pw), 0) >> (hd.bit_length() - 1))
              == (lax.broadcasted_iota(jnp.int32, (pw, pw), 1) >> (hd.bit_length() - 1)))
    v_mask = ((lax.broadcasted_iota(jnp.int32, (2 * c, pw), 0) >> (c.bit_length() - 1))
              == (lax.broadcasted_iota(jnp.int32, (2 * c, pw), 1) >> (hd.bit_length() - 1)))
    views = ((uf, wf, qf, kf, af, gf, of_ref), (ub, wb, qb, kb, ab, gb, ob_ref))
    chains = [(d, p) for d in range(2) for p in range(heads // 2)]
    for step in range(n_ch):
        def locate(d, p):
            ch = step if d == 0 else n_ch - 1 - step
            return views[d], ch, slice(ch * c, (ch + 1) * c), slice(p * pw, (p + 1) * pw)

        state, r_all, v_all = {}, {}, {}
        for d, p in chains:
            (u_r, w_r, q_r, k_r, a_r, g_r, o_r), ch, rows, sl = locate(d, p)
            s = s_ref[d, p]
            state[d, p] = s
            s_diag = _bf(jnp.where(s_mask, jnp.concatenate([s, s], axis=0), 0.0))
            r_all[d, p] = _dot(jnp.concatenate([w_r[0, rows, sl], q_r[0, rows, sl]], axis=0), s_diag)
        for d, p in chains:
            (u_r, w_r, q_r, k_r, a_r, g_r, o_r), ch, rows, sl = locate(d, p)
            v_new = u_r[0, rows, sl] - r_all[d, p][:c]
            v_all[d, p] = _bf(jnp.where(v_mask, jnp.concatenate([v_new, v_new], axis=0), 0.0))
        for d, p in chains:
            (u_r, w_r, q_r, k_r, a_r, g_r, o_r), ch, rows, sl = locate(d, p)
            o_r[0, rows, sl] = r_all[d, p][c:] + _dot(a_r[0, rows, p * 2 * c:(p + 1) * 2 * c], v_all[d, p])
        for d, p in chains:
            (u_r, w_r, q_r, k_r, a_r, g_r, o_r), ch, rows, sl = locate(d, p)
            k_stack = jnp.concatenate([k_r[0, rows, p * pw:p * pw + hd], k_r[0, rows, p * pw + hd:(p + 1) * pw]],
                                      axis=0)
            ge = g_r[0, ch * SUBLANES:ch * SUBLANES + 1, sl]
            s_ref[d, p] = state[d, p] * ge + _dot_tn(k_stack, v_all[d, p])


def gdn_scan(u, w, qd, ke, at, ge, n_heads, n_batch, blocks_per_seq):
    t = u.shape[1]
    hd = GDN_HEAD_DIM
    g = GDN_SCAN_HEADS
    gw = g * hd
    n_lat_blocks = n_batch * blocks_per_seq

    def fwd(b, s):
        return jnp.where(s == 0, n_lat_blocks + b, b * blocks_per_seq + s - 1)

    def bwd(b, s):
        return jnp.where(s == 0, n_lat_blocks + b, b * blocks_per_seq + blocks_per_seq - s)

    def tok(d, width, blk):
        return pl.BlockSpec((1, ROW_BLOCK, width), lambda b, hg, s: (d, blk(b, s), hg))

    def gsp(d, blk):
        return pl.BlockSpec((1, ROW_BLOCK // SUBLANES, gw), lambda b, hg, s: (d, blk(b, s), hg))

    in_specs = []
    for d, blk in ((0, fwd), (1, bwd)):
        in_specs += [tok(d, gw, blk), tok(d, gw, blk), tok(d, gw, blk), tok(d, gw, blk),
                     tok(d, g * GDN_CHUNK, blk), gsp(d, blk)]
    o_f, o_b = pl.pallas_call(
        functools.partial(_gdn_scan_body, heads=g),
        out_shape=(jax.ShapeDtypeStruct((1, t, n_heads * hd), jnp.float32),) * 2,
        grid=(n_batch, n_heads // g, blocks_per_seq + 1),
        in_specs=in_specs,
        out_specs=(pl.BlockSpec((1, ROW_BLOCK, gw), lambda b, hg, s: (0, fwd(b, s), hg)),
                   pl.BlockSpec((1, ROW_BLOCK, gw), lambda b, hg, s: (0, bwd(b, s), hg))),
        scratch_shapes=[pltpu.VMEM((2, g // 2, hd, 2 * hd), jnp.float32)],
        compiler_params=_cparams("parallel", "parallel", "arbitrary"),
        name="gdn_scan",
    )(u, w, qd, ke, at, ge, u, w, qd, ke, at, ge)
    return o_f[0], o_b[0]


GATED_NORM_COLS = 1024


def _gated_norm_body(of_ref, ob_ref, z_ref, nw_ref, lead_ref, y_ref, *, hd):
    j = pl.program_id(1)

    @pl.when(j == 0)
    def _():
        y_ref[...] = lead_ref[...]

    @pl.when(j > 0)
    def _():
        for h in range(GATED_NORM_COLS // hd):
            sl = slice(h * hd, (h + 1) * hd)
            o = of_ref[:, sl] + ob_ref[:, sl]
            y = o * lax.rsqrt(jnp.mean(o * o, axis=-1, keepdims=True) + NORM_EPS) * nw_ref[...]
            z = z_ref[:, sl]
            y_ref[:, sl] = (y * (z * jax.nn.sigmoid(z))).astype(y_ref.dtype)


def gated_head_norm(o_f, o_b, proj, z_col0, norm_w, hd, lead):
    m, cw = lead.shape
    dim = o_f.shape[1]
    assert cw == GATED_NORM_COLS and z_col0 % cw == 0 and dim % cw == 0 and cw % hd == 0

    def col(j):
        return jnp.maximum(j - 1, 0)

    return pl.pallas_call(
        functools.partial(_gated_norm_body, hd=hd),
        out_shape=jax.ShapeDtypeStruct((m, cw + dim), jnp.bfloat16),
        grid=(m // ROW_BLOCK, 1 + dim // cw),
        in_specs=[pl.BlockSpec((ROW_BLOCK, cw), lambda i, j: (i, col(j))),
                  pl.BlockSpec((ROW_BLOCK, cw), lambda i, j: (i, col(j))),
                  pl.BlockSpec((ROW_BLOCK, cw), lambda i, j: (i, z_col0 // cw + col(j))),
                  pl.BlockSpec((1, hd), lambda i, j: (0, 0)),
                  pl.BlockSpec((ROW_BLOCK, cw), lambda i, j: (i, 0))],
        out_specs=pl.BlockSpec((ROW_BLOCK, cw), lambda i, j: (i, j)),
        compiler_params=_cparams("parallel", "parallel"),
        name="gated_head_norm",
    )(o_f, o_b, proj, norm_w.reshape(1, hd), lead)


def _pool_body(u_ref, w_ref, sc_ref, y_ref, *, n_lat_blocks):
    rb = pl.program_id(0)
    is_ctx = rb >= n_lat_blocks
    row_len = jnp.where(is_ctx, CTX_LEN, GRID_W)
    shift = jnp.where(is_ctx, CTX_LEN.bit_length() - 1, GRID_W.bit_length() - 1)
    n = ROW_BLOCK
    ri = lax.broadcasted_iota(jnp.int32, (n, n), 0)
    ci = lax.broadcasted_iota(jnp.int32, (n, n), 1)
    same = (ri >> shift) == (ci >> shift)
    pos_r = ri & (row_len - 1)
    pos_c = ci & (row_len - 1)
    rcol = lax.broadcasted_iota(jnp.int32, (n, 1), 0) & (row_len - 1)
    gd = POOL_GROUP_DIM
    for g, win in enumerate(POOL_WINDOWS):
        half = win // 2
        lo = jnp.maximum(pos_r - half, 0)
        hi = jnp.minimum(pos_r + half - 1, row_len - 1)
        band = _bf(jnp.where(same & (pos_c >= lo) & (pos_c <= hi), 1.0, 0.0))
        cnt = (jnp.minimum(rcol + half - 1, row_len - 1) - jnp.maximum(rcol - half, 0) + 1).astype(jnp.float32)
        u = u_ref[:, g * gd:(g + 1) * gd]
        parts = _split3(u)
        win_sum = _dot(band, parts[0]) + _dot(band, parts[1]) + _dot(band, parts[2])
        dlt = win_sum / cnt - u
        y = _dot(_bf(dlt), _bf(w_ref[g])) * sc_ref[:, g * gd:(g + 1) * gd]
        y_ref[:, g * gd:(g + 1) * gd] = y.astype(y_ref.dtype)


def pool_mix(proj, pool_w, pool_scale, n_lat_blocks):
    t = proj.shape[0]
    return pl.pallas_call(
        functools.partial(_pool_body, n_lat_blocks=n_lat_blocks),
        out_shape=jax.ShapeDtypeStruct((t, POOL_DIM), jnp.bfloat16),
        grid=(t // ROW_BLOCK,),
        in_specs=[pl.BlockSpec((ROW_BLOCK, POOL_DIM), lambda i: (i, 0)),
                  pl.BlockSpec((POOL_GROUPS, POOL_GROUP_DIM, POOL_GROUP_DIM), lambda i: (0, 0, 0)),
                  pl.BlockSpec((1, POOL_DIM), lambda i: (0, 0))],
        out_specs=pl.BlockSpec((ROW_BLOCK, POOL_DIM), lambda i: (i, 0)),
        compiler_params=_cparams("parallel"),
        name="pool_mix",
    )(proj, pool_w, pool_scale.reshape(1, POOL_DIM))


def _moe_up_body(be_ref, first_ref, nused_ref, src_ref, src_next_ref, f_ref, w1_ref, w3_ref, h_ref,
                 w1_bf, w3_bf, rows_ref, sem):
    del be_ref
    i = pl.program_id(0)
    bm = rows_ref.shape[1]
    n_used = nused_ref[0]

    def row_copy(slot, r, src_row):
        return pltpu.make_async_copy(f_ref.at[pl.ds(src_row, 1), :], rows_ref.at[slot, pl.ds(r, 1), :], sem.at[slot])

    def issue(slot, idx_ref):
        def body(r, carry):
            row_copy(slot, r, idx_ref[0, 0, r]).start()
            return carry
        lax.fori_loop(0, bm, body, 0, unroll=8)

    @pl.when(jnp.logical_and(i == 0, n_used > 0))
    def _():
        issue(0, src_ref)

    @pl.when(i + 1 < n_used)
    def _():
        issue(lax.rem(i + 1, 2), src_next_ref)

    @pl.when(first_ref[i] == 1)
    def _():
        w1_bf[...] = _bf(w1_ref[0])
        w3_bf[...] = _bf(w3_ref[0])

    @pl.when(i < n_used)
    def _():
        slot = lax.rem(i, 2)

        def drain(r, carry):
            row_copy(slot, r, 0).wait()
            return carry
        lax.fori_loop(0, bm, drain, 0, unroll=8)
        x = _bf(rows_ref[slot])
        a = _dot(x, w1_bf[...])
        b = _dot(x, w3_bf[...])
        h_ref[...] = _bf(a * jax.nn.sigmoid(a) * b)

    @pl.when(i >= nused_ref[0])
    def _():
        h_ref[...] = jnp.zeros_like(h_ref)


def _moe_down_body(be_ref, first_ref, nused_ref, h_ref, w2_ref, o_ref, w2_bf):
    del be_ref
    i = pl.program_id(0)

    @pl.when(first_ref[i] == 1)
    def _():
        w2_bf[...] = _bf(w2_ref[0])

    @pl.when(i < nused_ref[0])
    def _():
        o_ref[...] = _dot(h_ref[...], w2_bf[...]).astype(o_ref.dtype)

    @pl.when(i >= nused_ref[0])
    def _():
        o_ref[...] = jnp.zeros_like(o_ref)


def moe_expert_blocks(f, src, block_expert, block_first, n_used, w1, w3, w2):
    d = f.shape[1]
    bm = MOE_BLOCK_ROWS
    nb = src.shape[0] // bm
    p = nb * bm
    hid = w1.shape[-1]
    src_blocks = src.reshape(nb, 1, bm)
    up_spec = pltpu.PrefetchScalarGridSpec(
        num_scalar_prefetch=3,
        grid=(nb,),
        in_specs=[pl.BlockSpec((1, 1, bm), lambda i, be, fi, nu: (i, 0, 0), memory_space=pltpu.SMEM),
                  pl.BlockSpec((1, 1, bm), lambda i, be, fi, nu: (jnp.minimum(i + 1, nb - 1), 0, 0),
                               memory_space=pltpu.SMEM),
                  pl.BlockSpec(memory_space=pl.ANY),
                  pl.BlockSpec((1, d, hid), lambda i, be, fi, nu: (be[i], 0, 0)),
                  pl.BlockSpec((1, d, hid), lambda i, be, fi, nu: (be[i], 0, 0))],
        out_specs=pl.BlockSpec((bm, hid), lambda i, be, fi, nu: (i, 0)),
        scratch_shapes=[pltpu.VMEM((d, hid), jnp.bfloat16), pltpu.VMEM((d, hid), jnp.bfloat16),
                        pltpu.VMEM((2, bm, d), jnp.float32), pltpu.SemaphoreType.DMA((2,))],
    )
    hbuf = pl.pallas_call(
        _moe_up_body,
        out_shape=jax.ShapeDtypeStruct((p, hid), jnp.bfloat16),
        grid_spec=up_spec,
        compiler_params=pltpu.CompilerParams(dimension_semantics=("arbitrary",), vmem_limit_bytes=MOE_UP_VMEM_LIMIT),
        name="moe_up",
    )(block_expert, block_first, n_used, src_blocks, src_blocks, f, w1, w3)
    down_spec = pltpu.PrefetchScalarGridSpec(
        num_scalar_prefetch=3,
        grid=(nb,),
        in_specs=[pl.BlockSpec((bm, hid), lambda i, be, fi, nu: (i, 0)),
                  pl.BlockSpec((1, hid, d), lambda i, be, fi, nu: (be[i], 0, 0))],
        out_specs=pl.BlockSpec((bm, d), lambda i, be, fi, nu: (i, 0)),
        scratch_shapes=[pltpu.VMEM((hid, d), jnp.bfloat16)],
    )
    return pl.pallas_call(
        _moe_down_body,
        out_shape=jax.ShapeDtypeStruct((p, d), jnp.float32),
        grid_spec=down_spec,
        compiler_params=_cparams("arbitrary"),
        name="moe_down",
    )(block_expert, block_first, n_used, hbuf, w2)


def _moe_route_body(h_ref, w_ref, b_ref, ids_ref, wgt_ref, cnt_ref, run_ref):
    i = pl.program_id(0)

    @pl.when(i == 0)
    def _():
        run_ref[...] = jnp.zeros_like(run_ref)

    n = h_ref.shape[0]
    ng, ne = MOE_GROUPS, MOE_EXPERTS_PER_GROUP
    h = _bf(h_ref[...])
    logits = _dot(h, w_ref[0]) + _dot(h, w_ref[1]) + _dot(h, w_ref[2]) + b_ref[...]
    lane = lax.broadcasted_iota(jnp.int32, logits.shape, 1)
    neg = jnp.float32(-jnp.inf)

    lane_f = lane.astype(jnp.float32)

    def first_argmax(vals):
        top = jnp.max(vals, axis=1, keepdims=True)
        return top, jnp.min(jnp.where(vals == top, lane_f, float(LANES)), axis=1, keepdims=True).astype(jnp.int32)

    gl = jnp.where(lane < ng, logits, neg)
    g_top, group = first_argmax(gl)
    p_group = 1.0 / jnp.sum(jnp.exp(gl - g_top), axis=1, keepdims=True)
    lo = ng + group * ne
    sel = jnp.where((lane >= lo) & (lane < lo + ne), logits, neg)
    v1, i1 = first_argmax(sel)
    v2, i2 = first_argmax(jnp.where(lane == i1, neg, sel))
    e21 = jnp.exp(v2 - v1)
    w1 = p_group / (1.0 + e21)
    w2 = p_group * e21 / (1.0 + e21)
    e1 = i1 - ng
    e2 = i2 - ng
    oh1 = jnp.where(lane == e1, 1.0, 0.0)
    oh2 = jnp.where(lane == e2, 1.0, 0.0)
    both = oh1 + oh2
    ri = lax.broadcasted_iota(jnp.int32, (n, n), 0)
    ci = lax.broadcasted_iota(jnp.int32, (n, n), 1)
    before = _dot(_bf(jnp.where(ci < ri, 1.0, 0.0)), _bf(both)) + run_ref[0:1, :]
    r1 = jnp.sum(oh1 * before, axis=1, keepdims=True)
    r2 = jnp.sum(oh2 * before, axis=1, keepdims=True)
    run_ref[...] = run_ref[...] + jnp.sum(both, axis=0, keepdims=True)
    ids_ref[...] = jnp.where(lane == 0, e1, jnp.where(lane == 1, e2, jnp.where(
        lane == 2, r1.astype(jnp.int32), jnp.where(lane == 3, r2.astype(jnp.int32), 0))))
    wgt_ref[...] = jnp.where(lane == 0, w1, jnp.where(lane == 1, w2, 0.0))
    cnt_ref[...] = run_ref[...].astype(jnp.int32)


def moe_route(h, group_w, group_b, expert_w, expert_b):
    t, d = h.shape
    w = jnp.concatenate([group_w, expert_w.transpose(1, 0, 2).reshape(d, MOE_EXPERTS)], axis=1).astype(jnp.float32)
    w = jnp.stack(_split3(_pad_cols(w)))
    b = _pad_cols(jnp.concatenate([group_b, expert_b.reshape(-1)])[None, :].astype(jnp.float32))
    tm = ROW_BLOCK
    return pl.pallas_call(
        _moe_route_body,
        out_shape=(jax.ShapeDtypeStruct((t, LANES), jnp.int32), jax.ShapeDtypeStruct((t, LANES), jnp.float32),
                   jax.ShapeDtypeStruct((SUBLANES, LANES), jnp.int32)),
        grid=(t // tm,),
        in_specs=[pl.BlockSpec((tm, d), lambda i: (i, 0)), pl.BlockSpec((3, d, LANES), lambda i: (0, 0, 0)),
                  pl.BlockSpec((1, LANES), lambda i: (0, 0))],
        out_specs=(pl.BlockSpec((tm, LANES), lambda i: (i, 0)), pl.BlockSpec((tm, LANES), lambda i: (i, 0)),
                   pl.BlockSpec((SUBLANES, LANES), lambda i: (0, 0))),
        scratch_shapes=[pltpu.VMEM((SUBLANES, LANES), jnp.float32)],
        compiler_params=_cparams("arbitrary"),
        name="moe_route",
    )(h, w, b)


def hier_moe(h, group_w, group_b, expert_w, expert_b, w1, w3, w2, expert0):
    t, d = h.shape
    ids, weight, counts = moe_route(h, group_w, group_b, expert_w, expert_b)
    expert = ids[:, :MOE_TOP_K].T.reshape(-1)
    rank = ids[:, MOE_TOP_K:2 * MOE_TOP_K].T.reshape(-1)
    counts = counts[0, :MOE_EXPERTS]
    tk = t * MOE_TOP_K
    bm = MOE_BLOCK_ROWS
    padded = (counts + bm - 1) // bm * bm
    e_idx = jnp.arange(MOE_EXPERTS)
    pad_end = jnp.sum(jnp.where(e_idx[None, :] <= e_idx[:, None], padded[None, :], 0), axis=1)
    pad_start = pad_end - padded
    dest = pad_start[expert] + rank
    n_blocks = -(-tk // bm) + MOE_EXPERTS
    token = jnp.tile(jnp.arange(t, dtype=jnp.int32), MOE_TOP_K)
    src = jnp.zeros((n_blocks * bm,), jnp.int32).at[dest].set(token)
    block_start = jnp.arange(n_blocks, dtype=jnp.int32) * bm
    block_expert = jnp.minimum(jnp.sum((pad_end[None, :] <= block_start[:, None]).astype(jnp.int32), axis=1),
                               MOE_EXPERTS - 1)
    block_first = jnp.concatenate([jnp.ones((1,), jnp.int32),
                                   (block_expert[1:] != block_expert[:-1]).astype(jnp.int32)])
    n_used = (pad_end[-1:] // bm).astype(jnp.int32)
    ybuf = moe_expert_blocks(h, src, block_expert + expert0, block_first, n_used, w1, w3, w2)
    return ybuf, dest.astype(jnp.int32), weight


GLA_SUB = 32
GLA_STEP = 64


def _gla_prep_body(q_ref, k_ref, v_ref, lr_ref, wg_ref, gb_ref, qd_ref, ke_ref, at_ref, ge_ref, v16_ref):
    n = ROW_BLOCK
    c = GLA_STEP
    sb = GLA_SUB
    n_sb = n // sb
    q = q_ref[...] * (GLA_HEAD_K ** -0.5)
    k = k_ref[...]
    v16_ref[...] = _bf(v_ref[...])
    lr = lr_ref[...]
    ri = lax.broadcasted_iota(jnp.int32, (n, n), 0)
    ci = lax.broadcasted_iota(jnp.int32, (n, n), 1)
    cs = c.bit_length() - 1
    ss = sb.bit_length() - 1
    same_c = (ri >> cs) == (ci >> cs)
    same_s = (ri >> ss) == (ci >> ss)
    dirs = range(2)
    xs = [jnp.dot(lr, wg_ref[d], preferred_element_type=jnp.float32, precision=lax.Precision.HIGHEST) + gb_ref[d]
          for d in dirs]
    gks = [_split3((jnp.minimum(x, 0.0) - jnp.log1p(jnp.exp(-jnp.abs(x)))) * (1.0 / GLA_GATE_TEMP)) for x in xs]
    runs = [_bf(jnp.where(same_c & ((ci <= ri) if d == 0 else (ci >= ri)), 1.0, 0.0)) for d in dirs]
    gs = [_dot(runs[d], gks[d][0]) + _dot(runs[d], gks[d][1]) + _dot(runs[d], gks[d][2]) for d in dirs]
    locs, tots, totc = [], [], []
    for d in dirs:
        g = gs[d]
        off, tot_s, tot_c = [], [], []
        for b in range(n_sb):
            lo = b * sb
            inner = (b % 2 == 1) if d == 0 else (b % 2 == 0)
            edge = g[lo + sb - 1:lo + sb] if d == 0 else g[lo:lo + 1]
            if inner:
                prev = g[lo - 1:lo] if d == 0 else g[lo + sb:lo + sb + 1]
                off.append(jnp.broadcast_to(prev, (sb, prev.shape[1])))
                tot_s.append(jnp.broadcast_to(edge - prev, (sb, prev.shape[1])))
            else:
                off.append(jnp.zeros((sb, g.shape[1]), jnp.float32))
                tot_s.append(jnp.broadcast_to(edge, (sb, edge.shape[1])))
        for j in range(n // c):
            edge = g[j * c + c - 1:j * c + c] if d == 0 else g[j * c:j * c + 1]
            tot_c.append(jnp.broadcast_to(edge, (c, edge.shape[1])))
        locs.append(g - jnp.concatenate(off, axis=0))
        tots.append(jnp.concatenate(tot_s, axis=0))
        totc.append(jnp.concatenate(tot_c, axis=0))
    qd_loc = [_bf(q * jnp.exp(locs[d])) for d in dirs]
    m_diag = [_dot_nt(qd_loc[d], _bf(k * jnp.exp(-locs[d]))) for d in dirs]
    m_cross = [_dot_nt(qd_loc[d], _bf(k * jnp.exp(tots[d] - locs[d]))) for d in dirs]
    for d in dirs:
        tri = (ci <= ri) if d == 0 else (ci >= ri)
        nxt = ((ri >> ss) == (ci >> ss) + 1) if d == 0 else ((ri >> ss) + 1 == (ci >> ss))
        attn = jnp.where(same_s & tri, m_diag[d], 0.0) + jnp.where(same_c & nxt, m_cross[d], 0.0)
        qd_ref[d] = _bf(q * jnp.exp(gs[d]))
        ke_ref[d] = _bf(k * jnp.exp(totc[d] - gs[d]))
        for j in range(n // c):
            at_ref[d, 0, j * c:(j + 1) * c, :] = _bf(attn[j * c:(j + 1) * c, j * c:(j + 1) * c])
            ge_ref[d, j * SUBLANES:(j + 1) * SUBLANES, :] = jnp.exp(totc[d][j * c:j * c + SUBLANES])


def gla_prepare(proj, gate_up, gate_b):
    t = proj.shape[0]
    assert CD_MAIN % LANES == 0
    hk, hv = GLA_HEAD_K, GLA_HEAD_V
    wg = jnp.zeros((2, LANES, GLA_K_DIM), jnp.float32)
    for d in range(2):
        wg = wg.at[d, d * GLA_GATE_RANK:(d + 1) * GLA_GATE_RANK].set(gate_up[d].astype(jnp.float32))
    gb = gate_b.astype(jnp.float32).reshape(2, 1, GLA_K_DIM)
    q0, k0, v0 = CD_SPLITS[0] // hk, CD_SPLITS[1] // hk, CD_SPLITS[2] // hv
    assert CD_SPLITS[0] % hk == 0 and CD_SPLITS[1] % hk == 0 and CD_SPLITS[2] % hv == 0
    return pl.pallas_call(
        _gla_prep_body,
        out_shape=(jax.ShapeDtypeStruct((2, t, GLA_K_DIM), jnp.bfloat16),
                   jax.ShapeDtypeStruct((2, t, GLA_K_DIM), jnp.bfloat16),
                   jax.ShapeDtypeStruct((2, GLA_HEADS, t, GLA_STEP), jnp.bfloat16),
                   jax.ShapeDtypeStruct((2, t // SUBLANES, GLA_K_DIM), jnp.float32),
                   jax.ShapeDtypeStruct((t, GLA_V_DIM), jnp.bfloat16)),
        grid=(t // ROW_BLOCK, GLA_HEADS),
        in_specs=[pl.BlockSpec((ROW_BLOCK, hk), lambda rb, h: (rb, q0 + h)),
                  pl.BlockSpec((ROW_BLOCK, hk), lambda rb, h: (rb, k0 + h)),
                  pl.BlockSpec((ROW_BLOCK, hv), lambda rb, h: (rb, v0 + h)),
                  pl.BlockSpec((ROW_BLOCK, LANES), lambda rb, h: (rb, CD_MAIN // LANES)),
                  pl.BlockSpec((2, LANES, hk), lambda rb, h: (0, 0, h)),
                  pl.BlockSpec((2, 1, hk), lambda rb, h: (0, 0, h))],
        out_specs=(pl.BlockSpec((2, ROW_BLOCK, hk), lambda rb, h: (0, rb, h)),
                   pl.BlockSpec((2, ROW_BLOCK, hk), lambda rb, h: (0, rb, h)),
                   pl.BlockSpec((2, 1, ROW_BLOCK, GLA_STEP), lambda rb, h: (0, h, rb, 0)),
                   pl.BlockSpec((2, ROW_BLOCK // SUBLANES, hk), lambda rb, h: (0, rb, h)),
                   pl.BlockSpec((ROW_BLOCK, hv), lambda rb, h: (rb, h))),
        compiler_params=_cparams("parallel", "parallel"),
        name="gla_prepare",
    )(proj, proj, proj, proj, wg, gb)


def _gla_scan_body(qf, kf, af, gf, vf, qb, kb, ab, gb, vb, of_ref, ob_ref, s_ref):
    t = pl.program_id(2)
    c = GLA_STEP
    n_ch = ROW_BLOCK // c

    @pl.when(t == 0)
    def _():
        s_ref[...] = jnp.zeros_like(s_ref)

    views = ((qf, kf, af, gf, vf, of_ref), (qb, kb, ab, gb, vb, ob_ref))
    for step in range(n_ch):
        chunk = [step, n_ch - 1 - step]
        rows = [slice(ch * c, (ch + 1) * c) for ch in chunk]
        st = [s_ref[d] for d in range(2)]
        inter = [_dot_nt(views[d][0][0, rows[d], :], _bf(st[d])) for d in range(2)]
        intra = [_dot(views[d][2][0, 0, rows[d], :], views[d][4][rows[d], :]) for d in range(2)]
        upd = [_dot_tn(views[d][4][rows[d], :], views[d][1][0, rows[d], :]) for d in range(2)]
        for d in range(2):
            views[d][5][rows[d], :] = inter[d] + intra[d]
            ge = views[d][3][0, chunk[d] * SUBLANES:chunk[d] * SUBLANES + 1, :]
            s_ref[d] = st[d] * ge + upd[d]


def gla_scan(qd, ke, at, ge, v16, n_batch, blocks_per_seq):
    hk, hv = GLA_HEAD_K, GLA_HEAD_V
    t = v16.shape[0]
    n_lat_blocks = n_batch * blocks_per_seq

    def fwd(b, s):
        return jnp.where(s == 0, n_lat_blocks + b, b * blocks_per_seq + s - 1)

    def bwd(b, s):
        return jnp.where(s == 0, n_lat_blocks + b, b * blocks_per_seq + blocks_per_seq - s)

    def out_blk(blk):
        return lambda b, h, s: (blk(b, s), h)

    def key_spec(d, blk, rows):
        return pl.BlockSpec((1, rows, hk), lambda b, h, s: (d, blk(b, s), h))

    def attn_spec(d, blk):
        return pl.BlockSpec((1, 1, ROW_BLOCK, GLA_STEP), lambda b, h, s: (d, h, blk(b, s), 0))

    def val_spec(blk):
        return pl.BlockSpec((ROW_BLOCK, hv), lambda b, h, s: (blk(b, s), h))

    in_specs = []
    for d, blk in ((0, fwd), (1, bwd)):
        in_specs += [key_spec(d, blk, ROW_BLOCK), key_spec(d, blk, ROW_BLOCK), attn_spec(d, blk),
                     key_spec(d, blk, ROW_BLOCK // SUBLANES), val_spec(blk)]
    o_f, o_b = pl.pallas_call(
        _gla_scan_body,
        out_shape=(jax.ShapeDtypeStruct((t, GLA_V_DIM), jnp.float32),) * 2,
        grid=(n_batch, GLA_HEADS, blocks_per_seq + 1),
        in_specs=in_specs,
        out_specs=(pl.BlockSpec((ROW_BLOCK, hv), out_blk(fwd)), pl.BlockSpec((ROW_BLOCK, hv), out_blk(bwd))),
        scratch_shapes=[pltpu.VMEM((2, hv, hk), jnp.float32)],
        compiler_params=_cparams("parallel", "parallel", "arbitrary"),
        name="gla_scan",
    )(qd, ke, at, ge, v16, qd, ke, at, ge, v16)
    return o_f, o_b


def _fourier_in_body(x_ref, w_ref, o_ref):
    y = _dot(_bf(x_ref[...]), w_ref[0])
    half = y.shape[1] // 2
    o_ref[0] = _bf(y[:, :half])
    o_ref[1] = _bf(y[:, half:])


def dft_tables(n, nc):
    r = 64
    assert n % r == 0
    kk = jnp.arange(n, dtype=jnp.int32)[None, :]
    j1 = jnp.arange(n // r, dtype=jnp.int32)[:, None]
    j0 = jnp.arange(r, dtype=jnp.int32)[:, None]
    a = ((j1 * kk) % (n // r)).astype(jnp.float32) * (2.0 * jnp.pi / (n // r))
    b = ((j0 * kk) % n).astype(jnp.float32) * (2.0 * jnp.pi / n)
    ca, sa, cb, sb = jnp.cos(a)[:, None, :], jnp.sin(a)[:, None, :], jnp.cos(b)[None, :, :], jnp.sin(b)[None, :, :]
    scale = n ** -0.5
    cos_n = ((ca * cb - sa * sb) * scale).reshape(n, n)
    sin_n = ((sa * cb + ca * sb) * scale).reshape(n, n)
    table = jnp.concatenate([cos_n, -sin_n], axis=1).astype(jnp.bfloat16)
    cc = jnp.arange(nc, dtype=jnp.int32)
    ang = ((cc[:, None] * cc[None, :]) % nc).astype(jnp.float32) * (2.0 * jnp.pi / nc)
    return table, jnp.cos(ang) * nc ** -0.5, jnp.sin(ang) * nc ** -0.5


def fourier_mix(proj, fourier_w, n_batch, n_lat):
    gd = FOURIER_GROUP_DIM
    ng = FOURIER_GROUPS
    table, cos_c, sin_c = dft_tables(n_lat, gd)
    w_all = fourier_w.astype(jnp.float32).transpose(1, 0, 2).reshape(gd, ng * gd)
    folded = dense_matmul([(jnp.concatenate([cos_c, sin_c], axis=0), w_all)], tm=2 * gd, tn=512)
    w2 = jnp.concatenate([folded[:gd].reshape(gd, ng, gd), folded[gd:].reshape(gd, ng, gd)], axis=2)
    w2 = w2.transpose(1, 0, 2).astype(jnp.bfloat16)
    tm = min(DENSE_TM, n_lat)
    tiles = n_lat // tm
    z = pl.pallas_call(
        _fourier_in_body,
        out_shape=jax.ShapeDtypeStruct((2, n_lat, n_batch * ng * gd), jnp.bfloat16),
        grid=(n_batch, ng, tiles),
        in_specs=[pl.BlockSpec((tm, gd), lambda b, g, i: (b * tiles + i, g)),
                  pl.BlockSpec((1, gd, 2 * gd), lambda b, g, i: (g, 0, 0))],
        out_specs=pl.BlockSpec((2, tm, gd), lambda b, g, i: (0, i, b * ng + g)),
        compiler_params=_cparams("parallel", "parallel", "parallel"),
        name="fourier_in",
    )(proj, w2)
    z = z.reshape(2 * n_lat, n_batch * ng * gd)
    tn = 512
    per_b = ng * gd // tn
    return dense_matmul([(table, z)], out_dtype=jnp.bfloat16, tm=512, tn=tn,
                        out_shape=(n_batch * n_lat, ng * gd),
                        out_index=lambda i, j: ((j // per_b) * (n_lat // 512) + i, j % per_b))


def modulation(c, c_ctx, mod_w, mod_b, layer):
    b, d = c.shape
    rows = jnp.concatenate([c, c_ctx[None, :], jnp.zeros((SUBLANES - b - 1, d), c.dtype)], axis=0)
    out = dense_matmul([(jax.nn.silu(rows), mod_w.reshape(-1, mod_w.shape[-1]))], tm=SUBLANES, tn=512,
                       w_row_block=layer) + mod_b[layer]
    return tuple(m[:, None, :] for m in jnp.split(out, N_MOD, axis=-1))


def _combine_body(dest_ref, x_ref, ybuf_ref, w_ref, gate_ref, nw_ref, o_ref, rows_ref, sem, *, final_norm):
    tm = x_ref.shape[0]

    def row_copy(r, k, src_row):
        return pltpu.make_async_copy(ybuf_ref.at[pl.ds(src_row, 1), :], rows_ref.at[k, pl.ds(r, 1), :], sem.at[k])

    def issue(r, carry):
        for k in range(MOE_TOP_K):
            row_copy(r, k, dest_ref[0, 0, k * tm + r]).start()
        return carry

    def drain(r, carry):
        for k in range(MOE_TOP_K):
            row_copy(r, k, 0).wait()
        return carry

    lax.fori_loop(0, tm, issue, 0, unroll=8)
    lax.fori_loop(0, tm, drain, 0, unroll=8)
    y = w_ref[:, 0:1] * rows_ref[0] + w_ref[:, 1:2] * rows_ref[1]
    x = x_ref[...] + gate_ref[0] * y
    if final_norm:
        x = x * lax.rsqrt(jnp.mean(x * x, axis=-1, keepdims=True) + NORM_EPS) * nw_ref[...]
    o_ref[...] = x


def moe_combine(x, ybuf, dest, weight, gates, gate_index, final_norm_w=None):
    m, d = dest.shape[0] // MOE_TOP_K, x.shape[1]
    tm = ROW_BLOCK
    dest_tiles = dest.reshape(MOE_TOP_K, m // tm, tm).transpose(1, 0, 2).reshape(m // tm, 1, MOE_TOP_K * tm)
    nw = jnp.ones((1, d), jnp.float32) if final_norm_w is None else final_norm_w.reshape(1, d).astype(jnp.float32)
    return pl.pallas_call(
        functools.partial(_combine_body, final_norm=final_norm_w is not None),
        out_shape=jax.ShapeDtypeStruct((m, d), jnp.float32),
        grid=(m // tm,),
        in_specs=[pl.BlockSpec((1, 1, MOE_TOP_K * tm), lambda i: (i, 0, 0), memory_space=pltpu.SMEM),
                  pl.BlockSpec((tm, d), lambda i: (i, 0)),
                  pl.BlockSpec(memory_space=pl.ANY),
                  pl.BlockSpec((tm, LANES), lambda i: (i, 0)),
                  pl.BlockSpec((1, 1, d), lambda i: (gate_index(i), 0, 0)),
                  pl.BlockSpec((1, d), lambda i: (0, 0))],
        out_specs=pl.BlockSpec((tm, d), lambda i: (i, 0)),
        scratch_shapes=[pltpu.VMEM((MOE_TOP_K, tm, d), jnp.float32), pltpu.SemaphoreType.DMA((MOE_TOP_K,))],
        compiler_params=_cparams("parallel"),
        name="moe_combine",
    )(dest_tiles, x, ybuf, weight, gates, nw)


def kernel(x, c, ctx, c_ctx, mod_w, mod_b, norm1_w, norm2_w, ab_w_in, pool_w, pool_scale, gdn_conv_w,
           gdn_a_log, gdn_dt_bias, gdn_norm_w, ab_w_out, cd_w_in, fourier_w, gla_gate_up, gla_gate_b,
           gla_norm_w, cd_w_out, moe_group_w, moe_group_b, moe_expert_w, moe_expert_b, moe_w1, moe_w3,
           moe_w2, final_norm_w):
    assert DEPTH == 2
    n_batch, n_lat, d = x.shape
    n_ctx = ctx.shape[1]
    assert n_ctx == ROW_BLOCK and n_lat % DENSE_TM == 0 and (n_batch * n_ctx) % DENSE_TM == 0
    t_lat = n_batch * n_lat
    blocks_per_seq = n_lat // ROW_BLOCK
    n_lat_blocks = n_batch * blocks_per_seq
    xs = jnp.concatenate([x.reshape(t_lat, d), ctx.reshape(n_batch * n_ctx, d)], axis=0)

    def mod_row(rows_per_tile):
        tiles_per_seq = n_lat // rows_per_tile
        return lambda i: jnp.minimum(i // tiles_per_seq, n_batch)

    bf = jnp.bfloat16
    w1_all, w3_all, w2_all = (w.reshape((-1,) + w.shape[2:]) for w in (moe_w1, moe_w3, moe_w2))
    sh1, sc1, g1, sh2, sc2, g2 = modulation(c, c_ctx, mod_w, mod_b, 0)
    h = norm_modulate(xs, norm1_w[0], sh1, sc1, mod_row(512))
    proj = dense_matmul([(h, _pad_cols(ab_w_in[0].astype(bf), DENSE_TN))])
    gates, gates_t = gdn_gates(proj, AB_MAIN, gdn_a_log[0], gdn_dt_bias[0], GDN_HEADS)
    u, w, qd, ke, at, ge = gdn_prepare(proj, gates, gates_t, gdn_conv_w[0], GDN_HEADS, POOL_DIM, blocks_per_seq,
                                       n_lat_blocks)
    o_f, o_b = gdn_scan(u, w, qd, ke, at, ge, GDN_HEADS, n_batch, blocks_per_seq)
    y = gated_head_norm(o_f, o_b, proj, POOL_DIM + 3 * GDN_DIM, gdn_norm_w[0], GDN_HEAD_DIM,
                        pool_mix(proj, pool_w[0], pool_scale[0], n_lat_blocks))
    xs = dense_matmul([(y, ab_w_out[0].astype(bf))], residual=xs, gates=g1, gate_index=mod_row(DENSE_TM))
    f = norm_modulate(xs, norm2_w[0], sh2, sc2, mod_row(512), out_dtype=jnp.float32)
    ybuf, dest, wgt = hier_moe(f, moe_group_w[0], moe_group_b[0], moe_expert_w[0], moe_expert_b[0], w1_all, w3_all,
                               w2_all, 0)
    xs = moe_combine(xs, ybuf, dest, wgt, g2, mod_row(ROW_BLOCK))

    sh1, sc1, g1, sh2, sc2, g2 = modulation(c, c_ctx, mod_w, mod_b, 1)
    h = norm_modulate(xs, norm1_w[1], sh1, sc1, mod_row(512))
    proj = dense_matmul([(h, _pad_cols(cd_w_in[0].astype(bf), DENSE_TN))])
    qd, ke, at, ge, v16 = gla_prepare(proj, gla_gate_up[0], gla_gate_b[0])
    o_f, o_b = gla_scan(qd, ke, at, ge, v16, n_batch, blocks_per_seq)
    y = gated_head_norm(o_f, o_b, proj, CD_SPLITS[3], gla_norm_w[0], GLA_HEAD_V,
                        fourier_mix(proj, fourier_w[0], n_batch, n_lat))
    xl = dense_matmul([(y, cd_w_out[0].astype(bf))], residual=xs, gates=g1, gate_index=mod_row(DENSE_TM))
    f = norm_modulate(xl, norm2_w[1], sh2, sc2, mod_row(512), out_dtype=jnp.float32)
    ybuf, dest, wgt = hier_moe(f, moe_group_w[1], moe_group_b[1], moe_expert_w[1], moe_expert_b[1], w1_all, w3_all,
                               w2_all, MOE_EXPERTS)
    out = moe_combine(xl, ybuf, dest, wgt, g2, mod_row(ROW_BLOCK), final_norm_w=final_norm_w)
    return out.reshape(n_batch, n_lat, d)
```

```python
import functools

import jax
import jax.numpy as jnp
from jax import lax
from jax.experimental import pallas as pl
from jax.experimental.pallas import tpu as pltpu

D_MODEL = 4096
DEPTH = 2
CTX_LEN = 256
GRID_W = 64
N_MOD = 6
NORM_EPS = 1e-6

POOL_WINDOWS = (2, 4, 8, 16)
POOL_GROUPS = 4
POOL_GROUP_DIM = D_MODEL // 16
POOL_DIM = POOL_GROUPS * POOL_GROUP_DIM

GDN_HEAD_DIM = 128
GDN_DIM = D_MODEL - POOL_DIM
GDN_HEADS = GDN_DIM // GDN_HEAD_DIM
GDN_CONV = 5
GDN_CHUNK = 64

FOURIER_GROUPS = 4
FOURIER_GROUP_DIM = D_MODEL // 16
FOURIER_DIM = FOURIER_GROUPS * FOURIER_GROUP_DIM

GLA_HEADS = 6
GLA_V_DIM = D_MODEL - FOURIER_DIM
GLA_K_DIM = GLA_V_DIM // 2
GLA_HEAD_K = GLA_K_DIM // GLA_HEADS
GLA_HEAD_V = GLA_V_DIM // GLA_HEADS
GLA_GATE_RANK = 16
GLA_GATE_TEMP = 16.0
GLA_CHUNK = 32

MOE_GROUPS = 4
MOE_EXPERTS_PER_GROUP = 8
MOE_EXPERTS = MOE_GROUPS * MOE_EXPERTS_PER_GROUP
MOE_TOP_K = 2
MOE_HIDDEN = D_MODEL // 8

AB_MAIN = POOL_DIM + 4 * GDN_DIM
CD_SPLITS = (FOURIER_DIM, FOURIER_DIM + GLA_K_DIM, FOURIER_DIM + 2 * GLA_K_DIM,
             FOURIER_DIM + 2 * GLA_K_DIM + GLA_V_DIM, FOURIER_DIM + 2 * GLA_K_DIM + 2 * GLA_V_DIM)
CD_MAIN = FOURIER_DIM + 2 * GLA_K_DIM + 2 * GLA_V_DIM

LANES = 128
SUBLANES = 8
VMEM_LIMIT = 48 * 1024 * 1024
MOE_UP_VMEM_LIMIT = 56 * 1024 * 1024
MOE_BLOCK_ROWS = 256
ROW_BLOCK = 256
DENSE_TM = 1024
DENSE_TN = 512
GDN_PREP_HEADS = 4
GDN_SCAN_HEADS = 8


def _cparams(*sem):
    return pltpu.CompilerParams(dimension_semantics=sem, vmem_limit_bytes=VMEM_LIMIT)


def _dot(a, b):
    return jnp.dot(a, b, preferred_element_type=jnp.float32)


def _dot_nt(a, b):
    return lax.dot_general(a, b, (((1,), (1,)), ((), ())), preferred_element_type=jnp.float32)


def _dot_tn(a, b):
    return lax.dot_general(a, b, (((0,), (0,)), ((), ())), preferred_element_type=jnp.float32)


def _bf(x):
    return x.astype(jnp.bfloat16)


def _split3(x):
    hi = _bf(x)
    r1 = x - hi.astype(jnp.float32)
    mid = _bf(r1)
    lo = _bf(r1 - mid.astype(jnp.float32))
    return hi, mid, lo


def _mm_body(*refs, n_pairs, has_res):
    o_ref = refs[-1]
    acc = _dot(refs[0][...], refs[1][...])
    for p in range(1, n_pairs):
        acc = acc + _dot(refs[2 * p][...], refs[2 * p + 1][...])
    if has_res:
        res_ref, gate_ref = refs[2 * n_pairs], refs[2 * n_pairs + 1]
        acc = res_ref[...] + gate_ref[0] * acc
    o_ref[...] = acc.astype(o_ref.dtype)


def dense_matmul(pairs, out_dtype=jnp.float32, tm=DENSE_TM, tn=512, m=None, n=None, w_row_block=0, residual=None,
                 gates=None, gate_index=None, out_shape=None, out_index=None):
    m = pairs[0][0].shape[0] if m is None else m
    n = pairs[0][1].shape[1] if n is None else n
    tm = min(tm, m)
    tn = min(tn, n)
    assert m % tm == 0 and n % tn == 0, (m, n, tm, tn)
    in_specs, args = [], []
    for x, w in pairs:
        k = x.shape[1]
        assert w.shape[0] % k == 0
        in_specs += [pl.BlockSpec((tm, k), lambda i, j: (i, 0)),
                     pl.BlockSpec((k, tn), lambda i, j: (w_row_block, j))]
        args += [x, w]
    if residual is not None:
        in_specs += [pl.BlockSpec((tm, tn), lambda i, j: (i, j)),
                     pl.BlockSpec((1, 1, tn), lambda i, j: (gate_index(i), 0, j))]
        args += [residual, gates]
    return pl.pallas_call(
        functools.partial(_mm_body, n_pairs=len(pairs), has_res=residual is not None),
        out_shape=jax.ShapeDtypeStruct((m, n) if out_shape is None else out_shape, out_dtype),
        grid=(m // tm, n // tn),
        in_specs=in_specs,
        out_specs=pl.BlockSpec((tm, tn), (lambda i, j: (i, j)) if out_index is None else out_index),
        compiler_params=_cparams("parallel", "parallel"),
        name="dense_matmul",
    )(*args)


def _pad_cols(w, mult=LANES):
    pad = (-w.shape[1]) % mult
    return jnp.pad(w, ((0, 0), (0, pad))) if pad else w


def _norm_mod_body(x_ref, nw_ref, shift_ref, scale_ref, o_ref):
    x = x_ref[...]
    y = x * lax.rsqrt(jnp.mean(x * x, axis=-1, keepdims=True) + NORM_EPS) * nw_ref[...]
    o_ref[...] = (y * (1.0 + scale_ref[0]) + shift_ref[0]).astype(o_ref.dtype)


def norm_modulate(x, norm_w, shift, scale, row_index, m=None, tm=512, out_dtype=jnp.bfloat16):
    m = x.shape[0] if m is None else m
    d = x.shape[1]
    vec = pl.BlockSpec((1, 1, d), lambda i: (row_index(i), 0, 0))
    return pl.pallas_call(
        _norm_mod_body,
        out_shape=jax.ShapeDtypeStruct((m, d), out_dtype),
        grid=(m // tm,),
        in_specs=[pl.BlockSpec((tm, d), lambda i: (i, 0)), pl.BlockSpec((1, d), lambda i: (0, 0)), vec, vec],
        out_specs=pl.BlockSpec((tm, d), lambda i: (i, 0)),
        compiler_params=_cparams("parallel"),
        name="norm_modulate",
    )(x, norm_w.reshape(1, d), shift, scale)


def _gdn_gates_body(tail_ref, par_ref, o_ref, ot_ref, *, n_heads, chunk):
    t = tail_ref[...]
    rows = t.shape[0]
    lane = lax.broadcasted_iota(jnp.int32, t.shape, 1)
    a_row = par_ref[0:1, :]
    dtb_row = par_ref[1:2, :]
    beta = jax.nn.sigmoid(t)
    z = t + dtb_row
    g = a_row * (jnp.maximum(z, 0.0) + jnp.log1p(jnp.exp(-jnp.abs(z))))
    ri = lax.broadcasted_iota(jnp.int32, (rows, rows), 0)
    ci = lax.broadcasted_iota(jnp.int32, (rows, rows), 1)
    shift = chunk.bit_length() - 1
    same = (ri >> shift) == (ci >> shift)
    pre = _bf(jnp.where(same & (ci <= ri), 1.0, 0.0))
    suf = _bf(jnp.where(same & (ci >= ri), 1.0, 0.0))
    ones = _bf(jnp.where(same, 1.0, 0.0))
    parts = _split3(g)
    cf = _dot(pre, parts[0]) + _dot(pre, parts[1]) + _dot(pre, parts[2])
    cb = _dot(suf, parts[0]) + _dot(suf, parts[1]) + _dot(suf, parts[2])
    tot = _dot(ones, parts[0]) + _dot(ones, parts[1]) + _dot(ones, parts[2])
    main = jnp.where(lane < 2 * n_heads, beta,
                     jnp.where(lane < 3 * n_heads, cf, jnp.where(lane < 4 * n_heads, cb, 0.0)))
    o_ref[:, :LANES] = main
    o_ref[:, LANES:] = tot
    ot_ref[...] = main.T


def gdn_gates(proj, tail_col0, a_log, dt_bias, n_heads):
    t = proj.shape[0]
    assert tail_col0 % LANES == 0
    par = jnp.zeros((SUBLANES, LANES), jnp.float32)
    par = par.at[0, 2 * n_heads:4 * n_heads].set(-jnp.exp(a_log.astype(jnp.float32)).reshape(-1))
    par = par.at[1, 2 * n_heads:4 * n_heads].set(dt_bias.astype(jnp.float32).reshape(-1))
    return pl.pallas_call(
        functools.partial(_gdn_gates_body, n_heads=n_heads, chunk=GDN_CHUNK),
        out_shape=(jax.ShapeDtypeStruct((t, 2 * LANES), jnp.float32),
                   jax.ShapeDtypeStruct((t // ROW_BLOCK * LANES, ROW_BLOCK), jnp.float32)),
        grid=(t // ROW_BLOCK,),
        in_specs=[pl.BlockSpec((ROW_BLOCK, LANES), lambda i: (i, tail_col0 // LANES)),
                  pl.BlockSpec((SUBLANES, LANES), lambda i: (0, 0))],
        out_specs=(pl.BlockSpec((ROW_BLOCK, 2 * LANES), lambda i: (i, 0)),
                   pl.BlockSpec((LANES, ROW_BLOCK), lambda i: (i, 0))),
        compiler_params=_cparams("parallel"),
        name="gdn_gates",
    )(proj, par)


def _gdn_prep_body(q_ref, k_ref, v_ref, qp_ref, kp_ref, vp_ref, qn_ref, kn_ref, vn_ref,
                   wq_ref, wk_ref, wv_ref, gt_ref, gtt_ref,
                   u_ref, w_ref, qd_ref, ke_ref, at_ref, ge_ref, *, heads, n_heads, blocks_per_seq, n_lat_blocks):
    rb = pl.program_id(0)
    hg = pl.program_id(1)
    n = ROW_BLOCK
    c = GDN_CHUNK
    n_ch = n // c
    cs = c.bit_length() - 1
    hd = GDN_HEAD_DIM
    is_ctx = rb >= n_lat_blocks
    first = jnp.logical_or(is_ctx, rb % blocks_per_seq == 0)
    last = jnp.logical_or(is_ctx, rb % blocks_per_seq == blocks_per_seq - 1)
    m_prev = jnp.where(first, 0.0, 1.0)
    m_next = jnp.where(last, 0.0, 1.0)

    def conv_silu(cur_ref, prev_ref, next_ref, cw_ref):
        ext = jnp.concatenate([prev_ref[...] * m_prev, cur_ref[...], next_ref[...] * m_next], axis=0)
        acc = None
        for j in range(GDN_CONV):
            lo = SUBLANES - GDN_CONV // 2 + j
            term = ext[lo:lo + ROW_BLOCK, :] * cw_ref[j:j + 1, :]
            acc = term if acc is None else acc + term
        return acc * jax.nn.sigmoid(acc)

    qc = conv_silu(q_ref, qp_ref, qn_ref, wq_ref)
    kc = conv_silu(k_ref, kp_ref, kn_ref, wk_ref)
    vc = conv_silu(v_ref, vp_ref, vn_ref, wv_ref)

    gt_lo = gt_ref[:, :LANES]
    gt_hi = gt_ref[:, LANES:]
    lane = lax.broadcasted_iota(jnp.int32, (n, LANES), 1)
    blk16 = _bf(jnp.where((lax.broadcasted_iota(jnp.int32, (n, n), 0) >> cs)
                          == (lax.broadcasted_iota(jnp.int32, (n, n), 1) >> cs), 1.0, 0.0))
    wr = lax.broadcasted_iota(jnp.int32, (c, n), 0)
    wc = lax.broadcasted_iota(jnp.int32, (c, n), 1)
    eye_wide = jnp.where(wr == (wc & (c - 1)), 1.0, 0.0)
    sr = lax.broadcasted_iota(jnp.int32, (c, LANES), 0)
    sc = lax.broadcasted_iota(jnp.int32, (c, LANES), 1)
    slab_masks = []
    for odd in range(LANES // c):
        rel = sc - odd * c
        own = (sc >> cs) == odd
        slab_masks.append(((own & (rel <= sr), own & (rel < sr)), (own & (rel >= sr), own & (rel > sr))))

    def tall(wide):
        return jnp.concatenate([_bf(wide)] * n_ch, axis=0) * blk16

    def column(src, idx):
        return jnp.sum(jnp.where(lane == idx, src, 0.0), axis=1, keepdims=True)

    per_tile = LANES // c
    chains = []
    for gi in range(heads):
        h = hg * heads + gi
        sl = slice(gi * hd, (gi + 1) * hd)
        qh, kh, v = qc[:, sl], kc[:, sl], vc[:, sl]
        q = qh * lax.rsqrt(jnp.sum(qh * qh, axis=-1, keepdims=True) + NORM_EPS) * (hd ** -0.5)
        k = kh * lax.rsqrt(jnp.sum(kh * kh, axis=-1, keepdims=True) + NORM_EPS)
        k16 = _bf(k)
        qk = _dot_nt(_bf(q), k16)
        kk = _dot_nt(k16, k16)
        for d in range(2):
            beta = column(gt_lo, d * n_heads + h)
            gc = column(gt_lo, (2 + d) * n_heads + h)
            tot = column(gt_hi, (2 + d) * n_heads + h)
            gc_row = gtt_ref[pl.ds((2 + d) * n_heads + h, 1), :]
            p_tiles = []
            for j in range(n_ch):
                rows = slice(j * c, (j + 1) * c)
                cols = slice((j // per_tile) * LANES, (j // per_tile + 1) * LANES)
                incl, strict = slab_masks[j % per_tile][d]
                decay = jnp.where(incl, jnp.exp(jnp.where(incl, gc[rows] - gc_row[:, cols], 0.0)), 0.0)
                a = jnp.where(strict, (beta[rows] * kk[rows, cols]) * decay, 0.0)
                if j % per_tile == 0:
                    p_tiles.append(-a)
                else:
                    p_tiles[-1] = p_tiles[-1] - a
                lo = (j % per_tile) * c
                at_ref[d, rows, gi * c:(gi + 1) * c] = _bf((qk[rows, cols] * decay)[:, lo:lo + c])
                ge_ref[d, j * SUBLANES:(j + 1) * SUBLANES, sl] = jnp.broadcast_to(
                    jnp.exp(tot[j * c:j * c + SUBLANES]), (SUBLANES, hd))
            p = jnp.concatenate(p_tiles, axis=1)
            eg = jnp.exp(gc)
            kbeta = k * beta
            rhs = _bf(jnp.concatenate([v * beta, kbeta * eg], axis=1))
            qd_ref[d, :, sl] = _bf(q * eg)
            ke_ref[d, :, sl] = _bf(k * jnp.exp(tot - gc))
            chains.append([d, sl, p, eye_wide + p, tall(p), rhs])

    span = 2
    while span < c:
        for ch in chains:
            ch[2] = _dot(_bf(ch[2]), ch[4])
            ch[4] = tall(ch[2])
            ch[3] = ch[3] + _dot(_bf(ch[3]), ch[4])
        span *= 2
    for d, sl, _, t, _, rhs in chains:
        sol = _dot(tall(t), rhs)
        u_ref[d, :, sl] = sol[:, :hd]
        w_ref[d, :, sl] = _bf(sol[:, hd:])


def gdn_prepare(proj, gates, gates_t, conv_w, n_heads, qkv_col0, blocks_per_seq, n_lat_blocks):
    t = proj.shape[0]
    hd = GDN_HEAD_DIM
    g = GDN_PREP_HEADS
    gw = g * hd
    dim = n_heads * hd
    nb = t // ROW_BLOCK
    cb0 = qkv_col0 // gw
    per = dim // gw
    halo_per_block = ROW_BLOCK // SUBLANES

    def cur(off):
        return pl.BlockSpec((ROW_BLOCK, gw), lambda rb, hg: (rb, cb0 + off * per + hg))

    def prev(off):
        return pl.BlockSpec((SUBLANES, gw),
                            lambda rb, hg: (jnp.maximum(rb * halo_per_block - 1, 0), cb0 + off * per + hg))

    def nxt(off):
        return pl.BlockSpec((SUBLANES, gw),
                            lambda rb, hg: (jnp.minimum((rb + 1) * halo_per_block, nb * halo_per_block - 1),
                                            cb0 + off * per + hg))

    def cw(off):
        return pl.BlockSpec((GDN_CONV, gw), lambda rb, hg: (0, off * per + hg))

    out_tok = lambda width: pl.BlockSpec((2, ROW_BLOCK, width), lambda rb, hg: (0, rb, hg))
    outs = pl.pallas_call(
        functools.partial(_gdn_prep_body, heads=g, n_heads=n_heads, blocks_per_seq=blocks_per_seq,
                          n_lat_blocks=n_lat_blocks),
        out_shape=(jax.ShapeDtypeStruct((2, t, dim), jnp.float32),
                   jax.ShapeDtypeStruct((2, t, dim), jnp.bfloat16),
                   jax.ShapeDtypeStruct((2, t, dim), jnp.bfloat16),
                   jax.ShapeDtypeStruct((2, t, dim), jnp.bfloat16),
                   jax.ShapeDtypeStruct((2, t, n_heads * GDN_CHUNK), jnp.bfloat16),
                   jax.ShapeDtypeStruct((2, t // SUBLANES, dim), jnp.float32)),
        grid=(nb, n_heads // g),
        in_specs=[cur(0), cur(1), cur(2), prev(0), prev(1), prev(2), nxt(0), nxt(1), nxt(2),
                  cw(0), cw(1), cw(2), pl.BlockSpec((ROW_BLOCK, 2 * LANES), lambda rb, hg: (rb, 0)),
                  pl.BlockSpec((LANES, ROW_BLOCK), lambda rb, hg: (rb, 0))],
        out_specs=(out_tok(gw), out_tok(gw), out_tok(gw), out_tok(gw), out_tok(g * GDN_CHUNK),
                   pl.BlockSpec((2, ROW_BLOCK // SUBLANES, gw), lambda rb, hg: (0, rb, hg))),
        compiler_params=_cparams("parallel", "parallel"),
        name="gdn_prepare",
    )(proj, proj, proj, proj, proj, proj, proj, proj, proj, conv_w, conv_w, conv_w, gates, gates_t)
    return outs


def _gdn_scan_body(uf, wf, qf, kf, af, gf, ub, wb, qb, kb, ab, gb, of_ref, ob_ref, s_ref, *, heads):
    t = pl.program_id(2)
    c = GDN_CHUNK
    hd = GDN_HEAD_DIM
    n_ch = ROW_BLOCK // c

    @pl.when(t == 0)
    def _():
        s_ref[...] = jnp.zeros_like(s_ref)

    pw = 2 * hd
    s_mask = ((lax.broadcasted_iota(jnp.int32, (pw, pw), 0) >> (hd.bit_length() - 1))
              == (lax.broadcasted_iota(jnp.int32, (pw, pw), 1) >> (hd.bit_length() - 1)))
    v_mask = ((lax.broadcasted_iota(jnp.int32, (2 * c, pw), 0) >> (c.bit_length() - 1))
              == (lax.broadcasted_iota(jnp.int32, (2 * c, pw), 1) >> (hd.bit_length() - 1)))
    views = ((uf, wf, qf, kf, af, gf, of_ref), (ub, wb, qb, kb, ab, gb, ob_ref))
    chains = [(d, p) for d in range(2) for p in range(heads // 2)]
    for step in range(n_ch):
        def locate(d, p):
            ch = step if d == 0 else n_ch - 1 - step
            return views[d], ch, slice(ch * c, (ch + 1) * c), slice(p * pw, (p + 1) * pw)

        state, r_all, v_all = {}, {}, {}
        for d, p in chains:
            (u_r, w_r, q_r, k_r, a_r, g_r, o_r), ch, rows, sl = locate(d, p)
            s = s_ref[d, p]
            state[d, p] = s
            s_diag = _bf(jnp.where(s_mask, jnp.concatenate([s, s], axis=0), 0.0))
            r_all[d, p] = _dot(jnp.concatenate([w_r[0, rows, sl], q_r[0, rows, sl]], axis=0), s_diag)
        for d, p in chains:
            (u_r, w_r, q_r, k_r, a_r, g_r, o_r), ch, rows, sl = locate(d, p)
            v_new = u_r[0, rows, sl] - r_all[d, p][:c]
            v_all[d, p] = _bf(jnp.where(v_mask, jnp.concatenate([v_new, v_new], axis=0), 0.0))
        for d, p in chains:
            (u_r, w_r, q_r, k_r, a_r, g_r, o_r), ch, rows, sl = locate(d, p)
            o_r[0, rows, sl] = r_all[d, p][c:] + _dot(a_r[0, rows, p * 2 * c:(p + 1) * 2 * c], v_all[d, p])
        for d, p in chains:
            (u_r, w_r, q_r, k_r, a_r, g_r, o_r), ch, rows, sl = locate(d, p)
            k_stack = jnp.concatenate([k_r[0, rows, p * pw:p * pw + hd], k_r[0, rows, p * pw + hd:(p + 1) * pw]],
                                      axis=0)
            ge = g_r[0, ch * SUBLANES:ch * SUBLANES + 1, sl]
            s_ref[d, p] = state[d, p] * ge + _dot_tn(k_stack, v_all[d, p])


def gdn_scan(u, w, qd, ke, at, ge, n_heads, n_batch, blocks_per_seq):
    t = u.shape[1]
    hd = GDN_HEAD_DIM
    g = GDN_SCAN_HEADS
    gw = g * hd
    n_lat_blocks = n_batch * blocks_per_seq

    def fwd(b, s):
        return jnp.where(s == 0, n_lat_blocks + b, b * blocks_per_seq + s - 1)

    def bwd(b, s):
        return jnp.where(s == 0, n_lat_blocks + b, b * blocks_per_seq + blocks_per_seq - s)

    def tok(d, width, blk):
        return pl.BlockSpec((1, ROW_BLOCK, width), lambda b, hg, s: (d, blk(b, s), hg))

    def gsp(d, blk):
        return pl.BlockSpec((1, ROW_BLOCK // SUBLANES, gw), lambda b, hg, s: (d, blk(b, s), hg))

    in_specs = []
    for d, blk in ((0, fwd), (1, bwd)):
        in_specs += [tok(d, gw, blk), tok(d, gw, blk), tok(d, gw, blk), tok(d, gw, blk),
                     tok(d, g * GDN_CHUNK, blk), gsp(d, blk)]
    o_f, o_b = pl.pallas_call(
        functools.partial(_gdn_scan_body, heads=g),
        out_shape=(jax.ShapeDtypeStruct((1, t, n_heads * hd), jnp.float32),) * 2,
        grid=(n_batch, n_heads // g, blocks_per_seq + 1),
        in_specs=in_specs,
        out_specs=(pl.BlockSpec((1, ROW_BLOCK, gw), lambda b, hg, s: (0, fwd(b, s), hg)),
                   pl.BlockSpec((1, ROW_BLOCK, gw), lambda b, hg, s: (0, bwd(b, s), hg))),
        scratch_shapes=[pltpu.VMEM((2, g // 2, hd, 2 * hd), jnp.float32)],
        compiler_params=_cparams("parallel", "parallel", "arbitrary"),
        name="gdn_scan",
    )(u, w, qd, ke, at, ge, u, w, qd, ke, at, ge)
    return o_f[0], o_b[0]


GATED_NORM_COLS = 1024


def _gated_norm_body(of_ref, ob_ref, z_ref, nw_ref, lead_ref, y_ref, *, hd):
    j = pl.program_id(1)

    @pl.when(j == 0)
    def _():
        y_ref[...] = lead_ref[...]

    @pl.when(j > 0)
    def _():
        for h in range(GATED_NORM_COLS // hd):
            sl = slice(h * hd, (h + 1) * hd)
            o = of_ref[:, sl] + ob_ref[:, sl]
            y = o * lax.rsqrt(jnp.mean(o * o, axis=-1, keepdims=True) + NORM_EPS) * nw_ref[...]
            z = z_ref[:, sl]
            y_ref[:, sl] = (y * (z * jax.nn.sigmoid(z))).astype(y_ref.dtype)


def gated_head_norm(o_f, o_b, proj, z_col0, norm_w, hd, lead):
    m, cw = lead.shape
    dim = o_f.shape[1]
    assert cw == GATED_NORM_COLS and z_col0 % cw == 0 and dim % cw == 0 and cw % hd == 0

    def col(j):
        return jnp.maximum(j - 1, 0)

    return pl.pallas_call(
        functools.partial(_gated_norm_body, hd=hd),
        out_shape=jax.ShapeDtypeStruct((m, cw + dim), jnp.bfloat16),
        grid=(m // ROW_BLOCK, 1 + dim // cw),
        in_specs=[pl.BlockSpec((ROW_BLOCK, cw), lambda i, j: (i, col(j))),
                  pl.BlockSpec((ROW_BLOCK, cw), lambda i, j: (i, col(j))),
                  pl.BlockSpec((ROW_BLOCK, cw), lambda i, j: (i, z_col0 // cw + col(j))),
                  pl.BlockSpec((1, hd), lambda i, j: (0, 0)),
                  pl.BlockSpec((ROW_BLOCK, cw), lambda i, j: (i, 0))],
        out_specs=pl.BlockSpec((ROW_BLOCK, cw), lambda i, j: (i, j)),
        compiler_params=_cparams("parallel", "parallel"),
        name="gated_head_norm",
    )(o_f, o_b, proj, norm_w.reshape(1, hd), lead)


def _pool_body(u_ref, w_ref, sc_ref, y_ref, *, n_lat_blocks):
    rb = pl.program_id(0)
    is_ctx = rb >= n_lat_blocks
    row_len = jnp.where(is_ctx, CTX_LEN, GRID_W)
    shift = jnp.where(is_ctx, CTX_LEN.bit_length() - 1, GRID_W.bit_length() - 1)
    n = ROW_BLOCK
    ri = lax.broadcasted_iota(jnp.int32, (n, n), 0)
    ci = lax.broadcasted_iota(jnp.int32, (n, n), 1)
    same = (ri >> shift) == (ci >> shift)
    pos_r = ri & (row_len - 1)
    pos_c = ci & (row_len - 1)
    rcol = lax.broadcasted_iota(jnp.int32, (n, 1), 0) & (row_len - 1)
    gd = POOL_GROUP_DIM
    for g, win in enumerate(POOL_WINDOWS):
        half = win // 2
        lo = jnp.maximum(pos_r - half, 0)
        hi = jnp.minimum(pos_r + half - 1, row_len - 1)
        band = _bf(jnp.where(same & (pos_c >= lo) & (pos_c <= hi), 1.0, 0.0))
        cnt = (jnp.minimum(rcol + half - 1, row_len - 1) - jnp.maximum(rcol - half, 0) + 1).astype(jnp.float32)
        u = u_ref[:, g * gd:(g + 1) * gd]
        parts = _split3(u)
        win_sum = _dot(band, parts[0]) + _dot(band, parts[1]) + _dot(band, parts[2])
        dlt = win_sum / cnt - u
        y = _dot(_bf(dlt), _bf(w_ref[g])) * sc_ref[:, g * gd:(g + 1) * gd]
        y_ref[:, g * gd:(g + 1) * gd] = y.astype(y_ref.dtype)


def pool_mix(proj, pool_w, pool_scale, n_lat_blocks):
    t = proj.shape[0]
    return pl.pallas_call(
        functools.partial(_pool_body, n_lat_blocks=n_lat_blocks),
        out_shape=jax.ShapeDtypeStruct((t, POOL_DIM), jnp.bfloat16),
        grid=(t // ROW_BLOCK,),
        in_specs=[pl.BlockSpec((ROW_BLOCK, POOL_DIM), lambda i: (i, 0)),
                  pl.BlockSpec((POOL_GROUPS, POOL_GROUP_DIM, POOL_GROUP_DIM), lambda i: (0, 0, 0)),
                  pl.BlockSpec((1, POOL_DIM), lambda i: (0, 0))],
        out_specs=pl.BlockSpec((ROW_BLOCK, POOL_DIM), lambda i: (i, 0)),
        compiler_params=_cparams("parallel"),
        name="pool_mix",
    )(proj, pool_w, pool_scale.reshape(1, POOL_DIM))


def _moe_up_body(be_ref, first_ref, nused_ref, src_ref, src_next_ref, f_ref, w1_ref, w3_ref, h_ref,
                 w1_bf, w3_bf, rows_ref, sem):
    del be_ref
    i = pl.program_id(0)
    bm = rows_ref.shape[1]
    n_used = nused_ref[0]

    def row_copy(slot, r, src_row):
        return pltpu.make_async_copy(f_ref.at[pl.ds(src_row, 1), :], rows_ref.at[slot, pl.ds(r, 1), :], sem.at[slot])

    def issue(slot, idx_ref):
        for r in range(bm):
            row_copy(slot, r, idx_ref[0, 0, r]).start(priority=r % 2)

    @pl.when(jnp.logical_and(i == 0, n_used > 0))
    def _():
        issue(0, src_ref)

    @pl.when(i + 1 < n_used)
    def _():
        issue(lax.rem(i + 1, 2), src_next_ref)

    @pl.when(first_ref[i] == 1)
    def _():
        w1_bf[...] = _bf(w1_ref[0])
        w3_bf[...] = _bf(w3_ref[0])

    @pl.when(i < n_used)
    def _():
        slot = lax.rem(i, 2)

        def drain(r, carry):
            row_copy(slot, r, 0).wait()
            return carry
        lax.fori_loop(0, bm, drain, 0, unroll=8)
        x = _bf(rows_ref[slot])
        a = _dot(x, w1_bf[...])
        b = _dot(x, w3_bf[...])
        h_ref[...] = _bf(a * jax.nn.sigmoid(a) * b)

    @pl.when(i >= nused_ref[0])
    def _():
        h_ref[...] = jnp.zeros_like(h_ref)


def _moe_down_body(be_ref, first_ref, nused_ref, h_ref, w2_ref, o_ref, w2_bf):
    del be_ref
    i = pl.program_id(0)

    @pl.when(first_ref[i] == 1)
    def _():
        w2_bf[...] = _bf(w2_ref[0])

    @pl.when(i < nused_ref[0])
    def _():
        o_ref[...] = _dot(h_ref[...], w2_bf[...]).astype(o_ref.dtype)

    @pl.when(i >= nused_ref[0])
    def _():
        o_ref[...] = jnp.zeros_like(o_ref)


def moe_expert_blocks(f, src, block_expert, block_first, n_used, w1, w3, w2):
    d = f.shape[1]
    bm = MOE_BLOCK_ROWS
    nb = src.shape[0] // bm
    p = nb * bm
    hid = w1.shape[-1]
    src_blocks = src.reshape(nb, 1, bm)
    up_spec = pltpu.PrefetchScalarGridSpec(
        num_scalar_prefetch=3,
        grid=(nb,),
        in_specs=[pl.BlockSpec((1, 1, bm), lambda i, be, fi, nu: (i, 0, 0), memory_space=pltpu.SMEM),
                  pl.BlockSpec((1, 1, bm), lambda i, be, fi, nu: (jnp.minimum(i + 1, nb - 1), 0, 0),
                               memory_space=pltpu.SMEM),
                  pl.BlockSpec(memory_space=pl.ANY),
                  pl.BlockSpec((1, d, hid), lambda i, be, fi, nu: (be[i], 0, 0)),
                  pl.BlockSpec((1, d, hid), lambda i, be, fi, nu: (be[i], 0, 0))],
        out_specs=pl.BlockSpec((bm, hid), lambda i, be, fi, nu: (i, 0)),
        scratch_shapes=[pltpu.VMEM((d, hid), jnp.bfloat16), pltpu.VMEM((d, hid), jnp.bfloat16),
                        pltpu.VMEM((2, bm, d), jnp.float32), pltpu.SemaphoreType.DMA((2,))],
    )
    hbuf = pl.pallas_call(
        _moe_up_body,
        out_shape=jax.ShapeDtypeStruct((p, hid), jnp.bfloat16),
        grid_spec=up_spec,
        compiler_params=pltpu.CompilerParams(dimension_semantics=("arbitrary",), vmem_limit_bytes=MOE_UP_VMEM_LIMIT),
        name="moe_up",
    )(block_expert, block_first, n_used, src_blocks, src_blocks, f, w1, w3)
    down_spec = pltpu.PrefetchScalarGridSpec(
        num_scalar_prefetch=3,
        grid=(nb,),
        in_specs=[pl.BlockSpec((bm, hid), lambda i, be, fi, nu: (i, 0)),
                  pl.BlockSpec((1, hid, d), lambda i, be, fi, nu: (be[i], 0, 0))],
        out_specs=pl.BlockSpec((bm, d), lambda i, be, fi, nu: (i, 0)),
        scratch_shapes=[pltpu.VMEM((hid, d), jnp.bfloat16)],
    )
    return pl.pallas_call(
        _moe_down_body,
        out_shape=jax.ShapeDtypeStruct((p, d), jnp.float32),
        grid_spec=down_spec,
        compiler_params=_cparams("arbitrary"),
        name="moe_down",
    )(block_expert, block_first, n_used, hbuf, w2)


def _moe_route_body(h_ref, w_ref, b_ref, ids_ref, wgt_ref, cnt_ref, run_ref):
    i = pl.program_id(0)

    @pl.when(i == 0)
    def _():
        run_ref[...] = jnp.zeros_like(run_ref)

    n = h_ref.shape[0]
    ng, ne = MOE_GROUPS, MOE_EXPERTS_PER_GROUP
    h = _bf(h_ref[...])
    logits = _dot(h, w_ref[0]) + _dot(h, w_ref[1]) + _dot(h, w_ref[2]) + b_ref[...]
    lane = lax.broadcasted_iota(jnp.int32, logits.shape, 1)
    neg = jnp.float32(-jnp.inf)

    lane_f = lane.astype(jnp.float32)

    def first_argmax(vals):
        top = jnp.max(vals, axis=1, keepdims=True)
        return top, jnp.min(jnp.where(vals == top, lane_f, float(LANES)), axis=1, keepdims=True).astype(jnp.int32)

    gl = jnp.where(lane < ng, logits, neg)
    g_top, group = first_argmax(gl)
    p_group = 1.0 / jnp.sum(jnp.exp(gl - g_top), axis=1, keepdims=True)
    lo = ng + group * ne
    sel = jnp.where((lane >= lo) & (lane < lo + ne), logits, neg)
    v1, i1 = first_argmax(sel)
    v2, i2 = first_argmax(jnp.where(lane == i1, neg, sel))
    e21 = jnp.exp(v2 - v1)
    w1 = p_group / (1.0 + e21)
    w2 = p_group * e21 / (1.0 + e21)
    e1 = i1 - ng
    e2 = i2 - ng
    oh1 = jnp.where(lane == e1, 1.0, 0.0)
    oh2 = jnp.where(lane == e2, 1.0, 0.0)
    both = oh1 + oh2
    ri = lax.broadcasted_iota(jnp.int32, (n, n), 0)
    ci = lax.broadcasted_iota(jnp.int32, (n, n), 1)
    before = _dot(_bf(jnp.where(ci < ri, 1.0, 0.0)), _bf(both)) + run_ref[0:1, :]
    r1 = jnp.sum(oh1 * before, axis=1, keepdims=True)
    r2 = jnp.sum(oh2 * before, axis=1, keepdims=True)
    run_ref[...] = run_ref[...] + jnp.sum(both, axis=0, keepdims=True)
    ids_ref[...] = jnp.where(lane == 0, e1, jnp.where(lane == 1, e2, jnp.where(
        lane == 2, r1.astype(jnp.int32), jnp.where(lane == 3, r2.astype(jnp.int32), 0))))
    wgt_ref[...] = jnp.where(lane == 0, w1, jnp.where(lane == 1, w2, 0.0))
    cnt_ref[...] = run_ref[...].astype(jnp.int32)


def moe_route(h, group_w, group_b, expert_w, expert_b):
    t, d = h.shape
    w = jnp.concatenate([group_w, expert_w.transpose(1, 0, 2).reshape(d, MOE_EXPERTS)], axis=1).astype(jnp.float32)
    w = jnp.stack(_split3(_pad_cols(w)))
    b = _pad_cols(jnp.concatenate([group_b, expert_b.reshape(-1)])[None, :].astype(jnp.float32))
    tm = ROW_BLOCK
    return pl.pallas_call(
        _moe_route_body,
        out_shape=(jax.ShapeDtypeStruct((t, LANES), jnp.int32), jax.ShapeDtypeStruct((t, LANES), jnp.float32),
                   jax.ShapeDtypeStruct((SUBLANES, LANES), jnp.int32)),
        grid=(t // tm,),
        in_specs=[pl.BlockSpec((tm, d), lambda i: (i, 0)), pl.BlockSpec((3, d, LANES), lambda i: (0, 0, 0)),
                  pl.BlockSpec((1, LANES), lambda i: (0, 0))],
        out_specs=(pl.BlockSpec((tm, LANES), lambda i: (i, 0)), pl.BlockSpec((tm, LANES), lambda i: (i, 0)),
                   pl.BlockSpec((SUBLANES, LANES), lambda i: (0, 0))),
        scratch_shapes=[pltpu.VMEM((SUBLANES, LANES), jnp.float32)],
        compiler_params=_cparams("arbitrary"),
        name="moe_route",
    )(h, w, b)


def hier_moe(h, group_w, group_b, expert_w, expert_b, w1, w3, w2, expert0):
    t, d = h.shape
    ids, weight, counts = moe_route(h, group_w, group_b, expert_w, expert_b)
    expert = ids[:, :MOE_TOP_K].T.reshape(-1)
    rank = ids[:, MOE_TOP_K:2 * MOE_TOP_K].T.reshape(-1)
    counts = counts[0, :MOE_EXPERTS]
    tk = t * MOE_TOP_K
    bm = MOE_BLOCK_ROWS
    padded = (counts + bm - 1) // bm * bm
    e_idx = jnp.arange(MOE_EXPERTS)
    pad_end = jnp.sum(jnp.where(e_idx[None, :] <= e_idx[:, None], padded[None, :], 0), axis=1)
    pad_start = pad_end - padded
    dest = pad_start[expert] + rank
    n_blocks = -(-tk // bm) + MOE_EXPERTS
    token = jnp.tile(jnp.arange(t, dtype=jnp.int32), MOE_TOP_K)
    src = jnp.zeros((n_blocks * bm,), jnp.int32).at[dest].set(token)
    block_start = jnp.arange(n_blocks, dtype=jnp.int32) * bm
    block_expert = jnp.minimum(jnp.sum((pad_end[None, :] <= block_start[:, None]).astype(jnp.int32), axis=1),
                               MOE_EXPERTS - 1)
    block_first = jnp.concatenate([jnp.ones((1,), jnp.int32),
                                   (block_expert[1:] != block_expert[:-1]).astype(jnp.int32)])
    n_used = (pad_end[-1:] // bm).astype(jnp.int32)
    ybuf = moe_expert_blocks(h, src, block_expert + expert0, block_first, n_used, w1, w3, w2)
    return ybuf, dest.astype(jnp.int32), weight


GLA_SUB = 32
GLA_STEP = 64


def _gla_prep_body(q_ref, k_ref, v_ref, lr_ref, wg_ref, gb_ref, qd_ref, ke_ref, at_ref, ge_ref, v16_ref):
    n = ROW_BLOCK
    c = GLA_STEP
    sb = GLA_SUB
    n_sb = n // sb
    q = q_ref[...] * (GLA_HEAD_K ** -0.5)
    k = k_ref[...]
    v16_ref[...] = _bf(v_ref[...])
    lr = lr_ref[...]
    ri = lax.broadcasted_iota(jnp.int32, (n, n), 0)
    ci = lax.broadcasted_iota(jnp.int32, (n, n), 1)
    cs = c.bit_length() - 1
    ss = sb.bit_length() - 1
    same_c = (ri >> cs) == (ci >> cs)
    same_s = (ri >> ss) == (ci >> ss)
    dirs = range(2)
    xs = [jnp.dot(lr, wg_ref[d], preferred_element_type=jnp.float32, precision=lax.Precision.HIGHEST) + gb_ref[d]
          for d in dirs]
    gks = [_split3((jnp.minimum(x, 0.0) - jnp.log1p(jnp.exp(-jnp.abs(x)))) * (1.0 / GLA_GATE_TEMP)) for x in xs]
    runs = [_bf(jnp.where(same_c & ((ci <= ri) if d == 0 else (ci >= ri)), 1.0, 0.0)) for d in dirs]
    gs = [_dot(runs[d], gks[d][0]) + _dot(runs[d], gks[d][1]) + _dot(runs[d], gks[d][2]) for d in dirs]
    locs, tots, totc = [], [], []
    for d in dirs:
        g = gs[d]
        off, tot_s, tot_c = [], [], []
        for b in range(n_sb):
            lo = b * sb
            inner = (b % 2 == 1) if d == 0 else (b % 2 == 0)
            edge = g[lo + sb - 1:lo + sb] if d == 0 else g[lo:lo + 1]
            if inner:
                prev = g[lo - 1:lo] if d == 0 else g[lo + sb:lo + sb + 1]
                off.append(jnp.broadcast_to(prev, (sb, prev.shape[1])))
                tot_s.append(jnp.broadcast_to(edge - prev, (sb, prev.shape[1])))
            else:
                off.append(jnp.zeros((sb, g.shape[1]), jnp.float32))
                tot_s.append(jnp.broadcast_to(edge, (sb, edge.shape[1])))
        for j in range(n // c):
            edge = g[j * c + c - 1:j * c + c] if d == 0 else g[j * c:j * c + 1]
            tot_c.append(jnp.broadcast_to(edge, (c, edge.shape[1])))
        locs.append(g - jnp.concatenate(off, axis=0))
        tots.append(jnp.concatenate(tot_s, axis=0))
        totc.append(jnp.concatenate(tot_c, axis=0))
    qd_loc = [_bf(q * jnp.exp(locs[d])) for d in dirs]
    m_diag = [_dot_nt(qd_loc[d], _bf(k * jnp.exp(-locs[d]))) for d in dirs]
    m_cross = [_dot_nt(qd_loc[d], _bf(k * jnp.exp(tots[d] - locs[d]))) for d in dirs]
    for d in dirs:
        tri = (ci <= ri) if d == 0 else (ci >= ri)
        nxt = ((ri >> ss) == (ci >> ss) + 1) if d == 0 else ((ri >> ss) + 1 == (ci >> ss))
        attn = jnp.where(same_s & tri, m_diag[d], 0.0) + jnp.where(same_c & nxt, m_cross[d], 0.0)
        qd_ref[d] = _bf(q * jnp.exp(gs[d]))
        ke_ref[d] = _bf(k * jnp.exp(totc[d] - gs[d]))
        for j in range(n // c):
            at_ref[d, 0, j * c:(j + 1) * c, :] = _bf(attn[j * c:(j + 1) * c, j * c:(j + 1) * c])
            ge_ref[d, j * SUBLANES:(j + 1) * SUBLANES, :] = jnp.exp(totc[d][j * c:j * c + SUBLANES])


def gla_prepare(proj, gate_up, gate_b):
    t = proj.shape[0]
    assert CD_MAIN % LANES == 0
    hk, hv = GLA_HEAD_K, GLA_HEAD_V
    wg = jnp.zeros((2, LANES, GLA_K_DIM), jnp.float32)
    for d in range(2):
        wg = wg.at[d, d * GLA_GATE_RANK:(d + 1) * GLA_GATE_RANK].set(gate_up[d].astype(jnp.float32))
    gb = gate_b.astype(jnp.float32).reshape(2, 1, GLA_K_DIM)
    q0, k0, v0 = CD_SPLITS[0] // hk, CD_SPLITS[1] // hk, CD_SPLITS[2] // hv
    assert CD_SPLITS[0] % hk == 0 and CD_SPLITS[1] % hk == 0 and CD_SPLITS[2] % hv == 0
    return pl.pallas_call(
        _gla_prep_body,
        out_shape=(jax.ShapeDtypeStruct((2, t, GLA_K_DIM), jnp.bfloat16),
                   jax.ShapeDtypeStruct((2, t, GLA_K_DIM), jnp.bfloat16),
                   jax.ShapeDtypeStruct((2, GLA_HEADS, t, GLA_STEP), jnp.bfloat16),
                   jax.ShapeDtypeStruct((2, t // SUBLANES, GLA_K_DIM), jnp.float32),
                   jax.ShapeDtypeStruct((t, GLA_V_DIM), jnp.bfloat16)),
        grid=(t // ROW_BLOCK, GLA_HEADS),
        in_specs=[pl.BlockSpec((ROW_BLOCK, hk), lambda rb, h: (rb, q0 + h)),
                  pl.BlockSpec((ROW_BLOCK, hk), lambda rb, h: (rb, k0 + h)),
                  pl.BlockSpec((ROW_BLOCK, hv), lambda rb, h: (rb, v0 + h)),
                  pl.BlockSpec((ROW_BLOCK, LANES), lambda rb, h: (rb, CD_MAIN // LANES)),
                  pl.BlockSpec((2, LANES, hk), lambda rb, h: (0, 0, h)),
                  pl.BlockSpec((2, 1, hk), lambda rb, h: (0, 0, h))],
        out_specs=(pl.BlockSpec((2, ROW_BLOCK, hk), lambda rb, h: (0, rb, h)),
                   pl.BlockSpec((2, ROW_BLOCK, hk), lambda rb, h: (0, rb, h)),
                   pl.BlockSpec((2, 1, ROW_BLOCK, GLA_STEP), lambda rb, h: (0, h, rb, 0)),
                   pl.BlockSpec((2, ROW_BLOCK // SUBLANES, hk), lambda rb, h: (0, rb, h)),
                   pl.BlockSpec((ROW_BLOCK, hv), lambda rb, h: (rb, h))),
        compiler_params=_cparams("parallel", "parallel"),
        name="gla_prepare",
    )(proj, proj, proj, proj, wg, gb)


def _gla_scan_body(qf, kf, af, gf, vf, qb, kb, ab, gb, vb, of_ref, ob_ref, s_ref):
    t = pl.program_id(2)
    c = GLA_STEP
    n_ch = ROW_BLOCK // c

    @pl.when(t == 0)
    def _():
        s_ref[...] = jnp.zeros_like(s_ref)

    views = ((qf, kf, af, gf, vf, of_ref), (qb, kb, ab, gb, vb, ob_ref))
    for step in range(n_ch):
        chunk = [step, n_ch - 1 - step]
        rows = [slice(ch * c, (ch + 1) * c) for ch in chunk]
        st = [s_ref[d] for d in range(2)]
        inter = [_dot_nt(views[d][0][0, rows[d], :], _bf(st[d])) for d in range(2)]
        intra = [_dot(views[d][2][0, 0, rows[d], :], views[d][4][rows[d], :]) for d in range(2)]
        upd = [_dot_tn(views[d][4][rows[d], :], views[d][1][0, rows[d], :]) for d in range(2)]
        for d in range(2):
            views[d][5][rows[d], :] = inter[d] + intra[d]
            ge = views[d][3][0, chunk[d] * SUBLANES:chunk[d] * SUBLANES + 1, :]
            s_ref[d] = st[d] * ge + upd[d]


def gla_scan(qd, ke, at, ge, v16, n_batch, blocks_per_seq):
    hk, hv = GLA_HEAD_K, GLA_HEAD_V
    t = v16.shape[0]
    n_lat_blocks = n_batch * blocks_per_seq

    def fwd(b, s):
        return jnp.where(s == 0, n_lat_blocks + b, b * blocks_per_seq + s - 1)

    def bwd(b, s):
        return jnp.where(s == 0, n_lat_blocks + b, b * blocks_per_seq + blocks_per_seq - s)

    def out_blk(blk):
        return lambda b, h, s: (blk(b, s), h)

    def key_spec(d, blk, rows):
        return pl.BlockSpec((1, rows, hk), lambda b, h, s: (d, blk(b, s), h))

    def attn_spec(d, blk):
        return pl.BlockSpec((1, 1, ROW_BLOCK, GLA_STEP), lambda b, h, s: (d, h, blk(b, s), 0))

    def val_spec(blk):
        return pl.BlockSpec((ROW_BLOCK, hv), lambda b, h, s: (blk(b, s), h))

    in_specs = []
    for d, blk in ((0, fwd), (1, bwd)):
        in_specs += [key_spec(d, blk, ROW_BLOCK), key_spec(d, blk, ROW_BLOCK), attn_spec(d, blk),
                     key_spec(d, blk, ROW_BLOCK // SUBLANES), val_spec(blk)]
    o_f, o_b = pl.pallas_call(
        _gla_scan_body,
        out_shape=(jax.ShapeDtypeStruct((t, GLA_V_DIM), jnp.float32),) * 2,
        grid=(n_batch, GLA_HEADS, blocks_per_seq + 1),
        in_specs=in_specs,
        out_specs=(pl.BlockSpec((ROW_BLOCK, hv), out_blk(fwd)), pl.BlockSpec((ROW_BLOCK, hv), out_blk(bwd))),
        scratch_shapes=[pltpu.VMEM((2, hv, hk), jnp.float32)],
        compiler_params=_cparams("parallel", "parallel", "arbitrary"),
        name="gla_scan",
    )(qd, ke, at, ge, v16, qd, ke, at, ge, v16)
    return o_f, o_b


def _fourier_in_body(x_ref, w_ref, o_ref):
    y = _dot(_bf(x_ref[...]), w_ref[0])
    half = y.shape[1] // 2
    o_ref[0] = _bf(y[:, :half])
    o_ref[1] = _bf(y[:, half:])


def dft_tables(n, nc):
    r = 64
    assert n % r == 0
    kk = jnp.arange(n, dtype=jnp.int32)[None, :]
    j1 = jnp.arange(n // r, dtype=jnp.int32)[:, None]
    j0 = jnp.arange(r, dtype=jnp.int32)[:, None]
    a = ((j1 * kk) % (n // r)).astype(jnp.float32) * (2.0 * jnp.pi / (n // r))
    b = ((j0 * kk) % n).astype(jnp.float32) * (2.0 * jnp.pi / n)
    ca, sa, cb, sb = jnp.cos(a)[:, None, :], jnp.sin(a)[:, None, :], jnp.cos(b)[None, :, :], jnp.sin(b)[None, :, :]
    scale = n ** -0.5
    cos_n = ((ca * cb - sa * sb) * scale).reshape(n, n)
    sin_n = ((sa * cb + ca * sb) * scale).reshape(n, n)
    table = jnp.concatenate([cos_n, -sin_n], axis=1).astype(jnp.bfloat16)
    cc = jnp.arange(nc, dtype=jnp.int32)
    ang = ((cc[:, None] * cc[None, :]) % nc).astype(jnp.float32) * (2.0 * jnp.pi / nc)
    return table, jnp.cos(ang) * nc ** -0.5, jnp.sin(ang) * nc ** -0.5


def fourier_mix(proj, fourier_w, n_batch, n_lat):
    gd = FOURIER_GROUP_DIM
    ng = FOURIER_GROUPS
    table, cos_c, sin_c = dft_tables(n_lat, gd)
    w_all = fourier_w.astype(jnp.float32).transpose(1, 0, 2).reshape(gd, ng * gd)
    folded = dense_matmul([(jnp.concatenate([cos_c, sin_c], axis=0), w_all)], tm=2 * gd, tn=512)
    w2 = jnp.concatenate([folded[:gd].reshape(gd, ng, gd), folded[gd:].reshape(gd, ng, gd)], axis=2)
    w2 = w2.transpose(1, 0, 2).astype(jnp.bfloat16)
    tm = min(DENSE_TM, n_lat)
    tiles = n_lat // tm
    z = pl.pallas_call(
        _fourier_in_body,
        out_shape=jax.ShapeDtypeStruct((2, n_lat, n_batch * ng * gd), jnp.bfloat16),
        grid=(n_batch, ng, tiles),
        in_specs=[pl.BlockSpec((tm, gd), lambda b, g, i: (b * tiles + i, g)),
                  pl.BlockSpec((1, gd, 2 * gd), lambda b, g, i: (g, 0, 0))],
        out_specs=pl.BlockSpec((2, tm, gd), lambda b, g, i: (0, i, b * ng + g)),
        compiler_params=_cparams("parallel", "parallel", "parallel"),
        name="fourier_in",
    )(proj, w2)
    z = z.reshape(2 * n_lat, n_batch * ng * gd)
    tn = 512
    per_b = ng * gd // tn
    return dense_matmul([(table, z)], out_dtype=jnp.bfloat16, tm=512, tn=tn,
                        out_shape=(n_batch * n_lat, ng * gd),
                        out_index=lambda i, j: ((j // per_b) * (n_lat // 512) + i, j % per_b))


def modulation(c, c_ctx, mod_w, mod_b, layer):
    b, d = c.shape
    rows = jnp.concatenate([c, c_ctx[None, :], jnp.zeros((SUBLANES - b - 1, d), c.dtype)], axis=0)
    out = dense_matmul([(jax.nn.silu(rows), mod_w.reshape(-1, mod_w.shape[-1]))], tm=SUBLANES, tn=512,
                       w_row_block=layer) + mod_b[layer]
    return tuple(m[:, None, :] for m in jnp.split(out, N_MOD, axis=-1))


COMBINE_COLS = 512


def _combine_body(dest_ref, dest_next_ref, x_ref, ybuf_ref, w_ref, gate_ref, nw_ref, o_ref, rows_ref, sem, *,
                  final_norm):
    i = pl.program_id(0)
    n_tiles = pl.num_programs(0)
    tm, d = x_ref.shape

    def row_copy(slot, r, k, src_row):
        return pltpu.make_async_copy(ybuf_ref.at[pl.ds(src_row, 1), :], rows_ref.at[slot, k, pl.ds(r, 1), :],
                                     sem.at[slot])

    def issue(slot, idx_ref):
        def body(r, carry):
            for k in range(MOE_TOP_K):
                row_copy(slot, r, k, idx_ref[0, 0, k * tm + r]).start()
            return carry
        lax.fori_loop(0, tm, body, 0, unroll=8)

    @pl.when(i == 0)
    def _():
        issue(0, dest_ref)

    @pl.when(i + 1 < n_tiles)
    def _():
        issue(lax.rem(i + 1, 2), dest_next_ref)

    slot = lax.rem(i, 2)

    def drain(r, carry):
        for k in range(MOE_TOP_K):
            row_copy(slot, r, k, 0).wait()
        return carry
    lax.fori_loop(0, tm, drain, 0, unroll=8)

    w0 = w_ref[:, 0:1]
    w1 = w_ref[:, 1:2]
    sq = jnp.zeros((tm, 1), jnp.float32)
    for c0 in range(0, d, COMBINE_COLS):
        cols = slice(c0, c0 + COMBINE_COLS)
        y = w0 * rows_ref[slot, 0, :, cols] + w1 * rows_ref[slot, 1, :, cols]
        x = x_ref[:, cols] + gate_ref[0, :, cols] * y
        o_ref[:, cols] = x
        if final_norm:
            sq = sq + jnp.sum(x * x, axis=-1, keepdims=True)
    if final_norm:
        inv = lax.rsqrt(sq * (1.0 / d) + NORM_EPS)
        for c0 in range(0, d, COMBINE_COLS):
            cols = slice(c0, c0 + COMBINE_COLS)
            o_ref[:, cols] = o_ref[:, cols] * inv * nw_ref[:, cols]


def moe_combine(x, ybuf, dest, weight, gates, gate_index, final_norm_w=None):
    m, d = dest.shape[0] // MOE_TOP_K, x.shape[1]
    tm = ROW_BLOCK
    dest_tiles = dest.reshape(MOE_TOP_K, m // tm, tm).transpose(1, 0, 2).reshape(m // tm, 1, MOE_TOP_K * tm)
    nw = jnp.ones((1, d), jnp.float32) if final_norm_w is None else final_norm_w.reshape(1, d).astype(jnp.float32)
    return pl.pallas_call(
        functools.partial(_combine_body, final_norm=final_norm_w is not None),
        out_shape=jax.ShapeDtypeStruct((m, d), jnp.float32),
        grid=(m // tm,),
        in_specs=[pl.BlockSpec((1, 1, MOE_TOP_K * tm), lambda i: (i, 0, 0), memory_space=pltpu.SMEM),
                  pl.BlockSpec((1, 1, MOE_TOP_K * tm), lambda i: (jnp.minimum(i + 1, m // tm - 1), 0, 0),
                               memory_space=pltpu.SMEM),
                  pl.BlockSpec((tm, d), lambda i: (i, 0)),
                  pl.BlockSpec(memory_space=pl.ANY),
                  pl.BlockSpec((tm, LANES), lambda i: (i, 0)),
                  pl.BlockSpec((1, 1, d), lambda i: (gate_index(i), 0, 0)),
                  pl.BlockSpec((1, d), lambda i: (0, 0))],
        out_specs=pl.BlockSpec((tm, d), lambda i: (i, 0)),
        scratch_shapes=[pltpu.VMEM((2, MOE_TOP_K, tm, d), jnp.float32), pltpu.SemaphoreType.DMA((2,))],
        compiler_params=_cparams("arbitrary"),
        name="moe_combine",
    )(dest_tiles, dest_tiles, x, ybuf, weight, gates, nw)


def kernel(x, c, ctx, c_ctx, mod_w, mod_b, norm1_w, norm2_w, ab_w_in, pool_w, pool_scale, gdn_conv_w,
           gdn_a_log, gdn_dt_bias, gdn_norm_w, ab_w_out, cd_w_in, fourier_w, gla_gate_up, gla_gate_b,
           gla_norm_w, cd_w_out, moe_group_w, moe_group_b, moe_expert_w, moe_expert_b, moe_w1, moe_w3,
           moe_w2, final_norm_w):
    assert DEPTH == 2
    n_batch, n_lat, d = x.shape
    n_ctx = ctx.shape[1]
    assert n_ctx == ROW_BLOCK and n_lat % DENSE_TM == 0 and (n_batch * n_ctx) % DENSE_TM == 0
    t_lat = n_batch * n_lat
    blocks_per_seq = n_lat // ROW_BLOCK
    n_lat_blocks = n_batch * blocks_per_seq
    xs = jnp.concatenate([x.reshape(t_lat, d), ctx.reshape(n_batch * n_ctx, d)], axis=0)

    def mod_row(rows_per_tile):
        tiles_per_seq = n_lat // rows_per_tile
        return lambda i: jnp.minimum(i // tiles_per_seq, n_batch)

    bf = jnp.bfloat16
    w1_all, w3_all, w2_all = (w.reshape((-1,) + w.shape[2:]) for w in (moe_w1, moe_w3, moe_w2))
    sh1, sc1, g1, sh2, sc2, g2 = modulation(c, c_ctx, mod_w, mod_b, 0)
    h = norm_modulate(xs, norm1_w[0], sh1, sc1, mod_row(512))
    proj = dense_matmul([(h, _pad_cols(ab_w_in[0].astype(bf), DENSE_TN))])
    gates, gates_t = gdn_gates(proj, AB_MAIN, gdn_a_log[0], gdn_dt_bias[0], GDN_HEADS)
    u, w, qd, ke, at, ge = gdn_prepare(proj, gates, gates_t, gdn_conv_w[0], GDN_HEADS, POOL_DIM, blocks_per_seq,
                                       n_lat_blocks)
    o_f, o_b = gdn_scan(u, w, qd, ke, at, ge, GDN_HEADS, n_batch, blocks_per_seq)
    y = gated_head_norm(o_f, o_b, proj, POOL_DIM + 3 * GDN_DIM, gdn_norm_w[0], GDN_HEAD_DIM,
                        pool_mix(proj, pool_w[0], pool_scale[0], n_lat_blocks))
    xs = dense_matmul([(y, ab_w_out[0].astype(bf))], residual=xs, gates=g1, gate_index=mod_row(DENSE_TM))
    f = norm_modulate(xs, norm2_w[0], sh2, sc2, mod_row(512), out_dtype=jnp.float32)
    ybuf, dest, wgt = hier_moe(f, moe_group_w[0], moe_group_b[0], moe_expert_w[0], moe_expert_b[0], w1_all, w3_all,
                               w2_all, 0)
    xs = moe_combine(xs, ybuf, dest, wgt, g2, mod_row(ROW_BLOCK))

    sh1, sc1, g1, sh2, sc2, g2 = modulation(c, c_ctx, mod_w, mod_b, 1)
    h = norm_modulate(xs, norm1_w[1], sh1, sc1, mod_row(512))
    proj = dense_matmul([(h, _pad_cols(cd_w_in[0].astype(bf), DENSE_TN))])
    qd, ke, at, ge, v16 = gla_prepare(proj, gla_gate_up[0], gla_gate_b[0])
    o_f, o_b = gla_scan(qd, ke, at, ge, v16, n_batch, blocks_per_seq)
    y = gated_head_norm(o_f, o_b, proj, CD_SPLITS[3], gla_norm_w[0], GLA_HEAD_V,
                        fourier_mix(proj, fourier_w[0], n_batch, n_lat))
    xl = dense_matmul([(y, cd_w_out[0].astype(bf))], residual=xs, gates=g1, gate_index=mod_row(DENSE_TM))
    f = norm_modulate(xl, norm2_w[1], sh2, sc2, mod_row(512), out_dtype=jnp.float32)
    ybuf, dest, wgt = hier_moe(f, moe_group_w[1], moe_group_b[1], moe_expert_w[1], moe_expert_b[1], w1_all, w3_all,
                               w2_all, MOE_EXPERTS)
    out = moe_combine(xl, ybuf, dest, wgt, g2, mod_row(ROW_BLOCK), final_norm_w=final_norm_w)
    return out.reshape(n_batch, n_lat, d)
```

```python
import functools

import jax
import jax.numpy as jnp
from jax import lax
from jax.experimental import pallas as pl
from jax.experimental.pallas import tpu as pltpu

D_MODEL = 4096
DEPTH = 2
CTX_LEN = 256
GRID_W = 64
N_MOD = 6
NORM_EPS = 1e-6

POOL_WINDOWS = (2, 4, 8, 16)
POOL_GROUPS = 4
POOL_GROUP_DIM = D_MODEL // 16
POOL_DIM = POOL_GROUPS * POOL_GROUP_DIM

GDN_HEAD_DIM = 128
GDN_DIM = D_MODEL - POOL_DIM
GDN_HEADS = GDN_DIM // GDN_HEAD_DIM
GDN_CONV = 5
GDN_CHUNK = 64

FOURIER_GROUPS = 4
FOURIER_GROUP_DIM = D_MODEL // 16
FOURIER_DIM = FOURIER_GROUPS * FOURIER_GROUP_DIM

GLA_HEADS = 6
GLA_V_DIM = D_MODEL - FOURIER_DIM
GLA_K_DIM = GLA_V_DIM // 2
GLA_HEAD_K = GLA_K_DIM // GLA_HEADS
GLA_HEAD_V = GLA_V_DIM // GLA_HEADS
GLA_GATE_RANK = 16
GLA_GATE_TEMP = 16.0
GLA_CHUNK = 32

MOE_GROUPS = 4
MOE_EXPERTS_PER_GROUP = 8
MOE_EXPERTS = MOE_GROUPS * MOE_EXPERTS_PER_GROUP
MOE_TOP_K = 2
MOE_HIDDEN = D_MODEL // 8

AB_MAIN = POOL_DIM + 4 * GDN_DIM
CD_SPLITS = (FOURIER_DIM, FOURIER_DIM + GLA_K_DIM, FOURIER_DIM + 2 * GLA_K_DIM,
             FOURIER_DIM + 2 * GLA_K_DIM + GLA_V_DIM, FOURIER_DIM + 2 * GLA_K_DIM + 2 * GLA_V_DIM)
CD_MAIN = FOURIER_DIM + 2 * GLA_K_DIM + 2 * GLA_V_DIM

LANES = 128
SUBLANES = 8
VMEM_LIMIT = 48 * 1024 * 1024
MOE_UP_VMEM_LIMIT = 56 * 1024 * 1024
MOE_BLOCK_ROWS = 256
ROW_BLOCK = 256
DENSE_TM = 1024
DENSE_TN = 512
IN_PROJ_TN = 768
GDN_PREP_HEADS = 4
GDN_SCAN_HEADS = 8


def _cparams(*sem):
    return pltpu.CompilerParams(dimension_semantics=sem, vmem_limit_bytes=VMEM_LIMIT)


def _dot(a, b):
    return jnp.dot(a, b, preferred_element_type=jnp.float32)


def _dot_nt(a, b):
    return lax.dot_general(a, b, (((1,), (1,)), ((), ())), preferred_element_type=jnp.float32)


def _dot_tn(a, b):
    return lax.dot_general(a, b, (((0,), (0,)), ((), ())), preferred_element_type=jnp.float32)


def _bf(x):
    return x.astype(jnp.bfloat16)


def _split3(x):
    hi = _bf(x)
    r1 = x - hi.astype(jnp.float32)
    mid = _bf(r1)
    lo = _bf(r1 - mid.astype(jnp.float32))
    return hi, mid, lo


def _mm_body(*refs, n_pairs, has_res):
    o_ref = refs[-1]
    acc = _dot(refs[0][...], refs[1][...])
    for p in range(1, n_pairs):
        acc = acc + _dot(refs[2 * p][...], refs[2 * p + 1][...])
    if has_res:
        res_ref, gate_ref = refs[2 * n_pairs], refs[2 * n_pairs + 1]
        acc = res_ref[...] + gate_ref[0] * acc
    o_ref[...] = acc.astype(o_ref.dtype)


def dense_matmul(pairs, out_dtype=jnp.float32, tm=DENSE_TM, tn=DENSE_TN, m=None, n=None, w_row_block=0, residual=None,
                 gates=None, gate_index=None, out_shape=None, out_index=None):
    m = pairs[0][0].shape[0] if m is None else m
    n = pairs[0][1].shape[1] if n is None else n
    tm = min(tm, m)
    tn = min(tn, n)
    assert m % tm == 0 and n % tn == 0, (m, n, tm, tn)
    in_specs, args = [], []
    for x, w in pairs:
        k = x.shape[1]
        assert w.shape[0] % k == 0
        in_specs += [pl.BlockSpec((tm, k), lambda i, j: (i, 0)),
                     pl.BlockSpec((k, tn), lambda i, j: (w_row_block, j))]
        args += [x, w]
    if residual is not None:
        in_specs += [pl.BlockSpec((tm, tn), lambda i, j: (i, j)),
                     pl.BlockSpec((1, 1, tn), lambda i, j: (gate_index(i), 0, j))]
        args += [residual, gates]
    return pl.pallas_call(
        functools.partial(_mm_body, n_pairs=len(pairs), has_res=residual is not None),
        out_shape=jax.ShapeDtypeStruct((m, n) if out_shape is None else out_shape, out_dtype),
        grid=(m // tm, n // tn),
        in_specs=in_specs,
        out_specs=pl.BlockSpec((tm, tn), (lambda i, j: (i, j)) if out_index is None else out_index),
        compiler_params=_cparams("parallel", "parallel"),
        name="dense_matmul",
    )(*args)


def _pad_cols(w, mult=LANES):
    pad = (-w.shape[1]) % mult
    return jnp.pad(w, ((0, 0), (0, pad))) if pad else w


HI16 = 0xFFFF0000


def _pack_bf16_halves(y):
    half = y.shape[1] // 2
    u = pltpu.bitcast(_bf(y).astype(jnp.float32), jnp.uint32)
    return lax.shift_right_logical(u[:, :half], jnp.uint32(16)) | (u[:, half:] & jnp.uint32(HI16))


def _unpack_bf16_halves(u):
    lo = pltpu.bitcast(lax.shift_left(u, jnp.uint32(16)), jnp.float32)
    hi = pltpu.bitcast(u & jnp.uint32(HI16), jnp.float32)
    return _bf(lo), _bf(hi)


def _norm_mod_body(x_ref, nw_ref, shift_ref, scale_ref, o_ref, *, packed):
    x = x_ref[...]
    y = x * lax.rsqrt(jnp.mean(x * x, axis=-1, keepdims=True) + NORM_EPS) * nw_ref[...]
    y = y * (1.0 + scale_ref[0]) + shift_ref[0]
    o_ref[...] = _pack_bf16_halves(y) if packed else y.astype(o_ref.dtype)


def norm_modulate(x, norm_w, shift, scale, row_index, m=None, tm=512, packed=False):
    m = x.shape[0] if m is None else m
    d = x.shape[1]
    vec = pl.BlockSpec((1, 1, d), lambda i: (row_index(i), 0, 0))
    d_out = d // 2 if packed else d
    return pl.pallas_call(
        functools.partial(_norm_mod_body, packed=packed),
        out_shape=jax.ShapeDtypeStruct((m, d_out), jnp.uint32 if packed else jnp.bfloat16),
        grid=(m // tm,),
        in_specs=[pl.BlockSpec((tm, d), lambda i: (i, 0)), pl.BlockSpec((1, d), lambda i: (0, 0)), vec, vec],
        out_specs=pl.BlockSpec((tm, d_out), lambda i: (i, 0)),
        compiler_params=_cparams("parallel"),
        name="norm_modulate",
    )(x, norm_w.reshape(1, d), shift, scale)


def _gdn_gates_body(tail_ref, par_ref, o_ref, ot_ref, *, n_heads, chunk):
    t = tail_ref[...]
    rows = t.shape[0]
    lane = lax.broadcasted_iota(jnp.int32, t.shape, 1)
    a_row = par_ref[0:1, :]
    dtb_row = par_ref[1:2, :]
    beta = jax.nn.sigmoid(t)
    z = t + dtb_row
    g = a_row * (jnp.maximum(z, 0.0) + jnp.log1p(jnp.exp(-jnp.abs(z))))
    ri = lax.broadcasted_iota(jnp.int32, (rows, rows), 0)
    ci = lax.broadcasted_iota(jnp.int32, (rows, rows), 1)
    shift = chunk.bit_length() - 1
    same = (ri >> shift) == (ci >> shift)
    pre = _bf(jnp.where(same & (ci <= ri), 1.0, 0.0))
    suf = _bf(jnp.where(same & (ci >= ri), 1.0, 0.0))
    ones = _bf(jnp.where(same, 1.0, 0.0))
    parts = _split3(g)
    cf = _dot(pre, parts[0]) + _dot(pre, parts[1]) + _dot(pre, parts[2])
    cb = _dot(suf, parts[0]) + _dot(suf, parts[1]) + _dot(suf, parts[2])
    tot = _dot(ones, parts[0]) + _dot(ones, parts[1]) + _dot(ones, parts[2])
    main = jnp.where(lane < 2 * n_heads, beta,
                     jnp.where(lane < 3 * n_heads, cf, jnp.where(lane < 4 * n_heads, cb, 0.0)))
    o_ref[:, :LANES] = main
    o_ref[:, LANES:] = tot
    ot_ref[...] = main.T


def gdn_gates(proj, tail_col0, a_log, dt_bias, n_heads):
    t = proj.shape[0]
    assert tail_col0 % LANES == 0
    par = jnp.zeros((SUBLANES, LANES), jnp.float32)
    par = par.at[0, 2 * n_heads:4 * n_heads].set(-jnp.exp(a_log.astype(jnp.float32)).reshape(-1))
    par = par.at[1, 2 * n_heads:4 * n_heads].set(dt_bias.astype(jnp.float32).reshape(-1))
    return pl.pallas_call(
        functools.partial(_gdn_gates_body, n_heads=n_heads, chunk=GDN_CHUNK),
        out_shape=(jax.ShapeDtypeStruct((t, 2 * LANES), jnp.float32),
                   jax.ShapeDtypeStruct((t // ROW_BLOCK * LANES, ROW_BLOCK), jnp.float32)),
        grid=(t // ROW_BLOCK,),
        in_specs=[pl.BlockSpec((ROW_BLOCK, LANES), lambda i: (i, tail_col0 // LANES)),
                  pl.BlockSpec((SUBLANES, LANES), lambda i: (0, 0))],
        out_specs=(pl.BlockSpec((ROW_BLOCK, 2 * LANES), lambda i: (i, 0)),
                   pl.BlockSpec((LANES, ROW_BLOCK), lambda i: (i, 0))),
        compiler_params=_cparams("parallel"),
        name="gdn_gates",
    )(proj, par)


def _gdn_prep_body(q_ref, k_ref, v_ref, qp_ref, kp_ref, vp_ref, qn_ref, kn_ref, vn_ref,
                   wq_ref, wk_ref, wv_ref, gt_ref, gtt_ref,
                   u_ref, w_ref, qd_ref, ke_ref, at_ref, ge_ref, *, heads, n_heads, blocks_per_seq, n_lat_blocks):
    rb = pl.program_id(0)
    hg = pl.program_id(1)
    n = ROW_BLOCK
    c = GDN_CHUNK
    n_ch = n // c
    cs = c.bit_length() - 1
    hd = GDN_HEAD_DIM
    is_ctx = rb >= n_lat_blocks
    first = jnp.logical_or(is_ctx, rb % blocks_per_seq == 0)
    last = jnp.logical_or(is_ctx, rb % blocks_per_seq == blocks_per_seq - 1)
    m_prev = jnp.where(first, 0.0, 1.0)
    m_next = jnp.where(last, 0.0, 1.0)

    def conv_silu(cur_ref, prev_ref, next_ref, cw_ref):
        ext = jnp.concatenate([prev_ref[...] * m_prev, cur_ref[...], next_ref[...] * m_next], axis=0)
        acc = None
        for j in range(GDN_CONV):
            lo = SUBLANES - GDN_CONV // 2 + j
            term = ext[lo:lo + ROW_BLOCK, :] * cw_ref[j:j + 1, :]
            acc = term if acc is None else acc + term
        return acc * jax.nn.sigmoid(acc)

    qc = conv_silu(q_ref, qp_ref, qn_ref, wq_ref)
    kc = conv_silu(k_ref, kp_ref, kn_ref, wk_ref)
    vc = conv_silu(v_ref, vp_ref, vn_ref, wv_ref)

    gt_lo = gt_ref[:, :LANES]
    gt_hi = gt_ref[:, LANES:]
    lane = lax.broadcasted_iota(jnp.int32, (n, LANES), 1)
    blk16 = _bf(jnp.where((lax.broadcasted_iota(jnp.int32, (n, n), 0) >> cs)
                          == (lax.broadcasted_iota(jnp.int32, (n, n), 1) >> cs), 1.0, 0.0))
    wr = lax.broadcasted_iota(jnp.int32, (c, n), 0)
    wc = lax.broadcasted_iota(jnp.int32, (c, n), 1)
    eye_wide = jnp.where(wr == (wc & (c - 1)), 1.0, 0.0)
    sr = lax.broadcasted_iota(jnp.int32, (c, LANES), 0)
    sc = lax.broadcasted_iota(jnp.int32, (c, LANES), 1)
    slab_masks = []
    for odd in range(LANES // c):
        rel = sc - odd * c
        own = (sc >> cs) == odd
        slab_masks.append(((own & (rel <= sr), own & (rel < sr)), (own & (rel >= sr), own & (rel > sr))))

    def tall(wide):
        return jnp.concatenate([_bf(wide)] * n_ch, axis=0) * blk16

    def column(src, idx):
        return jnp.sum(jnp.where(lane == idx, src, 0.0), axis=1, keepdims=True)

    per_tile = LANES // c
    chains = []
    for gi in range(heads):
        h = hg * heads + gi
        sl = slice(gi * hd, (gi + 1) * hd)
        qh, kh, v = qc[:, sl], kc[:, sl], vc[:, sl]
        q = qh * lax.rsqrt(jnp.sum(qh * qh, axis=-1, keepdims=True) + NORM_EPS) * (hd ** -0.5)
        k = kh * lax.rsqrt(jnp.sum(kh * kh, axis=-1, keepdims=True) + NORM_EPS)
        k16 = _bf(k)
        qk = _dot_nt(_bf(q), k16)
        kk = _dot_nt(k16, k16)
        for d in range(2):
            beta = column(gt_lo, d * n_heads + h)
            gc = column(gt_lo, (2 + d) * n_heads + h)
            tot = column(gt_hi, (2 + d) * n_heads + h)
            gc_row = gtt_ref[pl.ds((2 + d) * n_heads + h, 1), :]
            p_tiles = []
            for j in range(n_ch):
                rows = slice(j * c, (j + 1) * c)
                cols = slice((j // per_tile) * LANES, (j // per_tile + 1) * LANES)
                incl, strict = slab_masks[j % per_tile][d]
                decay = jnp.where(incl, jnp.exp(jnp.where(incl, gc[rows] - gc_row[:, cols], 0.0)), 0.0)
                a = jnp.where(strict, (beta[rows] * kk[rows, cols]) * decay, 0.0)
                if j % per_tile == 0:
                    p_tiles.append(-a)
                else:
                    p_tiles[-1] = p_tiles[-1] - a
                lo = (j % per_tile) * c
                at_ref[d, rows, gi * c:(gi + 1) * c] = _bf((qk[rows, cols] * decay)[:, lo:lo + c])
                ge_ref[d, j * SUBLANES:(j + 1) * SUBLANES, sl] = jnp.broadcast_to(
                    jnp.exp(tot[j * c:j * c + SUBLANES]), (SUBLANES, hd))
            p = jnp.concatenate(p_tiles, axis=1)
            eg = jnp.exp(gc)
            kbeta = k * beta
            rhs = _bf(jnp.concatenate([v * beta, kbeta * eg], axis=1))
            qd_ref[d, :, sl] = _bf(q * eg)
            ke_ref[d, :, sl] = _bf(k * jnp.exp(tot - gc))
            chains.append([d, sl, p, eye_wide + p, tall(p), rhs])

    span = 2
    while span < c:
        for ch in chains:
            ch[2] = _dot(_bf(ch[2]), ch[4])
            ch[4] = tall(ch[2])
            ch[3] = ch[3] + _dot(_bf(ch[3]), ch[4])
        span *= 2
    for d, sl, _, t, _, rhs in chains:
        sol = _dot(tall(t), rhs)
        u_ref[d, :, sl] = sol[:, :hd]
        w_ref[d, :, sl] = _bf(sol[:, hd:])


def gdn_prepare(proj, gates, gates_t, conv_w, n_heads, qkv_col0, blocks_per_seq, n_lat_blocks):
    t = proj.shape[0]
    hd = GDN_HEAD_DIM
    g = GDN_PREP_HEADS
    gw = g * hd
    dim = n_heads * hd
    nb = t // ROW_BLOCK
    cb0 = qkv_col0 // gw
    per = dim // gw
    halo_per_block = ROW_BLOCK // SUBLANES

    def cur(off):
        return pl.BlockSpec((ROW_BLOCK, gw), lambda rb, hg: (rb, cb0 + off * per + hg))

    def prev(off):
        return pl.BlockSpec((SUBLANES, gw),
                            lambda rb, hg: (jnp.maximum(rb * halo_per_block - 1, 0), cb0 + off * per + hg))

    def nxt(off):
        return pl.BlockSpec((SUBLANES, gw),
                            lambda rb, hg: (jnp.minimum((rb + 1) * halo_per_block, nb * halo_per_block - 1),
                                            cb0 + off * per + hg))

    def cw(off):
        return pl.BlockSpec((GDN_CONV, gw), lambda rb, hg: (0, off * per + hg))

    out_tok = lambda width: pl.BlockSpec((2, ROW_BLOCK, width), lambda rb, hg: (0, rb, hg))
    outs = pl.pallas_call(
        functools.partial(_gdn_prep_body, heads=g, n_heads=n_heads, blocks_per_seq=blocks_per_seq,
                          n_lat_blocks=n_lat_blocks),
        out_shape=(jax.ShapeDtypeStruct((2, t, dim), jnp.float32),
                   jax.ShapeDtypeStruct((2, t, dim), jnp.bfloat16),
                   jax.ShapeDtypeStruct((2, t, dim), jnp.bfloat16),
                   jax.ShapeDtypeStruct((2, t, dim), jnp.bfloat16),
                   jax.ShapeDtypeStruct((2, t, n_heads * GDN_CHUNK), jnp.bfloat16),
                   jax.ShapeDtypeStruct((2, t // SUBLANES, dim), jnp.float32)),
        grid=(nb, n_heads // g),
        in_specs=[cur(0), cur(1), cur(2), prev(0), prev(1), prev(2), nxt(0), nxt(1), nxt(2),
                  cw(0), cw(1), cw(2), pl.BlockSpec((ROW_BLOCK, 2 * LANES), lambda rb, hg: (rb, 0)),
                  pl.BlockSpec((LANES, ROW_BLOCK), lambda rb, hg: (rb, 0))],
        out_specs=(out_tok(gw), out_tok(gw), out_tok(gw), out_tok(gw), out_tok(g * GDN_CHUNK),
                   pl.BlockSpec((2, ROW_BLOCK // SUBLANES, gw), lambda rb, hg: (0, rb, hg))),
        compiler_params=_cparams("parallel", "parallel"),
        name="gdn_prepare",
    )(proj, proj, proj, proj, proj, proj, proj, proj, proj, conv_w, conv_w, conv_w, gates, gates_t)
    return outs


def _gdn_scan_body(uf, wf, qf, kf, af, gf, ub, wb, qb, kb, ab, gb, of_ref, ob_ref, s_ref, *, heads):
    t = pl.program_id(2)
    c = GDN_CHUNK
    hd = GDN_HEAD_DIM
    n_ch = ROW_BLOCK // c

    @pl.when(t == 0)
    def _():
        s_ref[...] = jnp.zeros_like(s_ref)

    pw = 2 * hd
    s_mask = ((lax.broadcasted_iota(jnp.int32, (pw, pw), 0) >> (hd.bit_length() - 1))
              == (lax.broadcasted_iota(jnp.int32, (pw, pw), 1) >> (hd.bit_length() - 1)))
    v_mask = ((lax.broadcasted_iota(jnp.int32, (2 * c, pw), 0) >> (c.bit_length() - 1))
              == (lax.broadcasted_iota(jnp.int32, (2 * c, pw), 1) >> (hd.bit_length() - 1)))
    views = ((uf, wf, qf, kf, af, gf, of_ref), (ub, wb, qb, kb, ab, gb, ob_ref))
    chains = [(d, p) for d in range(2) for p in range(heads // 2)]
    for step in range(n_ch):
        def locate(d, p):
            ch = step if d == 0 else n_ch - 1 - step
            return views[d], ch, slice(ch * c, (ch + 1) * c), slice(p * pw, (p + 1) * pw)

        state, r_all, v_all = {}, {}, {}
        for d, p in chains:
            (u_r, w_r, q_r, k_r, a_r, g_r, o_r), ch, rows, sl = locate(d, p)
            s = s_ref[d, p]
            state[d, p] = s
            s_diag = _bf(jnp.where(s_mask, jnp.concatenate([s, s], axis=0), 0.0))
            r_all[d, p] = _dot(jnp.concatenate([w_r[0, rows, sl], q_r[0, rows, sl]], axis=0), s_diag)
        for d, p in chains:
            (u_r, w_r, q_r, k_r, a_r, g_r, o_r), ch, rows, sl = locate(d, p)
            v_new = u_r[0, rows, sl] - r_all[d, p][:c]
            v_all[d, p] = _bf(jnp.where(v_mask, jnp.concatenate([v_new, v_new], axis=0), 0.0))
        for d, p in chains:
            (u_r, w_r, q_r, k_r, a_r, g_r, o_r), ch, rows, sl = locate(d, p)
            o_r[0, rows, sl] = r_all[d, p][c:] + _dot(a_r[0, rows, p * 2 * c:(p + 1) * 2 * c], v_all[d, p])
        for d, p in chains:
            (u_r, w_r, q_r, k_r, a_r, g_r, o_r), ch, rows, sl = locate(d, p)
            k_stack = jnp.concatenate([k_r[0, rows, p * pw:p * pw + hd], k_r[0, rows, p * pw + hd:(p + 1) * pw]],
                                      axis=0)
            ge = g_r[0, ch * SUBLANES:ch * SUBLANES + 1, sl]
            s_ref[d, p] = state[d, p] * ge + _dot_tn(k_stack, v_all[d, p])


def gdn_scan(u, w, qd, ke, at, ge, n_heads, n_batch, blocks_per_seq):
    t = u.shape[1]
    hd = GDN_HEAD_DIM
    g = GDN_SCAN_HEADS
    gw = g * hd
    n_lat_blocks = n_batch * blocks_per_seq

    def fwd(b, s):
        return jnp.where(s == 0, n_lat_blocks + b, b * blocks_per_seq + s - 1)

    def bwd(b, s):
        return jnp.where(s == 0, n_lat_blocks + b, b * blocks_per_seq + blocks_per_seq - s)

    def tok(d, width, blk):
        return pl.BlockSpec((1, ROW_BLOCK, width), lambda b, hg, s: (d, blk(b, s), hg))

    def gsp(d, blk):
        return pl.BlockSpec((1, ROW_BLOCK // SUBLANES, gw), lambda b, hg, s: (d, blk(b, s), hg))

    in_specs = []
    for d, blk in ((0, fwd), (1, bwd)):
        in_specs += [tok(d, gw, blk), tok(d, gw, blk), tok(d, gw, blk), tok(d, gw, blk),
                     tok(d, g * GDN_CHUNK, blk), gsp(d, blk)]
    o_f, o_b = pl.pallas_call(
        functools.partial(_gdn_scan_body, heads=g),
        out_shape=(jax.ShapeDtypeStruct((1, t, n_heads * hd), jnp.float32),) * 2,
        grid=(n_batch, n_heads // g, blocks_per_seq + 1),
        in_specs=in_specs,
        out_specs=(pl.BlockSpec((1, ROW_BLOCK, gw), lambda b, hg, s: (0, fwd(b, s), hg)),
                   pl.BlockSpec((1, ROW_BLOCK, gw), lambda b, hg, s: (0, bwd(b, s), hg))),
        scratch_shapes=[pltpu.VMEM((2, g // 2, hd, 2 * hd), jnp.float32)],
        compiler_params=_cparams("parallel", "parallel", "arbitrary"),
        name="gdn_scan",
    )(u, w, qd, ke, at, ge, u, w, qd, ke, at, ge)
    return o_f[0], o_b[0]


GATED_NORM_COLS = 1024


def _gated_norm_body(of_ref, ob_ref, z_ref, nw_ref, lead_ref, y_ref, *, hd):
    j = pl.program_id(1)

    @pl.when(j == 0)
    def _():
        y_ref[...] = lead_ref[...]

    @pl.when(j > 0)
    def _():
        for h in range(GATED_NORM_COLS // hd):
            sl = slice(h * hd, (h + 1) * hd)
            o = of_ref[:, sl] + ob_ref[:, sl]
            y = o * lax.rsqrt(jnp.mean(o * o, axis=-1, keepdims=True) + NORM_EPS) * nw_ref[...]
            z = z_ref[:, sl]
            y_ref[:, sl] = (y * (z * jax.nn.sigmoid(z))).astype(y_ref.dtype)


def gated_head_norm(o_f, o_b, proj, z_col0, norm_w, hd, lead):
    m, cw = lead.shape
    dim = o_f.shape[1]
    assert cw == GATED_NORM_COLS and z_col0 % cw == 0 and dim % cw == 0 and cw % hd == 0

    def col(j):
        return jnp.maximum(j - 1, 0)

    return pl.pallas_call(
        functools.partial(_gated_norm_body, hd=hd),
        out_shape=jax.ShapeDtypeStruct((m, cw + dim), jnp.bfloat16),
        grid=(m // ROW_BLOCK, 1 + dim // cw),
        in_specs=[pl.BlockSpec((ROW_BLOCK, cw), lambda i, j: (i, col(j))),
                  pl.BlockSpec((ROW_BLOCK, cw), lambda i, j: (i, col(j))),
                  pl.BlockSpec((ROW_BLOCK, cw), lambda i, j: (i, z_col0 // cw + col(j))),
                  pl.BlockSpec((1, hd), lambda i, j: (0, 0)),
                  pl.BlockSpec((ROW_BLOCK, cw), lambda i, j: (i, 0))],
        out_specs=pl.BlockSpec((ROW_BLOCK, cw), lambda i, j: (i, j)),
        compiler_params=_cparams("parallel", "parallel"),
        name="gated_head_norm",
    )(o_f, o_b, proj, norm_w.reshape(1, hd), lead)


def _pool_body(u_ref, w_ref, sc_ref, y_ref, *, n_lat_blocks):
    rb = pl.program_id(0)
    is_ctx = rb >= n_lat_blocks
    row_len = jnp.where(is_ctx, CTX_LEN, GRID_W)
    shift = jnp.where(is_ctx, CTX_LEN.bit_length() - 1, GRID_W.bit_length() - 1)
    n = ROW_BLOCK
    ri = lax.broadcasted_iota(jnp.int32, (n, n), 0)
    ci = lax.broadcasted_iota(jnp.int32, (n, n), 1)
    same = (ri >> shift) == (ci >> shift)
    pos_r = ri & (row_len - 1)
    pos_c = ci & (row_len - 1)
    rcol = lax.broadcasted_iota(jnp.int32, (n, 1), 0) & (row_len - 1)
    gd = POOL_GROUP_DIM
    for g, win in enumerate(POOL_WINDOWS):
        half = win // 2
        lo = jnp.maximum(pos_r - half, 0)
        hi = jnp.minimum(pos_r + half - 1, row_len - 1)
        band = _bf(jnp.where(same & (pos_c >= lo) & (pos_c <= hi), 1.0, 0.0))
        cnt = (jnp.minimum(rcol + half - 1, row_len - 1) - jnp.maximum(rcol - half, 0) + 1).astype(jnp.float32)
        u = u_ref[:, g * gd:(g + 1) * gd]
        parts = _split3(u)
        win_sum = _dot(band, parts[0]) + _dot(band, parts[1]) + _dot(band, parts[2])
        dlt = win_sum / cnt - u
        y = _dot(_bf(dlt), _bf(w_ref[g])) * sc_ref[:, g * gd:(g + 1) * gd]
        y_ref[:, g * gd:(g + 1) * gd] = y.astype(y_ref.dtype)


def pool_mix(proj, pool_w, pool_scale, n_lat_blocks):
    t = proj.shape[0]
    return pl.pallas_call(
        functools.partial(_pool_body, n_lat_blocks=n_lat_blocks),
        out_shape=jax.ShapeDtypeStruct((t, POOL_DIM), jnp.bfloat16),
        grid=(t // ROW_BLOCK,),
        in_specs=[pl.BlockSpec((ROW_BLOCK, POOL_DIM), lambda i: (i, 0)),
                  pl.BlockSpec((POOL_GROUPS, POOL_GROUP_DIM, POOL_GROUP_DIM), lambda i: (0, 0, 0)),
                  pl.BlockSpec((1, POOL_DIM), lambda i: (0, 0))],
        out_specs=pl.BlockSpec((ROW_BLOCK, POOL_DIM), lambda i: (i, 0)),
        compiler_params=_cparams("parallel"),
        name="pool_mix",
    )(proj, pool_w, pool_scale.reshape(1, POOL_DIM))


def _moe_up_body(be_ref, first_ref, nused_ref, src_ref, src_next_ref, f_ref, w1_ref, w3_ref, h_ref,
                 w1_bf, w3_bf, rows_ref, sem):
    del be_ref
    i = pl.program_id(0)
    bm = rows_ref.shape[1]
    n_used = nused_ref[0]

    def row_copy(slot, r, src_row):
        return pltpu.make_async_copy(f_ref.at[pl.ds(src_row, 1), :], rows_ref.at[slot, pl.ds(r, 1), :], sem.at[slot])

    def issue(slot, idx_ref):
        for r in range(bm):
            row_copy(slot, r, idx_ref[0, 0, r]).start()

    @pl.when(jnp.logical_and(i == 0, n_used > 0))
    def _():
        issue(0, src_ref)

    @pl.when(i + 1 < n_used)
    def _():
        issue(lax.rem(i + 1, 2), src_next_ref)

    @pl.when(first_ref[i] == 1)
    def _():
        w1_bf[...] = _bf(w1_ref[0])
        w3_bf[...] = _bf(w3_ref[0])

    @pl.when(i < n_used)
    def _():
        slot = lax.rem(i, 2)

        def drain(r, carry):
            row_copy(slot, r, 0).wait()
            return carry
        lax.fori_loop(0, bm, drain, 0, unroll=8)
        x_lo, x_hi = _unpack_bf16_halves(rows_ref[slot])
        half = x_lo.shape[1]
        a = _dot(x_lo, w1_bf[:half, :]) + _dot(x_hi, w1_bf[half:, :])
        b = _dot(x_lo, w3_bf[:half, :]) + _dot(x_hi, w3_bf[half:, :])
        h_ref[...] = _bf(a * jax.nn.sigmoid(a) * b)

    @pl.when(i >= nused_ref[0])
    def _():
        h_ref[...] = jnp.zeros_like(h_ref)


def _moe_down_body(be_ref, first_ref, nused_ref, h_ref, w2_ref, o_ref, w2_bf):
    del be_ref
    i = pl.program_id(0)

    @pl.when(first_ref[i] == 1)
    def _():
        w2_bf[...] = _bf(w2_ref[0])

    @pl.when(i < nused_ref[0])
    def _():
        o_ref[...] = _dot(h_ref[...], w2_bf[...]).astype(o_ref.dtype)

    @pl.when(i >= nused_ref[0])
    def _():
        o_ref[...] = jnp.zeros_like(o_ref)


def moe_expert_blocks(f, src, block_expert, block_first, n_used, w1, w3, w2):
    d = w1.shape[1]
    bm = MOE_BLOCK_ROWS
    nb = src.shape[0] // bm
    p = nb * bm
    hid = w1.shape[-1]
    src_blocks = src.reshape(nb, 1, bm)
    up_spec = pltpu.PrefetchScalarGridSpec(
        num_scalar_prefetch=3,
        grid=(nb,),
        in_specs=[pl.BlockSpec((1, 1, bm), lambda i, be, fi, nu: (i, 0, 0), memory_space=pltpu.SMEM),
                  pl.BlockSpec((1, 1, bm), lambda i, be, fi, nu: (jnp.minimum(i + 1, nb - 1), 0, 0),
                               memory_space=pltpu.SMEM),
                  pl.BlockSpec(memory_space=pl.ANY),
                  pl.BlockSpec((1, d, hid), lambda i, be, fi, nu: (be[i], 0, 0)),
                  pl.BlockSpec((1, d, hid), lambda i, be, fi, nu: (be[i], 0, 0))],
        out_specs=pl.BlockSpec((bm, hid), lambda i, be, fi, nu: (i, 0)),
        scratch_shapes=[pltpu.VMEM((d, hid), jnp.bfloat16), pltpu.VMEM((d, hid), jnp.bfloat16),
                        pltpu.VMEM((2, bm, d // 2), jnp.uint32), pltpu.SemaphoreType.DMA((2,))],
    )
    hbuf = pl.pallas_call(
        _moe_up_body,
        out_shape=jax.ShapeDtypeStruct((p, hid), jnp.bfloat16),
        grid_spec=up_spec,
        compiler_params=pltpu.CompilerParams(dimension_semantics=("arbitrary",), vmem_limit_bytes=MOE_UP_VMEM_LIMIT),
        name="moe_up",
    )(block_expert, block_first, n_used, src_blocks, src_blocks, f, w1, w3)
    down_spec = pltpu.PrefetchScalarGridSpec(
        num_scalar_prefetch=3,
        grid=(nb,),
        in_specs=[pl.BlockSpec((bm, hid), lambda i, be, fi, nu: (i, 0)),
                  pl.BlockSpec((1, hid, d), lambda i, be, fi, nu: (be[i], 0, 0))],
        out_specs=pl.BlockSpec((bm, d), lambda i, be, fi, nu: (i, 0)),
        scratch_shapes=[pltpu.VMEM((hid, d), jnp.bfloat16)],
    )
    return pl.pallas_call(
        _moe_down_body,
        out_shape=jax.ShapeDtypeStruct((p, d), jnp.float32),
        grid_spec=down_spec,
        compiler_params=_cparams("arbitrary"),
        name="moe_down",
    )(block_expert, block_first, n_used, hbuf, w2)


def _moe_route_body(h_ref, w_ref, b_ref, ids_ref, wgt_ref, cnt_ref, run_ref):
    i = pl.program_id(0)

    @pl.when(i == 0)
    def _():
        run_ref[...] = jnp.zeros_like(run_ref)

    n = h_ref.shape[0]
    ng, ne = MOE_GROUPS, MOE_EXPERTS_PER_GROUP
    h_lo, h_hi = _unpack_bf16_halves(h_ref[...])
    half = h_lo.shape[1]
    logits = b_ref[...]
    for part in range(3):
        logits = logits + _dot(h_lo, w_ref[part, :half, :]) + _dot(h_hi, w_ref[part, half:, :])
    lane = lax.broadcasted_iota(jnp.int32, logits.shape, 1)
    neg = jnp.float32(-jnp.inf)

    lane_f = lane.astype(jnp.float32)

    def first_argmax(vals):
        top = jnp.max(vals, axis=1, keepdims=True)
        return top, jnp.min(jnp.where(vals == top, lane_f, float(LANES)), axis=1, keepdims=True).astype(jnp.int32)

    gl = jnp.where(lane < ng, logits, neg)
    g_top, group = first_argmax(gl)
    p_group = 1.0 / jnp.sum(jnp.exp(gl - g_top), axis=1, keepdims=True)
    lo = ng + group * ne
    sel = jnp.where((lane >= lo) & (lane < lo + ne), logits, neg)
    v1, i1 = first_argmax(sel)
    v2, i2 = first_argmax(jnp.where(lane == i1, neg, sel))
    e21 = jnp.exp(v2 - v1)
    w1 = p_group / (1.0 + e21)
    w2 = p_group * e21 / (1.0 + e21)
    e1 = i1 - ng
    e2 = i2 - ng
    oh1 = jnp.where(lane == e1, 1.0, 0.0)
    oh2 = jnp.where(lane == e2, 1.0, 0.0)
    both = oh1 + oh2
    ri = lax.broadcasted_iota(jnp.int32, (n, n), 0)
    ci = lax.broadcasted_iota(jnp.int32, (n, n), 1)
    before = _dot(_bf(jnp.where(ci < ri, 1.0, 0.0)), _bf(both)) + run_ref[0:1, :]
    r1 = jnp.sum(oh1 * before, axis=1, keepdims=True)
    r2 = jnp.sum(oh2 * before, axis=1, keepdims=True)
    run_ref[...] = run_ref[...] + jnp.sum(both, axis=0, keepdims=True)
    ids_ref[...] = jnp.where(lane == 0, e1, jnp.where(lane == 1, e2, jnp.where(
        lane == 2, r1.astype(jnp.int32), jnp.where(lane == 3, r2.astype(jnp.int32), 0))))
    wgt_ref[...] = jnp.where(lane == 0, w1, jnp.where(lane == 1, w2, 0.0))
    cnt_ref[...] = run_ref[...].astype(jnp.int32)


def moe_route(h, group_w, group_b, expert_w, expert_b):
    t, d = h.shape[0], group_w.shape[0]
    w = jnp.concatenate([group_w, expert_w.transpose(1, 0, 2).reshape(d, MOE_EXPERTS)], axis=1).astype(jnp.float32)
    w = jnp.stack(_split3(_pad_cols(w)))
    b = _pad_cols(jnp.concatenate([group_b, expert_b.reshape(-1)])[None, :].astype(jnp.float32))
    tm = ROW_BLOCK
    return pl.pallas_call(
        _moe_route_body,
        out_shape=(jax.ShapeDtypeStruct((t, LANES), jnp.int32), jax.ShapeDtypeStruct((t, LANES), jnp.float32),
                   jax.ShapeDtypeStruct((SUBLANES, LANES), jnp.int32)),
        grid=(t // tm,),
        in_specs=[pl.BlockSpec((tm, d // 2), lambda i: (i, 0)), pl.BlockSpec((3, d, LANES), lambda i: (0, 0, 0)),
                  pl.BlockSpec((1, LANES), lambda i: (0, 0))],
        out_specs=(pl.BlockSpec((tm, LANES), lambda i: (i, 0)), pl.BlockSpec((tm, LANES), lambda i: (i, 0)),
                   pl.BlockSpec((SUBLANES, LANES), lambda i: (0, 0))),
        scratch_shapes=[pltpu.VMEM((SUBLANES, LANES), jnp.float32)],
        compiler_params=_cparams("arbitrary"),
        name="moe_route",
    )(h, w, b)


def hier_moe(h, group_w, group_b, expert_w, expert_b, w1, w3, w2, expert0):
    t = h.shape[0]
    ids, weight, counts = moe_route(h, group_w, group_b, expert_w, expert_b)
    expert = ids[:, :MOE_TOP_K].T.reshape(-1)
    rank = ids[:, MOE_TOP_K:2 * MOE_TOP_K].T.reshape(-1)
    counts = counts[0, :MOE_EXPERTS]
    tk = t * MOE_TOP_K
    bm = MOE_BLOCK_ROWS
    padded = (counts + bm - 1) // bm * bm
    e_idx = jnp.arange(MOE_EXPERTS)
    pad_end = jnp.sum(jnp.where(e_idx[None, :] <= e_idx[:, None], padded[None, :], 0), axis=1)
    pad_start = pad_end - padded
    dest = pad_start[expert] + rank
    n_blocks = -(-tk // bm) + MOE_EXPERTS
    token = jnp.tile(jnp.arange(t, dtype=jnp.int32), MOE_TOP_K)
    src = jnp.zeros((n_blocks * bm,), jnp.int32).at[dest].set(token)
    block_start = jnp.arange(n_blocks, dtype=jnp.int32) * bm
    block_expert = jnp.minimum(jnp.sum((pad_end[None, :] <= block_start[:, None]).astype(jnp.int32), axis=1),
                               MOE_EXPERTS - 1)
    block_first = jnp.concatenate([jnp.ones((1,), jnp.int32),
                                   (block_expert[1:] != block_expert[:-1]).astype(jnp.int32)])
    n_used = (pad_end[-1:] // bm).astype(jnp.int32)
    ybuf = moe_expert_blocks(h, src, block_expert + expert0, block_first, n_used, w1, w3, w2)
    return ybuf, dest.astype(jnp.int32), weight


GLA_SUB = 32
GLA_STEP = 64


def _gla_prep_body(q_ref, k_ref, v_ref, lr_ref, wg_ref, gb_ref, qd_ref, ke_ref, at_ref, ge_ref, v16_ref):
    n = ROW_BLOCK
    c = GLA_STEP
    sb = GLA_SUB
    n_sb = n // sb
    q = q_ref[...] * (GLA_HEAD_K ** -0.5)
    k = k_ref[...]
    v16_ref[...] = _bf(v_ref[...])
    lr = lr_ref[...]
    ri = lax.broadcasted_iota(jnp.int32, (n, n), 0)
    ci = lax.broadcasted_iota(jnp.int32, (n, n), 1)
    cs = c.bit_length() - 1
    ss = sb.bit_length() - 1
    same_c = (ri >> cs) == (ci >> cs)
    same_s = (ri >> ss) == (ci >> ss)
    dirs = range(2)
    xs = [jnp.dot(lr, wg_ref[d], preferred_element_type=jnp.float32, precision=lax.Precision.HIGHEST) + gb_ref[d]
          for d in dirs]
    gks = [_split3((jnp.minimum(x, 0.0) - jnp.log1p(jnp.exp(-jnp.abs(x)))) * (1.0 / GLA_GATE_TEMP)) for x in xs]
    runs = [_bf(jnp.where(same_c & ((ci <= ri) if d == 0 else (ci >= ri)), 1.0, 0.0)) for d in dirs]
    gs = [_dot(runs[d], gks[d][0]) + _dot(runs[d], gks[d][1]) + _dot(runs[d], gks[d][2]) for d in dirs]
    locs, tots, totc = [], [], []
    for d in dirs:
        g = gs[d]
        off, tot_s, tot_c = [], [], []
        for b in range(n_sb):
            lo = b * sb
            inner = (b % 2 == 1) if d == 0 else (b % 2 == 0)
            edge = g[lo + sb - 1:lo + sb] if d == 0 else g[lo:lo + 1]
            if inner:
                prev = g[lo - 1:lo] if d == 0 else g[lo + sb:lo + sb + 1]
                off.append(jnp.broadcast_to(prev, (sb, prev.shape[1])))
                tot_s.append(jnp.broadcast_to(edge - prev, (sb, prev.shape[1])))
            else:
                off.append(jnp.zeros((sb, g.shape[1]), jnp.float32))
                tot_s.append(jnp.broadcast_to(edge, (sb, edge.shape[1])))
        for j in range(n // c):
            edge = g[j * c + c - 1:j * c + c] if d == 0 else g[j * c:j * c + 1]
            tot_c.append(jnp.broadcast_to(edge, (c, edge.shape[1])))
        locs.append(g - jnp.concatenate(off, axis=0))
        tots.append(jnp.concatenate(tot_s, axis=0))
        totc.append(jnp.concatenate(tot_c, axis=0))
    qd_loc = [_bf(q * jnp.exp(locs[d])) for d in dirs]
    m_diag = [_dot_nt(qd_loc[d], _bf(k * jnp.exp(-locs[d]))) for d in dirs]
    m_cross = [_dot_nt(qd_loc[d], _bf(k * jnp.exp(tots[d] - locs[d]))) for d in dirs]
    for d in dirs:
        tri = (ci <= ri) if d == 0 else (ci >= ri)
        nxt = ((ri >> ss) == (ci >> ss) + 1) if d == 0 else ((ri >> ss) + 1 == (ci >> ss))
        attn = jnp.where(same_s & tri, m_diag[d], 0.0) + jnp.where(same_c & nxt, m_cross[d], 0.0)
        qd_ref[d] = _bf(q * jnp.exp(gs[d]))
        ke_ref[d] = _bf(k * jnp.exp(totc[d] - gs[d]))
        for j in range(n // c):
            at_ref[d, 0, j * c:(j + 1) * c, :] = _bf(attn[j * c:(j + 1) * c, j * c:(j + 1) * c])
            ge_ref[d, j * SUBLANES:(j + 1) * SUBLANES, :] = jnp.exp(totc[d][j * c:j * c + SUBLANES])


def gla_prepare(proj, gate_up, gate_b):
    t = proj.shape[0]
    assert CD_MAIN % LANES == 0
    hk, hv = GLA_HEAD_K, GLA_HEAD_V
    wg = jnp.zeros((2, LANES, GLA_K_DIM), jnp.float32)
    for d in range(2):
        wg = wg.at[d, d * GLA_GATE_RANK:(d + 1) * GLA_GATE_RANK].set(gate_up[d].astype(jnp.float32))
    gb = gate_b.astype(jnp.float32).reshape(2, 1, GLA_K_DIM)
    q0, k0, v0 = CD_SPLITS[0] // hk, CD_SPLITS[1] // hk, CD_SPLITS[2] // hv
    assert CD_SPLITS[0] % hk == 0 and CD_SPLITS[1] % hk == 0 and CD_SPLITS[2] % hv == 0
    return pl.pallas_call(
        _gla_prep_body,
        out_shape=(jax.ShapeDtypeStruct((2, t, GLA_K_DIM), jnp.bfloat16),
                   jax.ShapeDtypeStruct((2, t, GLA_K_DIM), jnp.bfloat16),
                   jax.ShapeDtypeStruct((2, GLA_HEADS, t, GLA_STEP), jnp.bfloat16),
                   jax.ShapeDtypeStruct((2, t // SUBLANES, GLA_K_DIM), jnp.float32),
                   jax.ShapeDtypeStruct((t, GLA_V_DIM), jnp.bfloat16)),
        grid=(t // ROW_BLOCK, GLA_HEADS),
        in_specs=[pl.BlockSpec((ROW_BLOCK, hk), lambda rb, h: (rb, q0 + h)),
                  pl.BlockSpec((ROW_BLOCK, hk), lambda rb, h: (rb, k0 + h)),
                  pl.BlockSpec((ROW_BLOCK, hv), lambda rb, h: (rb, v0 + h)),
                  pl.BlockSpec((ROW_BLOCK, LANES), lambda rb, h: (rb, CD_MAIN // LANES)),
                  pl.BlockSpec((2, LANES, hk), lambda rb, h: (0, 0, h)),
                  pl.BlockSpec((2, 1, hk), lambda rb, h: (0, 0, h))],
        out_specs=(pl.BlockSpec((2, ROW_BLOCK, hk), lambda rb, h: (0, rb, h)),
                   pl.BlockSpec((2, ROW_BLOCK, hk), lambda rb, h: (0, rb, h)),
                   pl.BlockSpec((2, 1, ROW_BLOCK, GLA_STEP), lambda rb, h: (0, h, rb, 0)),
                   pl.BlockSpec((2, ROW_BLOCK // SUBLANES, hk), lambda rb, h: (0, rb, h)),
                   pl.BlockSpec((ROW_BLOCK, hv), lambda rb, h: (rb, h))),
        compiler_params=_cparams("parallel", "parallel"),
        name="gla_prepare",
    )(proj, proj, proj, proj, wg, gb)


def _gla_scan_body(qf, kf, af, gf, vf, qb, kb, ab, gb, vb, of_ref, ob_ref, s_ref):
    t = pl.program_id(2)
    c = GLA_STEP
    n_ch = ROW_BLOCK // c

    @pl.when(t == 0)
    def _():
        s_ref[...] = jnp.zeros_like(s_ref)

    views = ((qf, kf, af, gf, vf, of_ref), (qb, kb, ab, gb, vb, ob_ref))
    for step in range(n_ch):
        chunk = [step, n_ch - 1 - step]
        rows = [slice(ch * c, (ch + 1) * c) for ch in chunk]
        st = [s_ref[d] for d in range(2)]
        inter = [_dot_nt(views[d][0][0, rows[d], :], _bf(st[d])) for d in range(2)]
        intra = [_dot(views[d][2][0, 0, rows[d], :], views[d][4][rows[d], :]) for d in range(2)]
        upd = [_dot_tn(views[d][4][rows[d], :], views[d][1][0, rows[d], :]) for d in range(2)]
        for d in range(2):
            views[d][5][rows[d], :] = inter[d] + intra[d]
            ge = views[d][3][0, chunk[d] * SUBLANES:chunk[d] * SUBLANES + 1, :]
            s_ref[d] = st[d] * ge + upd[d]


def gla_scan(qd, ke, at, ge, v16, n_batch, blocks_per_seq):
    hk, hv = GLA_HEAD_K, GLA_HEAD_V
    t = v16.shape[0]
    n_lat_blocks = n_batch * blocks_per_seq

    def fwd(b, s):
        return jnp.where(s == 0, n_lat_blocks + b, b * blocks_per_seq + s - 1)

    def bwd(b, s):
        return jnp.where(s == 0, n_lat_blocks + b, b * blocks_per_seq + blocks_per_seq - s)

    def out_blk(blk):
        return lambda b, h, s: (blk(b, s), h)

    def key_spec(d, blk, rows):
        return pl.BlockSpec((1, rows, hk), lambda b, h, s: (d, blk(b, s), h))

    def attn_spec(d, blk):
        return pl.BlockSpec((1, 1, ROW_BLOCK, GLA_STEP), lambda b, h, s: (d, h, blk(b, s), 0))

    def val_spec(blk):
        return pl.BlockSpec((ROW_BLOCK, hv), lambda b, h, s: (blk(b, s), h))

    in_specs = []
    for d, blk in ((0, fwd), (1, bwd)):
        in_specs += [key_spec(d, blk, ROW_BLOCK), key_spec(d, blk, ROW_BLOCK), attn_spec(d, blk),
                     key_spec(d, blk, ROW_BLOCK // SUBLANES), val_spec(blk)]
    o_f, o_b = pl.pallas_call(
        _gla_scan_body,
        out_shape=(jax.ShapeDtypeStruct((t, GLA_V_DIM), jnp.float32),) * 2,
        grid=(n_batch, GLA_HEADS, blocks_per_seq + 1),
        in_specs=in_specs,
        out_specs=(pl.BlockSpec((ROW_BLOCK, hv), out_blk(fwd)), pl.BlockSpec((ROW_BLOCK, hv), out_blk(bwd))),
        scratch_shapes=[pltpu.VMEM((2, hv, hk), jnp.float32)],
        compiler_params=_cparams("parallel", "parallel", "arbitrary"),
        name="gla_scan",
    )(qd, ke, at, ge, v16, qd, ke, at, ge, v16)
    return o_f, o_b


def _fourier_in_body(x_ref, w_ref, o_ref):
    y = _dot(_bf(x_ref[...]), w_ref[0])
    half = y.shape[1] // 2
    o_ref[0] = _bf(y[:, :half])
    o_ref[1] = _bf(y[:, half:])


def dft_tables(n, nc):
    r = 64
    assert n % r == 0
    kk = jnp.arange(n, dtype=jnp.int32)[None, :]
    j1 = jnp.arange(n // r, dtype=jnp.int32)[:, None]
    j0 = jnp.arange(r, dtype=jnp.int32)[:, None]
    a = ((j1 * kk) % (n // r)).astype(jnp.float32) * (2.0 * jnp.pi / (n // r))
    b = ((j0 * kk) % n).astype(jnp.float32) * (2.0 * jnp.pi / n)
    ca, sa, cb, sb = jnp.cos(a)[:, None, :], jnp.sin(a)[:, None, :], jnp.cos(b)[None, :, :], jnp.sin(b)[None, :, :]
    scale = n ** -0.5
    cos_n = ((ca * cb - sa * sb) * scale).reshape(n, n)
    sin_n = ((sa * cb + ca * sb) * scale).reshape(n, n)
    table = jnp.concatenate([cos_n, -sin_n], axis=1).astype(jnp.bfloat16)
    cc = jnp.arange(nc, dtype=jnp.int32)
    ang = ((cc[:, None] * cc[None, :]) % nc).astype(jnp.float32) * (2.0 * jnp.pi / nc)
    return table, jnp.cos(ang) * nc ** -0.5, jnp.sin(ang) * nc ** -0.5


def fourier_mix(proj, fourier_w, n_batch, n_lat):
    gd = FOURIER_GROUP_DIM
    ng = FOURIER_GROUPS
    table, cos_c, sin_c = dft_tables(n_lat, gd)
    w_all = fourier_w.astype(jnp.float32).transpose(1, 0, 2).reshape(gd, ng * gd)
    folded = dense_matmul([(jnp.concatenate([cos_c, sin_c], axis=0), w_all)], tm=2 * gd, tn=512)
    w2 = jnp.concatenate([folded[:gd].reshape(gd, ng, gd), folded[gd:].reshape(gd, ng, gd)], axis=2)
    w2 = w2.transpose(1, 0, 2).astype(jnp.bfloat16)
    tm = min(DENSE_TM, n_lat)
    tiles = n_lat // tm
    z = pl.pallas_call(
        _fourier_in_body,
        out_shape=jax.ShapeDtypeStruct((2, n_lat, n_batch * ng * gd), jnp.bfloat16),
        grid=(n_batch, ng, tiles),
        in_specs=[pl.BlockSpec((tm, gd), lambda b, g, i: (b * tiles + i, g)),
                  pl.BlockSpec((1, gd, 2 * gd), lambda b, g, i: (g, 0, 0))],
        out_specs=pl.BlockSpec((2, tm, gd), lambda b, g, i: (0, i, b * ng + g)),
        compiler_params=_cparams("parallel", "parallel", "parallel"),
        name="fourier_in",
    )(proj, w2)
    z = z.reshape(2 * n_lat, n_batch * ng * gd)
    tn = 512
    per_b = ng * gd // tn
    return dense_matmul([(table, z)], out_dtype=jnp.bfloat16, tm=512, tn=tn,
                        out_shape=(n_batch * n_lat, ng * gd),
                        out_index=lambda i, j: ((j // per_b) * (n_lat // 512) + i, j % per_b))


def modulation(c, c_ctx, mod_w, mod_b, layer):
    b, d = c.shape
    rows = jnp.concatenate([c, c_ctx[None, :], jnp.zeros((SUBLANES - b - 1, d), c.dtype)], axis=0)
    out = dense_matmul([(jax.nn.silu(rows), mod_w.reshape(-1, mod_w.shape[-1]))], tm=SUBLANES, tn=512,
                       w_row_block=layer) + mod_b[layer]
    return tuple(m[:, None, :] for m in jnp.split(out, N_MOD, axis=-1))


COMBINE_COLS = 512


def _combine_body(dest_ref, dest_next_ref, x_ref, ybuf_ref, w_ref, gate_ref, nw_ref, o_ref, rows_ref, sem, *,
                  final_norm):
    i = pl.program_id(0)
    n_tiles = pl.num_programs(0)
    tm, d = x_ref.shape

    def row_copy(slot, r, k, src_row):
        return pltpu.make_async_copy(ybuf_ref.at[pl.ds(src_row, 1), :], rows_ref.at[slot, k, pl.ds(r, 1), :],
                                     sem.at[slot])

    def issue(slot, idx_ref):
        def body(r, carry):
            for k in range(MOE_TOP_K):
                row_copy(slot, r, k, idx_ref[0, 0, k * tm + r]).start()
            return carry
        lax.fori_loop(0, tm, body, 0, unroll=8)

    @pl.when(i == 0)
    def _():
        issue(0, dest_ref)

    @pl.when(i + 1 < n_tiles)
    def _():
        issue(lax.rem(i + 1, 2), dest_next_ref)

    slot = lax.rem(i, 2)

    def drain(r, carry):
        for k in range(MOE_TOP_K):
            row_copy(slot, r, k, 0).wait()
        return carry
    lax.fori_loop(0, tm, drain, 0, unroll=8)

    w0 = w_ref[:, 0:1]
    w1 = w_ref[:, 1:2]
    sq = jnp.zeros((tm, 1), jnp.float32)
    for c0 in range(0, d, COMBINE_COLS):
        cols = slice(c0, c0 + COMBINE_COLS)
        y = w0 * rows_ref[slot, 0, :, cols] + w1 * rows_ref[slot, 1, :, cols]
        x = x_ref[:, cols] + gate_ref[0, :, cols] * y
        o_ref[:, cols] = x
        if final_norm:
            sq = sq + jnp.sum(x * x, axis=-1, keepdims=True)
    if final_norm:
        inv = lax.rsqrt(sq * (1.0 / d) + NORM_EPS)
        for c0 in range(0, d, COMBINE_COLS):
            cols = slice(c0, c0 + COMBINE_COLS)
            o_ref[:, cols] = o_ref[:, cols] * inv * nw_ref[:, cols]


def moe_combine(x, ybuf, dest, weight, gates, gate_index, final_norm_w=None):
    m, d = dest.shape[0] // MOE_TOP_K, x.shape[1]
    tm = ROW_BLOCK
    dest_tiles = dest.reshape(MOE_TOP_K, m // tm, tm).transpose(1, 0, 2).reshape(m // tm, 1, MOE_TOP_K * tm)
    nw = jnp.ones((1, d), jnp.float32) if final_norm_w is None else final_norm_w.reshape(1, d).astype(jnp.float32)
    return pl.pallas_call(
        functools.partial(_combine_body, final_norm=final_norm_w is not None),
        out_shape=jax.ShapeDtypeStruct((m, d), jnp.float32),
        grid=(m // tm,),
        in_specs=[pl.BlockSpec((1, 1, MOE_TOP_K * tm), lambda i: (i, 0, 0), memory_space=pltpu.SMEM),
                  pl.BlockSpec((1, 1, MOE_TOP_K * tm), lambda i: (jnp.minimum(i + 1, m // tm - 1), 0, 0),
                               memory_space=pltpu.SMEM),
                  pl.BlockSpec((tm, d), lambda i: (i, 0)),
                  pl.BlockSpec(memory_space=pl.ANY),
                  pl.BlockSpec((tm, LANES), lambda i: (i, 0)),
                  pl.BlockSpec((1, 1, d), lambda i: (gate_index(i), 0, 0)),
                  pl.BlockSpec((1, d), lambda i: (0, 0))],
        out_specs=pl.BlockSpec((tm, d), lambda i: (i, 0)),
        scratch_shapes=[pltpu.VMEM((2, MOE_TOP_K, tm, d), jnp.float32), pltpu.SemaphoreType.DMA((2,))],
        compiler_params=_cparams("arbitrary"),
        name="moe_combine",
    )(dest_tiles, dest_tiles, x, ybuf, weight, gates, nw)


def kernel(x, c, ctx, c_ctx, mod_w, mod_b, norm1_w, norm2_w, ab_w_in, pool_w, pool_scale, gdn_conv_w,
           gdn_a_log, gdn_dt_bias, gdn_norm_w, ab_w_out, cd_w_in, fourier_w, gla_gate_up, gla_gate_b,
           gla_norm_w, cd_w_out, moe_group_w, moe_group_b, moe_expert_w, moe_expert_b, moe_w1, moe_w3,
           moe_w2, final_norm_w):
    assert DEPTH == 2
    n_batch, n_lat, d = x.shape
    n_ctx = ctx.shape[1]
    assert n_ctx == ROW_BLOCK and n_lat % DENSE_TM == 0 and (n_batch * n_ctx) % DENSE_TM == 0
    t_lat = n_batch * n_lat
    blocks_per_seq = n_lat // ROW_BLOCK
    n_lat_blocks = n_batch * blocks_per_seq
    xs = jnp.concatenate([x.reshape(t_lat, d), ctx.reshape(n_batch * n_ctx, d)], axis=0)

    def mod_row(rows_per_tile):
        tiles_per_seq = n_lat // rows_per_tile
        return lambda i: jnp.minimum(i // tiles_per_seq, n_batch)

    bf = jnp.bfloat16
    w1_all, w3_all, w2_all = (w.reshape((-1,) + w.shape[2:]) for w in (moe_w1, moe_w3, moe_w2))
    sh1, sc1, g1, sh2, sc2, g2 = modulation(c, c_ctx, mod_w, mod_b, 0)
    h = norm_modulate(xs, norm1_w[0], sh1, sc1, mod_row(512))
    proj = dense_matmul([(h, _pad_cols(ab_w_in[0].astype(bf), IN_PROJ_TN))], tn=IN_PROJ_TN)
    gates, gates_t = gdn_gates(proj, AB_MAIN, gdn_a_log[0], gdn_dt_bias[0], GDN_HEADS)
    u, w, qd, ke, at, ge = gdn_prepare(proj, gates, gates_t, gdn_conv_w[0], GDN_HEADS, POOL_DIM, blocks_per_seq,
                                       n_lat_blocks)
    o_f, o_b = gdn_scan(u, w, qd, ke, at, ge, GDN_HEADS, n_batch, blocks_per_seq)
    y = gated_head_norm(o_f, o_b, proj, POOL_DIM + 3 * GDN_DIM, gdn_norm_w[0], GDN_HEAD_DIM,
                        pool_mix(proj, pool_w[0], pool_scale[0], n_lat_blocks))
    xs = dense_matmul([(y, ab_w_out[0].astype(bf))], residual=xs, gates=g1, gate_index=mod_row(DENSE_TM))
    f = norm_modulate(xs, norm2_w[0], sh2, sc2, mod_row(512), packed=True)
    ybuf, dest, wgt = hier_moe(f, moe_group_w[0], moe_group_b[0], moe_expert_w[0], moe_expert_b[0], w1_all, w3_all,
                               w2_all, 0)
    xs = moe_combine(xs, ybuf, dest, wgt, g2, mod_row(ROW_BLOCK))

    sh1, sc1, g1, sh2, sc2, g2 = modulation(c, c_ctx, mod_w, mod_b, 1)
    h = norm_modulate(xs, norm1_w[1], sh1, sc1, mod_row(512))
    proj = dense_matmul([(h, _pad_cols(cd_w_in[0].astype(bf), IN_PROJ_TN))], tn=IN_PROJ_TN)
    qd, ke, at, ge, v16 = gla_prepare(proj, gla_gate_up[0], gla_gate_b[0])
    o_f, o_b = gla_scan(qd, ke, at, ge, v16, n_batch, blocks_per_seq)
    y = gated_head_norm(o_f, o_b, proj, CD_SPLITS[3], gla_norm_w[0], GLA_HEAD_V,
                        fourier_mix(proj, fourier_w[0], n_batch, n_lat))
    xl = dense_matmul([(y, cd_w_out[0].astype(bf))], residual=xs, gates=g1, gate_index=mod_row(DENSE_TM))
    f = norm_modulate(xl, norm2_w[1], sh2, sc2, mod_row(512), packed=True)
    ybuf, dest, wgt = hier_moe(f, moe_group_w[1], moe_group_b[1], moe_expert_w[1], moe_expert_b[1], w1_all, w3_all,
                               w2_all, MOE_EXPERTS)
    out = moe_combine(xl, ybuf, dest, wgt, g2, mod_row(ROW_BLOCK), final_norm_w=final_norm_w)
    return out.reshape(n_batch, n_lat, d)
```

```python
import functools

import jax
import jax.numpy as jnp
from jax import lax
from jax.experimental import pallas as pl
from jax.experimental.pallas import tpu as pltpu

D_MODEL = 4096
DEPTH = 2
CTX_LEN = 256
GRID_W = 64
N_MOD = 6
NORM_EPS = 1e-6

POOL_WINDOWS = (2, 4, 8, 16)
POOL_GROUPS = 4
POOL_GROUP_DIM = D_MODEL // 16
POOL_DIM = POOL_GROUPS * POOL_GROUP_DIM

GDN_HEAD_DIM = 128
GDN_DIM = D_MODEL - POOL_DIM
GDN_HEADS = GDN_DIM // GDN_HEAD_DIM
GDN_CONV = 5
GDN_CHUNK = 64

FOURIER_GROUPS = 4
FOURIER_GROUP_DIM = D_MODEL // 16
FOURIER_DIM = FOURIER_GROUPS * FOURIER_GROUP_DIM

GLA_HEADS = 6
GLA_V_DIM = D_MODEL - FOURIER_DIM
GLA_K_DIM = GLA_V_DIM // 2
GLA_HEAD_K = GLA_K_DIM // GLA_HEADS
GLA_HEAD_V = GLA_V_DIM // GLA_HEADS
GLA_GATE_RANK = 16
GLA_GATE_TEMP = 16.0

MOE_GROUPS = 4
MOE_EXPERTS_PER_GROUP = 8
MOE_EXPERTS = MOE_GROUPS * MOE_EXPERTS_PER_GROUP
MOE_TOP_K = 2

AB_MAIN = POOL_DIM + 4 * GDN_DIM
CD_SPLITS = (FOURIER_DIM, FOURIER_DIM + GLA_K_DIM, FOURIER_DIM + 2 * GLA_K_DIM,
             FOURIER_DIM + 2 * GLA_K_DIM + GLA_V_DIM, FOURIER_DIM + 2 * GLA_K_DIM + 2 * GLA_V_DIM)
CD_MAIN = FOURIER_DIM + 2 * GLA_K_DIM + 2 * GLA_V_DIM

LANES = 128
SUBLANES = 8
VMEM_LIMIT = 48 * 1024 * 1024
MOE_UP_VMEM_LIMIT = 56 * 1024 * 1024
MOE_BLOCK_ROWS = 256
ROW_BLOCK = 256
DENSE_TM = 1024
DENSE_TN = 512
IN_PROJ_TN = 768
NORM_TM = 512
GDN_PREP_HEADS = 4
GDN_SCAN_HEADS = 8


def _cparams(*sem):
    return pltpu.CompilerParams(dimension_semantics=sem, vmem_limit_bytes=VMEM_LIMIT)


def _dot(a, b):
    return jnp.dot(a, b, preferred_element_type=jnp.float32)


def _dot_nt(a, b):
    return lax.dot_general(a, b, (((1,), (1,)), ((), ())), preferred_element_type=jnp.float32)


def _dot_tn(a, b):
    return lax.dot_general(a, b, (((0,), (0,)), ((), ())), preferred_element_type=jnp.float32)


def _bf(x):
    return x.astype(jnp.bfloat16)


def _split3(x):
    hi = _bf(x)
    r1 = x - hi.astype(jnp.float32)
    mid = _bf(r1)
    lo = _bf(r1 - mid.astype(jnp.float32))
    return hi, mid, lo


def _mm_body(*refs, n_pairs, has_res):
    o_ref = refs[-1]
    acc = _dot(refs[0][...], refs[1][...])
    for p in range(1, n_pairs):
        acc = acc + _dot(refs[2 * p][...], refs[2 * p + 1][...])
    if has_res:
        res_ref, gate_ref = refs[2 * n_pairs], refs[2 * n_pairs + 1]
        acc = res_ref[...] + gate_ref[0] * acc
    o_ref[...] = acc.astype(o_ref.dtype)


def dense_matmul(pairs, out_dtype=jnp.float32, tm=DENSE_TM, tn=DENSE_TN, m=None, n=None, w_row_block=0, residual=None,
                 gates=None, gate_index=None, out_shape=None, out_index=None):
    m = pairs[0][0].shape[0] if m is None else m
    n = pairs[0][1].shape[1] if n is None else n
    tm = min(tm, m)
    tn = min(tn, n)
    assert m % tm == 0 and n % tn == 0, (m, n, tm, tn)
    in_specs, args = [], []
    for pair in pairs:
        x, w = pair[:2]
        row_block = pair[2] if len(pair) > 2 else w_row_block
        k = x.shape[1]
        assert w.shape[0] % k == 0
        in_specs += [pl.BlockSpec((tm, k), lambda i, j: (i, 0)),
                     pl.BlockSpec((k, tn), functools.partial(lambda i, j, rb: (rb, j), rb=row_block))]
        args += [x, w]
    if residual is not None:
        in_specs += [pl.BlockSpec((tm, tn), lambda i, j: (i, j)),
                     pl.BlockSpec((1, 1, tn), lambda i, j: (gate_index(i), 0, j))]
        args += [residual, gates]
    return pl.pallas_call(
        functools.partial(_mm_body, n_pairs=len(pairs), has_res=residual is not None),
        out_shape=jax.ShapeDtypeStruct((m, n) if out_shape is None else out_shape, out_dtype),
        grid=(m // tm, n // tn),
        in_specs=in_specs,
        out_specs=pl.BlockSpec((tm, tn), (lambda i, j: (i, j)) if out_index is None else out_index),
        compiler_params=_cparams("parallel", "parallel"),
        name="dense_matmul",
    )(*args)


def _pad_cols(w, mult=LANES):
    pad = (-w.shape[1]) % mult
    return jnp.pad(w, ((0, 0), (0, pad))) if pad else w


HI16 = 0xFFFF0000


def _pack_bf16_halves(y):
    half = y.shape[1] // 2
    u = pltpu.bitcast(_bf(y).astype(jnp.float32), jnp.uint32)
    return lax.shift_right_logical(u[:, :half], jnp.uint32(16)) | (u[:, half:] & jnp.uint32(HI16))


def _unpack_bf16_halves(u):
    lo = pltpu.bitcast(lax.shift_left(u, jnp.uint32(16)), jnp.float32)
    hi = pltpu.bitcast(u & jnp.uint32(HI16), jnp.float32)
    return _bf(lo), _bf(hi)


def _norm_mod_body(x_ref, nw_ref, shift_ref, scale_ref, o_ref, *, packed):
    x = x_ref[...]
    y = x * lax.rsqrt(jnp.mean(x * x, axis=-1, keepdims=True) + NORM_EPS) * nw_ref[...]
    y = y * (1.0 + scale_ref[0]) + shift_ref[0]
    o_ref[...] = _pack_bf16_halves(y) if packed else y.astype(o_ref.dtype)


def norm_modulate(x, norm_w, shift, scale, row_index, m=None, tm=NORM_TM, packed=False):
    m = x.shape[0] if m is None else m
    d = x.shape[1]
    vec = pl.BlockSpec((1, 1, d), lambda i: (row_index(i), 0, 0))
    d_out = d // 2 if packed else d
    return pl.pallas_call(
        functools.partial(_norm_mod_body, packed=packed),
        out_shape=jax.ShapeDtypeStruct((m, d_out), jnp.uint32 if packed else jnp.bfloat16),
        grid=(m // tm,),
        in_specs=[pl.BlockSpec((tm, d), lambda i: (i, 0)), pl.BlockSpec((1, d), lambda i: (0, 0)), vec, vec],
        out_specs=pl.BlockSpec((tm, d_out), lambda i: (i, 0)),
        compiler_params=_cparams("parallel"),
        name="norm_modulate",
    )(x, norm_w.reshape(1, d), shift, scale)


def _gdn_gates_body(tail_ref, par_ref, o_ref, ot_ref, *, n_heads, chunk):
    t = tail_ref[...]
    rows = t.shape[0]
    lane = lax.broadcasted_iota(jnp.int32, t.shape, 1)
    a_row = par_ref[0:1, :]
    dtb_row = par_ref[1:2, :]
    beta = jax.nn.sigmoid(t)
    z = t + dtb_row
    g = a_row * (jnp.maximum(z, 0.0) + jnp.log1p(jnp.exp(-jnp.abs(z))))
    ri = lax.broadcasted_iota(jnp.int32, (rows, rows), 0)
    ci = lax.broadcasted_iota(jnp.int32, (rows, rows), 1)
    shift = chunk.bit_length() - 1
    same = (ri >> shift) == (ci >> shift)
    pre = _bf(jnp.where(same & (ci <= ri), 1.0, 0.0))
    suf = _bf(jnp.where(same & (ci >= ri), 1.0, 0.0))
    ones = _bf(jnp.where(same, 1.0, 0.0))
    parts = _split3(g)
    cf = _dot(pre, parts[0]) + _dot(pre, parts[1]) + _dot(pre, parts[2])
    cb = _dot(suf, parts[0]) + _dot(suf, parts[1]) + _dot(suf, parts[2])
    tot = _dot(ones, parts[0]) + _dot(ones, parts[1]) + _dot(ones, parts[2])
    main = jnp.where(lane < 2 * n_heads, beta,
                     jnp.where(lane < 3 * n_heads, cf, jnp.where(lane < 4 * n_heads, cb, 0.0)))
    o_ref[:, :LANES] = main
    o_ref[:, LANES:] = tot
    ot_ref[...] = main.T


def gdn_gates(proj, tail_col0, a_log, dt_bias, n_heads):
    t = proj.shape[0]
    assert tail_col0 % LANES == 0
    par = jnp.zeros((SUBLANES, LANES), jnp.float32)
    par = par.at[0, 2 * n_heads:4 * n_heads].set(-jnp.exp(a_log.astype(jnp.float32)).reshape(-1))
    par = par.at[1, 2 * n_heads:4 * n_heads].set(dt_bias.astype(jnp.float32).reshape(-1))
    return pl.pallas_call(
        functools.partial(_gdn_gates_body, n_heads=n_heads, chunk=GDN_CHUNK),
        out_shape=(jax.ShapeDtypeStruct((t, 2 * LANES), jnp.float32),
                   jax.ShapeDtypeStruct((t // ROW_BLOCK * LANES, ROW_BLOCK), jnp.float32)),
        grid=(t // ROW_BLOCK,),
        in_specs=[pl.BlockSpec((ROW_BLOCK, LANES), lambda i: (i, tail_col0 // LANES)),
                  pl.BlockSpec((SUBLANES, LANES), lambda i: (0, 0))],
        out_specs=(pl.BlockSpec((ROW_BLOCK, 2 * LANES), lambda i: (i, 0)),
                   pl.BlockSpec((LANES, ROW_BLOCK), lambda i: (i, 0))),
        compiler_params=_cparams("parallel"),
        name="gdn_gates",
    )(proj, par)


def _gdn_prep_body(q_ref, k_ref, v_ref, qp_ref, kp_ref, vp_ref, qn_ref, kn_ref, vn_ref,
                   wq_ref, wk_ref, wv_ref, gt_ref, gtt_ref,
                   u_ref, w_ref, qd_ref, ke_ref, at_ref, ge_ref, *, heads, n_heads, blocks_per_seq, n_lat_blocks):
    rb = pl.program_id(0)
    hg = pl.program_id(1)
    n = ROW_BLOCK
    c = GDN_CHUNK
    n_ch = n // c
    cs = c.bit_length() - 1
    hd = GDN_HEAD_DIM
    is_ctx = rb >= n_lat_blocks
    first = jnp.logical_or(is_ctx, rb % blocks_per_seq == 0)
    last = jnp.logical_or(is_ctx, rb % blocks_per_seq == blocks_per_seq - 1)
    m_prev = jnp.where(first, 0.0, 1.0)
    m_next = jnp.where(last, 0.0, 1.0)

    def conv_silu(cur_ref, prev_ref, next_ref, cw_ref):
        ext = jnp.concatenate([prev_ref[...] * m_prev, cur_ref[...], next_ref[...] * m_next], axis=0)
        acc = None
        for j in range(GDN_CONV):
            lo = SUBLANES - GDN_CONV // 2 + j
            term = ext[lo:lo + ROW_BLOCK, :] * cw_ref[j:j + 1, :]
            acc = term if acc is None else acc + term
        return acc * jax.nn.sigmoid(acc)

    qc = conv_silu(q_ref, qp_ref, qn_ref, wq_ref)
    kc = conv_silu(k_ref, kp_ref, kn_ref, wk_ref)
    vc = conv_silu(v_ref, vp_ref, vn_ref, wv_ref)

    gt_lo = gt_ref[:, :LANES]
    gt_hi = gt_ref[:, LANES:]
    lane = lax.broadcasted_iota(jnp.int32, (n, LANES), 1)
    blk16 = _bf(jnp.where((lax.broadcasted_iota(jnp.int32, (n, n), 0) >> cs)
                          == (lax.broadcasted_iota(jnp.int32, (n, n), 1) >> cs), 1.0, 0.0))
    wr = lax.broadcasted_iota(jnp.int32, (c, n), 0)
    wc = lax.broadcasted_iota(jnp.int32, (c, n), 1)
    eye_wide = jnp.where(wr == (wc & (c - 1)), 1.0, 0.0)
    sr = lax.broadcasted_iota(jnp.int32, (c, LANES), 0)
    sc = lax.broadcasted_iota(jnp.int32, (c, LANES), 1)
    slab_masks = []
    for odd in range(LANES // c):
        rel = sc - odd * c
        own = (sc >> cs) == odd
        slab_masks.append(((own & (rel <= sr), own & (rel < sr)), (own & (rel >= sr), own & (rel > sr))))

    def tall(wide):
        return jnp.concatenate([_bf(wide)] * n_ch, axis=0) * blk16

    def column(src, idx):
        return jnp.sum(jnp.where(lane == idx, src, 0.0), axis=1, keepdims=True)

    per_tile = LANES // c
    chains = []
    for gi in range(heads):
        h = hg * heads + gi
        sl = slice(gi * hd, (gi + 1) * hd)
        qh, kh, v = qc[:, sl], kc[:, sl], vc[:, sl]
        q = qh * lax.rsqrt(jnp.sum(qh * qh, axis=-1, keepdims=True) + NORM_EPS) * (hd ** -0.5)
        k = kh * lax.rsqrt(jnp.sum(kh * kh, axis=-1, keepdims=True) + NORM_EPS)
        k16 = _bf(k)
        qk = _dot_nt(_bf(q), k16)
        kk = _dot_nt(k16, k16)
        for d in range(2):
            beta = column(gt_lo, d * n_heads + h)
            gc = column(gt_lo, (2 + d) * n_heads + h)
            tot = column(gt_hi, (2 + d) * n_heads + h)
            gc_row = gtt_ref[pl.ds((2 + d) * n_heads + h, 1), :]
            p_tiles = []
            for j in range(n_ch):
                rows = slice(j * c, (j + 1) * c)
                cols = slice((j // per_tile) * LANES, (j // per_tile + 1) * LANES)
                incl, strict = slab_masks[j % per_tile][d]
                decay = jnp.where(incl, jnp.exp(jnp.where(incl, gc[rows] - gc_row[:, cols], 0.0)), 0.0)
                a = jnp.where(strict, (beta[rows] * kk[rows, cols]) * decay, 0.0)
                if j % per_tile == 0:
                    p_tiles.append(-a)
                else:
                    p_tiles[-1] = p_tiles[-1] - a
                lo = (j % per_tile) * c
                at_ref[d, rows, gi * c:(gi + 1) * c] = _bf((qk[rows, cols] * decay)[:, lo:lo + c])
                ge_ref[d, j * SUBLANES:(j + 1) * SUBLANES, sl] = jnp.broadcast_to(
                    jnp.exp(tot[j * c:j * c + SUBLANES]), (SUBLANES, hd))
            p = jnp.concatenate(p_tiles, axis=1)
            eg = jnp.exp(gc)
            kbeta = k * beta
            rhs = _bf(jnp.concatenate([v * beta, kbeta * eg], axis=1))
            qd_ref[d, :, sl] = _bf(q * eg)
            ke_ref[d, :, sl] = _bf(k * jnp.exp(tot - gc))
            chains.append([d, sl, p, eye_wide + p, tall(p), rhs])

    span = 2
    while span < c:
        for ch in chains:
            ch[2] = _dot(_bf(ch[2]), ch[4])
            ch[4] = tall(ch[2])
            ch[3] = ch[3] + _dot(_bf(ch[3]), ch[4])
        span *= 2
    for d, sl, _, t, _, rhs in chains:
        sol = _dot(tall(t), rhs)
        u_ref[d, :, sl] = sol[:, :hd]
        w_ref[d, :, sl] = _bf(sol[:, hd:])


def gdn_prepare(proj, gates, gates_t, conv_w, n_heads, qkv_col0, blocks_per_seq, n_lat_blocks):
    t = proj.shape[0]
    hd = GDN_HEAD_DIM
    g = GDN_PREP_HEADS
    gw = g * hd
    dim = n_heads * hd
    nb = t // ROW_BLOCK
    cb0 = qkv_col0 // gw
    per = dim // gw
    halo_per_block = ROW_BLOCK // SUBLANES

    def cur(off):
        return pl.BlockSpec((ROW_BLOCK, gw), lambda rb, hg: (rb, cb0 + off * per + hg))

    def prev(off):
        return pl.BlockSpec((SUBLANES, gw),
                            lambda rb, hg: (jnp.maximum(rb * halo_per_block - 1, 0), cb0 + off * per + hg))

    def nxt(off):
        return pl.BlockSpec((SUBLANES, gw),
                            lambda rb, hg: (jnp.minimum((rb + 1) * halo_per_block, nb * halo_per_block - 1),
                                            cb0 + off * per + hg))

    def cw(off):
        return pl.BlockSpec((GDN_CONV, gw), lambda rb, hg: (0, off * per + hg))

    out_tok = lambda width: pl.BlockSpec((2, ROW_BLOCK, width), lambda rb, hg: (0, rb, hg))
    outs = pl.pallas_call(
        functools.partial(_gdn_prep_body, heads=g, n_heads=n_heads, blocks_per_seq=blocks_per_seq,
                          n_lat_blocks=n_lat_blocks),
        out_shape=(jax.ShapeDtypeStruct((2, t, dim), jnp.float32),
                   jax.ShapeDtypeStruct((2, t, dim), jnp.bfloat16),
                   jax.ShapeDtypeStruct((2, t, dim), jnp.bfloat16),
                   jax.ShapeDtypeStruct((2, t, dim), jnp.bfloat16),
                   jax.ShapeDtypeStruct((2, t, n_heads * GDN_CHUNK), jnp.bfloat16),
                   jax.ShapeDtypeStruct((2, t // SUBLANES, dim), jnp.float32)),
        grid=(nb, n_heads // g),
        in_specs=[cur(0), cur(1), cur(2), prev(0), prev(1), prev(2), nxt(0), nxt(1), nxt(2),
                  cw(0), cw(1), cw(2), pl.BlockSpec((ROW_BLOCK, 2 * LANES), lambda rb, hg: (rb, 0)),
                  pl.BlockSpec((LANES, ROW_BLOCK), lambda rb, hg: (rb, 0))],
        out_specs=(out_tok(gw), out_tok(gw), out_tok(gw), out_tok(gw), out_tok(g * GDN_CHUNK),
                   pl.BlockSpec((2, ROW_BLOCK // SUBLANES, gw), lambda rb, hg: (0, rb, hg))),
        compiler_params=_cparams("parallel", "parallel"),
        name="gdn_prepare",
    )(proj, proj, proj, proj, proj, proj, proj, proj, proj, conv_w, conv_w, conv_w, gates, gates_t)
    return outs


def _gdn_scan_body(uf, wf, qf, kf, af, gf, ub, wb, qb, kb, ab, gb, of_ref, ob_ref, s_ref, *, heads):
    t = pl.program_id(2)
    c = GDN_CHUNK
    hd = GDN_HEAD_DIM
    n_ch = ROW_BLOCK // c

    @pl.when(t == 0)
    def _():
        s_ref[...] = jnp.zeros_like(s_ref)

    pw = 2 * hd
    s_mask = ((lax.broadcasted_iota(jnp.int32, (pw, pw), 0) >> (hd.bit_length() - 1))
              == (lax.broadcasted_iota(jnp.int32, (pw, pw), 1) >> (hd.bit_length() - 1)))
    v_mask = ((lax.broadcasted_iota(jnp.int32, (2 * c, pw), 0) >> (c.bit_length() - 1))
              == (lax.broadcasted_iota(jnp.int32, (2 * c, pw), 1) >> (hd.bit_length() - 1)))
    views = ((uf, wf, qf, kf, af, gf, of_ref), (ub, wb, qb, kb, ab, gb, ob_ref))
    chains = [(d, p) for d in range(2) for p in range(heads // 2)]
    for step in range(n_ch):
        def locate(d, p):
            ch = step if d == 0 else n_ch - 1 - step
            return views[d], ch, slice(ch * c, (ch + 1) * c), slice(p * pw, (p + 1) * pw)

        state, r_all, v_all = {}, {}, {}
        for d, p in chains:
            (u_r, w_r, q_r, k_r, a_r, g_r, o_r), ch, rows, sl = locate(d, p)
            s = s_ref[d, p]
            state[d, p] = s
            s_diag = _bf(jnp.where(s_mask, jnp.concatenate([s, s], axis=0), 0.0))
            r_all[d, p] = _dot(jnp.concatenate([w_r[0, rows, sl], q_r[0, rows, sl]], axis=0), s_diag)
        for d, p in chains:
            (u_r, w_r, q_r, k_r, a_r, g_r, o_r), ch, rows, sl = locate(d, p)
            v_new = u_r[0, rows, sl] - r_all[d, p][:c]
            v_all[d, p] = _bf(jnp.where(v_mask, jnp.concatenate([v_new, v_new], axis=0), 0.0))
        for d, p in chains:
            (u_r, w_r, q_r, k_r, a_r, g_r, o_r), ch, rows, sl = locate(d, p)
            o_r[0, rows, sl] = r_all[d, p][c:] + _dot(a_r[0, rows, p * 2 * c:(p + 1) * 2 * c], v_all[d, p])
        for d, p in chains:
            (u_r, w_r, q_r, k_r, a_r, g_r, o_r), ch, rows, sl = locate(d, p)
            k_stack = jnp.concatenate([k_r[0, rows, p * pw:p * pw + hd], k_r[0, rows, p * pw + hd:(p + 1) * pw]],
                                      axis=0)
            ge = g_r[0, ch * SUBLANES:ch * SUBLANES + 1, sl]
            s_ref[d, p] = state[d, p] * ge + _dot_tn(k_stack, v_all[d, p])


def gdn_scan(u, w, qd, ke, at, ge, n_heads, n_batch, blocks_per_seq):
    t = u.shape[1]
    hd = GDN_HEAD_DIM
    g = GDN_SCAN_HEADS
    gw = g * hd
    n_lat_blocks = n_batch * blocks_per_seq

    def fwd(b, s):
        return jnp.where(s == 0, n_lat_blocks + b, b * blocks_per_seq + s - 1)

    def bwd(b, s):
        return jnp.where(s == 0, n_lat_blocks + b, b * blocks_per_seq + blocks_per_seq - s)

    def tok(d, width, blk):
        return pl.BlockSpec((1, ROW_BLOCK, width), lambda b, hg, s: (d, blk(b, s), hg))

    def gsp(d, blk):
        return pl.BlockSpec((1, ROW_BLOCK // SUBLANES, gw), lambda b, hg, s: (d, blk(b, s), hg))

    in_specs = []
    for d, blk in ((0, fwd), (1, bwd)):
        in_specs += [tok(d, gw, blk), tok(d, gw, blk), tok(d, gw, blk), tok(d, gw, blk),
                     tok(d, g * GDN_CHUNK, blk), gsp(d, blk)]
    o_f, o_b = pl.pallas_call(
        functools.partial(_gdn_scan_body, heads=g),
        out_shape=(jax.ShapeDtypeStruct((1, t, n_heads * hd), jnp.float32),) * 2,
        grid=(n_batch, n_heads // g, blocks_per_seq + 1),
        in_specs=in_specs,
        out_specs=(pl.BlockSpec((1, ROW_BLOCK, gw), lambda b, hg, s: (0, fwd(b, s), hg)),
                   pl.BlockSpec((1, ROW_BLOCK, gw), lambda b, hg, s: (0, bwd(b, s), hg))),
        scratch_shapes=[pltpu.VMEM((2, g // 2, hd, 2 * hd), jnp.float32)],
        compiler_params=_cparams("parallel", "parallel", "arbitrary"),
        name="gdn_scan",
    )(u, w, qd, ke, at, ge, u, w, qd, ke, at, ge)
    return o_f[0], o_b[0]


GATED_NORM_COLS = 1024


def _gated_norm_body(of_ref, ob_ref, z_ref, nw_ref, lead_ref, y_ref, *, hd):
    j = pl.program_id(1)

    @pl.when(j == 0)
    def _():
        y_ref[...] = lead_ref[...]

    @pl.when(j > 0)
    def _():
        for h in range(GATED_NORM_COLS // hd):
            sl = slice(h * hd, (h + 1) * hd)
            o = of_ref[:, sl] + ob_ref[:, sl]
            y = o * lax.rsqrt(jnp.mean(o * o, axis=-1, keepdims=True) + NORM_EPS) * nw_ref[...]
            z = z_ref[:, sl]
            y_ref[:, sl] = (y * (z * jax.nn.sigmoid(z))).astype(y_ref.dtype)


def gated_head_norm(o_f, o_b, proj, z_col0, norm_w, hd, lead):
    m, cw = lead.shape
    dim = o_f.shape[1]
    assert cw == GATED_NORM_COLS and z_col0 % cw == 0 and dim % cw == 0 and cw % hd == 0

    def col(j):
        return jnp.maximum(j - 1, 0)

    return pl.pallas_call(
        functools.partial(_gated_norm_body, hd=hd),
        out_shape=jax.ShapeDtypeStruct((m, cw + dim), jnp.bfloat16),
        grid=(m // ROW_BLOCK, 1 + dim // cw),
        in_specs=[pl.BlockSpec((ROW_BLOCK, cw), lambda i, j: (i, col(j))),
                  pl.BlockSpec((ROW_BLOCK, cw), lambda i, j: (i, col(j))),
                  pl.BlockSpec((ROW_BLOCK, cw), lambda i, j: (i, z_col0 // cw + col(j))),
                  pl.BlockSpec((1, hd), lambda i, j: (0, 0)),
                  pl.BlockSpec((ROW_BLOCK, cw), lambda i, j: (i, 0))],
        out_specs=pl.BlockSpec((ROW_BLOCK, cw), lambda i, j: (i, j)),
        compiler_params=_cparams("parallel", "parallel"),
        name="gated_head_norm",
    )(o_f, o_b, proj, norm_w.reshape(1, hd), lead)


def _pool_body(u_ref, w_ref, sc_ref, y_ref, *, n_lat_blocks):
    rb = pl.program_id(0)
    is_ctx = rb >= n_lat_blocks
    row_len = jnp.where(is_ctx, CTX_LEN, GRID_W)
    shift = jnp.where(is_ctx, CTX_LEN.bit_length() - 1, GRID_W.bit_length() - 1)
    n = ROW_BLOCK
    ri = lax.broadcasted_iota(jnp.int32, (n, n), 0)
    ci = lax.broadcasted_iota(jnp.int32, (n, n), 1)
    same = (ri >> shift) == (ci >> shift)
    pos_r = ri & (row_len - 1)
    pos_c = ci & (row_len - 1)
    rcol = lax.broadcasted_iota(jnp.int32, (n, 1), 0) & (row_len - 1)
    gd = POOL_GROUP_DIM
    for g, win in enumerate(POOL_WINDOWS):
        half = win // 2
        lo = jnp.maximum(pos_r - half, 0)
        hi = jnp.minimum(pos_r + half - 1, row_len - 1)
        band = _bf(jnp.where(same & (pos_c >= lo) & (pos_c <= hi), 1.0, 0.0))
        cnt = (jnp.minimum(rcol + half - 1, row_len - 1) - jnp.maximum(rcol - half, 0) + 1).astype(jnp.float32)
        u = u_ref[:, g * gd:(g + 1) * gd]
        parts = _split3(u)
        win_sum = _dot(band, parts[0]) + _dot(band, parts[1]) + _dot(band, parts[2])
        dlt = win_sum / cnt - u
        y = _dot(_bf(dlt), _bf(w_ref[g])) * sc_ref[:, g * gd:(g + 1) * gd]
        y_ref[:, g * gd:(g + 1) * gd] = y.astype(y_ref.dtype)


def pool_mix(proj, pool_w, pool_scale, n_lat_blocks):
    t = proj.shape[0]
    return pl.pallas_call(
        functools.partial(_pool_body, n_lat_blocks=n_lat_blocks),
        out_shape=jax.ShapeDtypeStruct((t, POOL_DIM), jnp.bfloat16),
        grid=(t // ROW_BLOCK,),
        in_specs=[pl.BlockSpec((ROW_BLOCK, POOL_DIM), lambda i: (i, 0)),
                  pl.BlockSpec((POOL_GROUPS, POOL_GROUP_DIM, POOL_GROUP_DIM), lambda i: (0, 0, 0)),
                  pl.BlockSpec((1, POOL_DIM), lambda i: (0, 0))],
        out_specs=pl.BlockSpec((ROW_BLOCK, POOL_DIM), lambda i: (i, 0)),
        compiler_params=_cparams("parallel"),
        name="pool_mix",
    )(proj, pool_w, pool_scale.reshape(1, POOL_DIM))


def _moe_up_body(be_ref, first_ref, nused_ref, src_ref, src_next_ref, f_ref, w1_ref, w3_ref, h_ref,
                 w1_bf, w3_bf, rows_ref, sem):
    del be_ref
    i = pl.program_id(0)
    bm = rows_ref.shape[1]
    n_used = nused_ref[0]

    def row_copy(slot, r, src_row):
        return pltpu.make_async_copy(f_ref.at[pl.ds(src_row, 1), :], rows_ref.at[slot, pl.ds(r, 1), :], sem.at[slot])

    def issue(slot, idx_ref):
        for r in range(bm):
            row_copy(slot, r, idx_ref[0, 0, r]).start()

    @pl.when(jnp.logical_and(i == 0, n_used > 0))
    def _():
        issue(0, src_ref)

    @pl.when(i + 1 < n_used)
    def _():
        issue(lax.rem(i + 1, 2), src_next_ref)

    @pl.when(first_ref[i] == 1)
    def _():
        w1_bf[...] = _bf(w1_ref[0])
        w3_bf[...] = _bf(w3_ref[0])

    @pl.when(i < n_used)
    def _():
        slot = lax.rem(i, 2)

        def drain(r, carry):
            row_copy(slot, r, 0).wait()
            return carry
        lax.fori_loop(0, bm, drain, 0, unroll=8)
        x_lo, x_hi = _unpack_bf16_halves(rows_ref[slot])
        half = x_lo.shape[1]
        a = _dot(x_lo, w1_bf[:half, :]) + _dot(x_hi, w1_bf[half:, :])
        b = _dot(x_lo, w3_bf[:half, :]) + _dot(x_hi, w3_bf[half:, :])
        h_ref[...] = _bf(a * jax.nn.sigmoid(a) * b)

    @pl.when(i >= nused_ref[0])
    def _():
        h_ref[...] = jnp.zeros_like(h_ref)


def _moe_down_body(be_ref, first_ref, nused_ref, h_ref, w2_ref, o_ref, w2_bf):
    del be_ref
    i = pl.program_id(0)

    @pl.when(first_ref[i] == 1)
    def _():
        w2_bf[...] = _bf(w2_ref[0])

    @pl.when(i < nused_ref[0])
    def _():
        o_ref[...] = _dot(h_ref[...], w2_bf[...]).astype(o_ref.dtype)

    @pl.when(i >= nused_ref[0])
    def _():
        o_ref[...] = jnp.zeros_like(o_ref)


def moe_expert_blocks(f, src, block_expert, block_first, n_used, w1, w3, w2):
    d = w1.shape[1]
    bm = MOE_BLOCK_ROWS
    nb = src.shape[0] // bm
    p = nb * bm
    hid = w1.shape[-1]
    src_blocks = src.reshape(nb, 1, bm)
    up_spec = pltpu.PrefetchScalarGridSpec(
        num_scalar_prefetch=3,
        grid=(nb,),
        in_specs=[pl.BlockSpec((1, 1, bm), lambda i, be, fi, nu: (i, 0, 0), memory_space=pltpu.SMEM),
                  pl.BlockSpec((1, 1, bm), lambda i, be, fi, nu: (jnp.minimum(i + 1, nb - 1), 0, 0),
                               memory_space=pltpu.SMEM),
                  pl.BlockSpec(memory_space=pl.ANY),
                  pl.BlockSpec((1, d, hid), lambda i, be, fi, nu: (be[i], 0, 0)),
                  pl.BlockSpec((1, d, hid), lambda i, be, fi, nu: (be[i], 0, 0))],
        out_specs=pl.BlockSpec((bm, hid), lambda i, be, fi, nu: (i, 0)),
        scratch_shapes=[pltpu.VMEM((d, hid), jnp.bfloat16), pltpu.VMEM((d, hid), jnp.bfloat16),
                        pltpu.VMEM((2, bm, d // 2), jnp.uint32), pltpu.SemaphoreType.DMA((2,))],
    )
    hbuf = pl.pallas_call(
        _moe_up_body,
        out_shape=jax.ShapeDtypeStruct((p, hid), jnp.bfloat16),
        grid_spec=up_spec,
        compiler_params=pltpu.CompilerParams(dimension_semantics=("arbitrary",), vmem_limit_bytes=MOE_UP_VMEM_LIMIT),
        name="moe_up",
    )(block_expert, block_first, n_used, src_blocks, src_blocks, f, w1, w3)
    down_spec = pltpu.PrefetchScalarGridSpec(
        num_scalar_prefetch=3,
        grid=(nb,),
        in_specs=[pl.BlockSpec((bm, hid), lambda i, be, fi, nu: (i, 0)),
                  pl.BlockSpec((1, hid, d), lambda i, be, fi, nu: (be[i], 0, 0))],
        out_specs=pl.BlockSpec((bm, d), lambda i, be, fi, nu: (i, 0)),
        scratch_shapes=[pltpu.VMEM((hid, d), jnp.bfloat16)],
    )
    return pl.pallas_call(
        _moe_down_body,
        out_shape=jax.ShapeDtypeStruct((p, d), jnp.float32),
        grid_spec=down_spec,
        compiler_params=_cparams("arbitrary"),
        name="moe_down",
    )(block_expert, block_first, n_used, hbuf, w2)


def _moe_route_body(h_ref, w_ref, b_ref, ids_ref, wgt_ref, cnt_ref, run_ref):
    i = pl.program_id(0)

    @pl.when(i == 0)
    def _():
        run_ref[...] = jnp.zeros_like(run_ref)

    n = h_ref.shape[0]
    ng, ne = MOE_GROUPS, MOE_EXPERTS_PER_GROUP
    h_lo, h_hi = _unpack_bf16_halves(h_ref[...])
    half = h_lo.shape[1]
    logits = b_ref[...]
    for part in range(3):
        logits = logits + _dot(h_lo, w_ref[part, :half, :]) + _dot(h_hi, w_ref[part, half:, :])
    lane = lax.broadcasted_iota(jnp.int32, logits.shape, 1)
    neg = jnp.float32(-jnp.inf)

    lane_f = lane.astype(jnp.float32)

    def first_argmax(vals):
        top = jnp.max(vals, axis=1, keepdims=True)
        return top, jnp.min(jnp.where(vals == top, lane_f, float(LANES)), axis=1, keepdims=True).astype(jnp.int32)

    gl = jnp.where(lane < ng, logits, neg)
    g_top, group = first_argmax(gl)
    p_group = 1.0 / jnp.sum(jnp.exp(gl - g_top), axis=1, keepdims=True)
    lo = ng + group * ne
    sel = jnp.where((lane >= lo) & (lane < lo + ne), logits, neg)
    v1, i1 = first_argmax(sel)
    v2, i2 = first_argmax(jnp.where(lane == i1, neg, sel))
    e21 = jnp.exp(v2 - v1)
    w1 = p_group / (1.0 + e21)
    w2 = p_group * e21 / (1.0 + e21)
    e1 = i1 - ng
    e2 = i2 - ng
    oh1 = jnp.where(lane == e1, 1.0, 0.0)
    oh2 = jnp.where(lane == e2, 1.0, 0.0)
    both = oh1 + oh2
    ri = lax.broadcasted_iota(jnp.int32, (n, n), 0)
    ci = lax.broadcasted_iota(jnp.int32, (n, n), 1)
    before = _dot(_bf(jnp.where(ci < ri, 1.0, 0.0)), _bf(both)) + run_ref[0:1, :]
    r1 = jnp.sum(oh1 * before, axis=1, keepdims=True)
    r2 = jnp.sum(oh2 * before, axis=1, keepdims=True)
    run_ref[...] = run_ref[...] + jnp.sum(both, axis=0, keepdims=True)
    ids_ref[...] = jnp.where(lane == 0, e1, jnp.where(lane == 1, e2, jnp.where(
        lane == 2, r1.astype(jnp.int32), jnp.where(lane == 3, r2.astype(jnp.int32), 0))))
    wgt_ref[...] = jnp.where(lane == 0, w1, jnp.where(lane == 1, w2, 0.0))
    cnt_ref[...] = run_ref[...].astype(jnp.int32)


def moe_route(h, group_w, group_b, expert_w, expert_b):
    t, d = h.shape[0], group_w.shape[0]
    w = jnp.concatenate([group_w, expert_w.transpose(1, 0, 2).reshape(d, MOE_EXPERTS)], axis=1).astype(jnp.float32)
    w = jnp.stack(_split3(_pad_cols(w)))
    b = _pad_cols(jnp.concatenate([group_b, expert_b.reshape(-1)])[None, :].astype(jnp.float32))
    tm = ROW_BLOCK
    return pl.pallas_call(
        _moe_route_body,
        out_shape=(jax.ShapeDtypeStruct((t, LANES), jnp.int32), jax.ShapeDtypeStruct((t, LANES), jnp.float32),
                   jax.ShapeDtypeStruct((SUBLANES, LANES), jnp.int32)),
        grid=(t // tm,),
        in_specs=[pl.BlockSpec((tm, d // 2), lambda i: (i, 0)), pl.BlockSpec((3, d, LANES), lambda i: (0, 0, 0)),
                  pl.BlockSpec((1, LANES), lambda i: (0, 0))],
        out_specs=(pl.BlockSpec((tm, LANES), lambda i: (i, 0)), pl.BlockSpec((tm, LANES), lambda i: (i, 0)),
                   pl.BlockSpec((SUBLANES, LANES), lambda i: (0, 0))),
        scratch_shapes=[pltpu.VMEM((SUBLANES, LANES), jnp.float32)],
        compiler_params=_cparams("arbitrary"),
        name="moe_route",
    )(h, w, b)


def hier_moe(h, group_w, group_b, expert_w, expert_b, w1, w3, w2, expert0):
    t = h.shape[0]
    ids, weight, counts = moe_route(h, group_w, group_b, expert_w, expert_b)
    expert = ids[:, :MOE_TOP_K].T.reshape(-1)
    rank = ids[:, MOE_TOP_K:2 * MOE_TOP_K].T.reshape(-1)
    counts = counts[0, :MOE_EXPERTS]
    tk = t * MOE_TOP_K
    bm = MOE_BLOCK_ROWS
    padded = (counts + bm - 1) // bm * bm
    e_idx = jnp.arange(MOE_EXPERTS)
    pad_end = jnp.sum(jnp.where(e_idx[None, :] <= e_idx[:, None], padded[None, :], 0), axis=1)
    pad_start = pad_end - padded
    dest = pad_start[expert] + rank
    n_blocks = -(-tk // bm) + MOE_EXPERTS
    token = jnp.tile(jnp.arange(t, dtype=jnp.int32), MOE_TOP_K)
    src = jnp.zeros((n_blocks * bm,), jnp.int32).at[dest].set(token)
    block_start = jnp.arange(n_blocks, dtype=jnp.int32) * bm
    block_expert = jnp.minimum(jnp.sum((pad_end[None, :] <= block_start[:, None]).astype(jnp.int32), axis=1),
                               MOE_EXPERTS - 1)
    block_first = jnp.concatenate([jnp.ones((1,), jnp.int32),
                                   (block_expert[1:] != block_expert[:-1]).astype(jnp.int32)])
    n_used = (pad_end[-1:] // bm).astype(jnp.int32)
    ybuf = moe_expert_blocks(h, src, block_expert + expert0, block_first, n_used, w1, w3, w2)
    return ybuf, dest.astype(jnp.int32), weight


GLA_SUB = 32
GLA_STEP = 64


def _gla_prep_body(q_ref, k_ref, v_ref, lr_ref, wg_ref, gb_ref, qd_ref, ke_ref, at_ref, ge_ref, v16_ref):
    n = ROW_BLOCK
    c = GLA_STEP
    sb = GLA_SUB
    n_sb = n // sb
    q = q_ref[...] * (GLA_HEAD_K ** -0.5)
    k = k_ref[...]
    v16_ref[...] = _bf(v_ref[...])
    lr = lr_ref[...]
    ri = lax.broadcasted_iota(jnp.int32, (n, n), 0)
    ci = lax.broadcasted_iota(jnp.int32, (n, n), 1)
    cs = c.bit_length() - 1
    ss = sb.bit_length() - 1
    same_c = (ri >> cs) == (ci >> cs)
    same_s = (ri >> ss) == (ci >> ss)
    dirs = range(2)
    xs = [jnp.dot(lr, wg_ref[d], preferred_element_type=jnp.float32, precision=lax.Precision.HIGHEST) + gb_ref[d]
          for d in dirs]
    gks = [_split3((jnp.minimum(x, 0.0) - jnp.log1p(jnp.exp(-jnp.abs(x)))) * (1.0 / GLA_GATE_TEMP)) for x in xs]
    runs = [_bf(jnp.where(same_c & ((ci <= ri) if d == 0 else (ci >= ri)), 1.0, 0.0)) for d in dirs]
    gs = [_dot(runs[d], gks[d][0]) + _dot(runs[d], gks[d][1]) + _dot(runs[d], gks[d][2]) for d in dirs]
    locs, tots, totc = [], [], []
    for d in dirs:
        g = gs[d]
        off, tot_s, tot_c = [], [], []
        for b in range(n_sb):
            lo = b * sb
            inner = (b % 2 == 1) if d == 0 else (b % 2 == 0)
            edge = g[lo + sb - 1:lo + sb] if d == 0 else g[lo:lo + 1]
            if inner:
                prev = g[lo - 1:lo] if d == 0 else g[lo + sb:lo + sb + 1]
                off.append(jnp.broadcast_to(prev, (sb, prev.shape[1])))
                tot_s.append(jnp.broadcast_to(edge - prev, (sb, prev.shape[1])))
            else:
                off.append(jnp.zeros((sb, g.shape[1]), jnp.float32))
                tot_s.append(jnp.broadcast_to(edge, (sb, edge.shape[1])))
        for j in range(n // c):
            edge = g[j * c + c - 1:j * c + c] if d == 0 else g[j * c:j * c + 1]
            tot_c.append(jnp.broadcast_to(edge, (c, edge.shape[1])))
        locs.append(g - jnp.concatenate(off, axis=0))
        tots.append(jnp.concatenate(tot_s, axis=0))
        totc.append(jnp.concatenate(tot_c, axis=0))
    qd_loc = [_bf(q * jnp.exp(locs[d])) for d in dirs]
    m_diag = [_dot_nt(qd_loc[d], _bf(k * jnp.exp(-locs[d]))) for d in dirs]
    m_cross = [_dot_nt(qd_loc[d], _bf(k * jnp.exp(tots[d] - locs[d]))) for d in dirs]
    for d in dirs:
        tri = (ci <= ri) if d == 0 else (ci >= ri)
        nxt = ((ri >> ss) == (ci >> ss) + 1) if d == 0 else ((ri >> ss) + 1 == (ci >> ss))
        attn = jnp.where(same_s & tri, m_diag[d], 0.0) + jnp.where(same_c & nxt, m_cross[d], 0.0)
        qd_ref[d] = _bf(q * jnp.exp(gs[d]))
        ke_ref[d] = _bf(k * jnp.exp(totc[d] - gs[d]))
        for j in range(n // c):
            at_ref[d, 0, j * c:(j + 1) * c, :] = _bf(attn[j * c:(j + 1) * c, j * c:(j + 1) * c])
            ge_ref[d, j * SUBLANES:(j + 1) * SUBLANES, :] = jnp.exp(totc[d][j * c:j * c + SUBLANES])


def gla_prepare(proj, gate_up, gate_b):
    t = proj.shape[0]
    assert CD_MAIN % LANES == 0
    hk, hv = GLA_HEAD_K, GLA_HEAD_V
    wg = jnp.zeros((2, LANES, GLA_K_DIM), jnp.float32)
    for d in range(2):
        wg = wg.at[d, d * GLA_GATE_RANK:(d + 1) * GLA_GATE_RANK].set(gate_up[d].astype(jnp.float32))
    gb = gate_b.astype(jnp.float32).reshape(2, 1, GLA_K_DIM)
    q0, k0, v0 = CD_SPLITS[0] // hk, CD_SPLITS[1] // hk, CD_SPLITS[2] // hv
    assert CD_SPLITS[0] % hk == 0 and CD_SPLITS[1] % hk == 0 and CD_SPLITS[2] % hv == 0
    return pl.pallas_call(
        _gla_prep_body,
        out_shape=(jax.ShapeDtypeStruct((2, t, GLA_K_DIM), jnp.bfloat16),
                   jax.ShapeDtypeStruct((2, t, GLA_K_DIM), jnp.bfloat16),
                   jax.ShapeDtypeStruct((2, GLA_HEADS, t, GLA_STEP), jnp.bfloat16),
                   jax.ShapeDtypeStruct((2, t // SUBLANES, GLA_K_DIM), jnp.float32),
                   jax.ShapeDtypeStruct((t, GLA_V_DIM), jnp.bfloat16)),
        grid=(t // ROW_BLOCK, GLA_HEADS),
        in_specs=[pl.BlockSpec((ROW_BLOCK, hk), lambda rb, h: (rb, q0 + h)),
                  pl.BlockSpec((ROW_BLOCK, hk), lambda rb, h: (rb, k0 + h)),
                  pl.BlockSpec((ROW_BLOCK, hv), lambda rb, h: (rb, v0 + h)),
                  pl.BlockSpec((ROW_BLOCK, LANES), lambda rb, h: (rb, CD_MAIN // LANES)),
                  pl.BlockSpec((2, LANES, hk), lambda rb, h: (0, 0, h)),
                  pl.BlockSpec((2, 1, hk), lambda rb, h: (0, 0, h))],
        out_specs=(pl.BlockSpec((2, ROW_BLOCK, hk), lambda rb, h: (0, rb, h)),
                   pl.BlockSpec((2, ROW_BLOCK, hk), lambda rb, h: (0, rb, h)),
                   pl.BlockSpec((2, 1, ROW_BLOCK, GLA_STEP), lambda rb, h: (0, h, rb, 0)),
                   pl.BlockSpec((2, ROW_BLOCK // SUBLANES, hk), lambda rb, h: (0, rb, h)),
                   pl.BlockSpec((ROW_BLOCK, hv), lambda rb, h: (rb, h))),
        compiler_params=_cparams("parallel", "parallel"),
        name="gla_prepare",
    )(proj, proj, proj, proj, wg, gb)


def _gla_scan_body(qf, kf, af, gf, vf, qb, kb, ab, gb, vb, of_ref, ob_ref, s_ref):
    t = pl.program_id(2)
    c = GLA_STEP
    n_ch = ROW_BLOCK // c

    @pl.when(t == 0)
    def _():
        s_ref[...] = jnp.zeros_like(s_ref)

    views = ((qf, kf, af, gf, vf, of_ref), (qb, kb, ab, gb, vb, ob_ref))
    for step in range(n_ch):
        chunk = [step, n_ch - 1 - step]
        rows = [slice(ch * c, (ch + 1) * c) for ch in chunk]
        st = [s_ref[d] for d in range(2)]
        inter = [_dot_nt(views[d][0][0, rows[d], :], _bf(st[d])) for d in range(2)]
        intra = [_dot(views[d][2][0, 0, rows[d], :], views[d][4][rows[d], :]) for d in range(2)]
        upd = [_dot_tn(views[d][4][rows[d], :], views[d][1][0, rows[d], :]) for d in range(2)]
        for d in range(2):
            views[d][5][rows[d], :] = inter[d] + intra[d]
            ge = views[d][3][0, chunk[d] * SUBLANES:chunk[d] * SUBLANES + 1, :]
            s_ref[d] = st[d] * ge + upd[d]


def gla_scan(qd, ke, at, ge, v16, n_batch, blocks_per_seq):
    hk, hv = GLA_HEAD_K, GLA_HEAD_V
    t = v16.shape[0]
    n_lat_blocks = n_batch * blocks_per_seq

    def fwd(b, s):
        return jnp.where(s == 0, n_lat_blocks + b, b * blocks_per_seq + s - 1)

    def bwd(b, s):
        return jnp.where(s == 0, n_lat_blocks + b, b * blocks_per_seq + blocks_per_seq - s)

    def out_blk(blk):
        return lambda b, h, s: (blk(b, s), h)

    def key_spec(d, blk, rows):
        return pl.BlockSpec((1, rows, hk), lambda b, h, s: (d, blk(b, s), h))

    def attn_spec(d, blk):
        return pl.BlockSpec((1, 1, ROW_BLOCK, GLA_STEP), lambda b, h, s: (d, h, blk(b, s), 0))

    def val_spec(blk):
        return pl.BlockSpec((ROW_BLOCK, hv), lambda b, h, s: (blk(b, s), h))

    in_specs = []
    for d, blk in ((0, fwd), (1, bwd)):
        in_specs += [key_spec(d, blk, ROW_BLOCK), key_spec(d, blk, ROW_BLOCK), attn_spec(d, blk),
                     key_spec(d, blk, ROW_BLOCK // SUBLANES), val_spec(blk)]
    o_f, o_b = pl.pallas_call(
        _gla_scan_body,
        out_shape=(jax.ShapeDtypeStruct((t, GLA_V_DIM), jnp.float32),) * 2,
        grid=(n_batch, GLA_HEADS, blocks_per_seq + 1),
        in_specs=in_specs,
        out_specs=(pl.BlockSpec((ROW_BLOCK, hv), out_blk(fwd)), pl.BlockSpec((ROW_BLOCK, hv), out_blk(bwd))),
        scratch_shapes=[pltpu.VMEM((2, hv, hk), jnp.float32)],
        compiler_params=_cparams("parallel", "parallel", "arbitrary"),
        name="gla_scan",
    )(qd, ke, at, ge, v16, qd, ke, at, ge, v16)
    return o_f, o_b


def _fourier_in_body(x_ref, w_ref, o_ref):
    y = _dot(_bf(x_ref[...]), w_ref[0])
    half = y.shape[1] // 2
    o_ref[0] = _bf(y[:, :half])
    o_ref[1] = _bf(y[:, half:])


def dft_tables(n, nc):
    r = 64
    assert n % r == 0
    kk = jnp.arange(n, dtype=jnp.int32)[None, :]
    j1 = jnp.arange(n // r, dtype=jnp.int32)[:, None]
    j0 = jnp.arange(r, dtype=jnp.int32)[:, None]
    a = ((j1 * kk) % (n // r)).astype(jnp.float32) * (2.0 * jnp.pi / (n // r))
    b = ((j0 * kk) % n).astype(jnp.float32) * (2.0 * jnp.pi / n)
    ca, sa, cb, sb = jnp.cos(a)[:, None, :], jnp.sin(a)[:, None, :], jnp.cos(b)[None, :, :], jnp.sin(b)[None, :, :]
    scale = n ** -0.5
    cos_n = ((ca * cb - sa * sb) * scale).reshape(n, n).astype(jnp.bfloat16)
    neg_sin_n = ((sa * cb + ca * sb) * -scale).reshape(n, n).astype(jnp.bfloat16)
    cc = jnp.arange(nc, dtype=jnp.int32)
    ang = ((cc[:, None] * cc[None, :]) % nc).astype(jnp.float32) * (2.0 * jnp.pi / nc)
    return cos_n, neg_sin_n, jnp.cos(ang) * nc ** -0.5, jnp.sin(ang) * nc ** -0.5


def fourier_mix(proj, fourier_w, n_batch, n_lat):
    gd = FOURIER_GROUP_DIM
    ng = FOURIER_GROUPS
    cos_n, neg_sin_n, cos_c, sin_c = dft_tables(n_lat, gd)
    w_all = fourier_w.astype(jnp.float32).transpose(1, 0, 2).reshape(gd, ng * gd)
    folded = dense_matmul([(jnp.concatenate([cos_c, sin_c], axis=0), w_all)], tm=2 * gd)
    w2 = jnp.concatenate([folded[:gd].reshape(gd, ng, gd), folded[gd:].reshape(gd, ng, gd)], axis=2)
    w2 = w2.transpose(1, 0, 2).astype(jnp.bfloat16)
    tm = min(DENSE_TM, n_lat)
    tiles = n_lat // tm
    z = pl.pallas_call(
        _fourier_in_body,
        out_shape=jax.ShapeDtypeStruct((2, n_lat, n_batch * ng * gd), jnp.bfloat16),
        grid=(n_batch, ng, tiles),
        in_specs=[pl.BlockSpec((tm, gd), lambda b, g, i: (b * tiles + i, g)),
                  pl.BlockSpec((1, gd, 2 * gd), lambda b, g, i: (g, 0, 0))],
        out_specs=pl.BlockSpec((2, tm, gd), lambda b, g, i: (0, i, b * ng + g)),
        compiler_params=_cparams("parallel", "parallel", "parallel"),
        name="fourier_in",
    )(proj, w2)
    z = z.reshape(2 * n_lat, n_batch * ng * gd)
    tm2, tn = min(NORM_TM, n_lat), DENSE_TN
    per_b = ng * gd // tn
    return dense_matmul([(cos_n, z, 0), (neg_sin_n, z, 1)], out_dtype=jnp.bfloat16, tm=tm2, tn=tn,
                        out_shape=(n_batch * n_lat, ng * gd),
                        out_index=lambda i, j: ((j // per_b) * (n_lat // tm2) + i, j % per_b))


def modulation(c, c_ctx, mod_w, mod_b, layer):
    b, d = c.shape
    rows = jnp.concatenate([c, c_ctx[None, :], jnp.zeros((SUBLANES - b - 1, d), c.dtype)], axis=0)
    out = dense_matmul([(jax.nn.silu(rows), mod_w.reshape(-1, mod_w.shape[-1]))], tm=SUBLANES,
                       w_row_block=layer) + mod_b[layer]
    return tuple(m[:, None, :] for m in jnp.split(out, N_MOD, axis=-1))


COMBINE_COLS = 512


def _combine_body(dest_ref, dest_next_ref, x_ref, ybuf_ref, w_ref, gate_ref, nw_ref, o_ref, rows_ref, sem, *,
                  final_norm):
    i = pl.program_id(0)
    n_tiles = pl.num_programs(0)
    tm, d = x_ref.shape

    def row_copy(slot, r, k, src_row):
        return pltpu.make_async_copy(ybuf_ref.at[pl.ds(src_row, 1), :], rows_ref.at[slot, k, pl.ds(r, 1), :],
                                     sem.at[slot])

    def issue(slot, idx_ref):
        def body(r, carry):
            for k in range(MOE_TOP_K):
                row_copy(slot, r, k, idx_ref[0, 0, k * tm + r]).start()
            return carry
        lax.fori_loop(0, tm, body, 0, unroll=8)

    @pl.when(i == 0)
    def _():
        issue(0, dest_ref)

    @pl.when(i + 1 < n_tiles)
    def _():
        issue(lax.rem(i + 1, 2), dest_next_ref)

    slot = lax.rem(i, 2)

    def drain(r, carry):
        for k in range(MOE_TOP_K):
            row_copy(slot, r, k, 0).wait()
        return carry
    lax.fori_loop(0, tm, drain, 0, unroll=8)

    w0 = w_ref[:, 0:1]
    w1 = w_ref[:, 1:2]
    sq = jnp.zeros((tm, 1), jnp.float32)
    for c0 in range(0, d, COMBINE_COLS):
        cols = slice(c0, c0 + COMBINE_COLS)
        y = w0 * rows_ref[slot, 0, :, cols] + w1 * rows_ref[slot, 1, :, cols]
        x = x_ref[:, cols] + gate_ref[0, :, cols] * y
        o_ref[:, cols] = x
        if final_norm:
            sq = sq + jnp.sum(x * x, axis=-1, keepdims=True)
    if final_norm:
        inv = lax.rsqrt(sq * (1.0 / d) + NORM_EPS)
        for c0 in range(0, d, COMBINE_COLS):
            cols = slice(c0, c0 + COMBINE_COLS)
            o_ref[:, cols] = o_ref[:, cols] * inv * nw_ref[:, cols]


def moe_combine(x, ybuf, dest, weight, gates, gate_index, final_norm_w=None):
    m, d = dest.shape[0] // MOE_TOP_K, x.shape[1]
    tm = ROW_BLOCK
    dest_tiles = dest.reshape(MOE_TOP_K, m // tm, tm).transpose(1, 0, 2).reshape(m // tm, 1, MOE_TOP_K * tm)
    nw = jnp.ones((1, d), jnp.float32) if final_norm_w is None else final_norm_w.reshape(1, d).astype(jnp.float32)
    return pl.pallas_call(
        functools.partial(_combine_body, final_norm=final_norm_w is not None),
        out_shape=jax.ShapeDtypeStruct((m, d), jnp.float32),
        grid=(m // tm,),
        in_specs=[pl.BlockSpec((1, 1, MOE_TOP_K * tm), lambda i: (i, 0, 0), memory_space=pltpu.SMEM),
                  pl.BlockSpec((1, 1, MOE_TOP_K * tm), lambda i: (jnp.minimum(i + 1, m // tm - 1), 0, 0),
                               memory_space=pltpu.SMEM),
                  pl.BlockSpec((tm, d), lambda i: (i, 0)),
                  pl.BlockSpec(memory_space=pl.ANY),
                  pl.BlockSpec((tm, LANES), lambda i: (i, 0)),
                  pl.BlockSpec((1, 1, d), lambda i: (gate_index(i), 0, 0)),
                  pl.BlockSpec((1, d), lambda i: (0, 0))],
        out_specs=pl.BlockSpec((tm, d), lambda i: (i, 0)),
        scratch_shapes=[pltpu.VMEM((2, MOE_TOP_K, tm, d), jnp.float32), pltpu.SemaphoreType.DMA((2,))],
        compiler_params=_cparams("arbitrary"),
        name="moe_combine",
    )(dest_tiles, dest_tiles, x, ybuf, weight, gates, nw)


def kernel(x, c, ctx, c_ctx, mod_w, mod_b, norm1_w, norm2_w, ab_w_in, pool_w, pool_scale, gdn_conv_w,
           gdn_a_log, gdn_dt_bias, gdn_norm_w, ab_w_out, cd_w_in, fourier_w, gla_gate_up, gla_gate_b,
           gla_norm_w, cd_w_out, moe_group_w, moe_group_b, moe_expert_w, moe_expert_b, moe_w1, moe_w3,
           moe_w2, final_norm_w):
    assert DEPTH == 2
    n_batch, n_lat, d = x.shape
    n_ctx = ctx.shape[1]
    assert n_ctx == ROW_BLOCK and n_lat % DENSE_TM == 0 and (n_batch * n_ctx) % DENSE_TM == 0
    t_lat = n_batch * n_lat
    blocks_per_seq = n_lat // ROW_BLOCK
    n_lat_blocks = n_batch * blocks_per_seq
    xs = jnp.concatenate([x.reshape(t_lat, d), ctx.reshape(n_batch * n_ctx, d)], axis=0)

    def mod_row(rows_per_tile):
        tiles_per_seq = n_lat // rows_per_tile
        return lambda i: jnp.minimum(i // tiles_per_seq, n_batch)

    bf = jnp.bfloat16
    w1_all, w3_all, w2_all = (w.reshape((-1,) + w.shape[2:]) for w in (moe_w1, moe_w3, moe_w2))
    sh1, sc1, g1, sh2, sc2, g2 = modulation(c, c_ctx, mod_w, mod_b, 0)
    h = norm_modulate(xs, norm1_w[0], sh1, sc1, mod_row(NORM_TM))
    proj = dense_matmul([(h, _pad_cols(ab_w_in[0].astype(bf), IN_PROJ_TN))], tn=IN_PROJ_TN)
    gates, gates_t = gdn_gates(proj, AB_MAIN, gdn_a_log[0], gdn_dt_bias[0], GDN_HEADS)
    u, w, qd, ke, at, ge = gdn_prepare(proj, gates, gates_t, gdn_conv_w[0], GDN_HEADS, POOL_DIM, blocks_per_seq,
                                       n_lat_blocks)
    o_f, o_b = gdn_scan(u, w, qd, ke, at, ge, GDN_HEADS, n_batch, blocks_per_seq)
    y = gated_head_norm(o_f, o_b, proj, POOL_DIM + 3 * GDN_DIM, gdn_norm_w[0], GDN_HEAD_DIM,
                        pool_mix(proj, pool_w[0], pool_scale[0], n_lat_blocks))
    xs = dense_matmul([(y, ab_w_out[0].astype(bf))], residual=xs, gates=g1, gate_index=mod_row(DENSE_TM))
    f = norm_modulate(xs, norm2_w[0], sh2, sc2, mod_row(NORM_TM), packed=True)
    ybuf, dest, wgt = hier_moe(f, moe_group_w[0], moe_group_b[0], moe_expert_w[0], moe_expert_b[0], w1_all, w3_all,
                               w2_all, 0)
    xs = moe_combine(xs, ybuf, dest, wgt, g2, mod_row(ROW_BLOCK))

    sh1, sc1, g1, sh2, sc2, g2 = modulation(c, c_ctx, mod_w, mod_b, 1)
    h = norm_modulate(xs, norm1_w[1], sh1, sc1, mod_row(NORM_TM))
    proj = dense_matmul([(h, _pad_cols(cd_w_in[0].astype(bf), IN_PROJ_TN))], tn=IN_PROJ_TN)
    qd, ke, at, ge, v16 = gla_prepare(proj, gla_gate_up[0], gla_gate_b[0])
    o_f, o_b = gla_scan(qd, ke, at, ge, v16, n_batch, blocks_per_seq)
    y = gated_head_norm(o_f, o_b, proj, CD_SPLITS[3], gla_norm_w[0], GLA_HEAD_V,
                        fourier_mix(proj, fourier_w[0], n_batch, n_lat))
    xl = dense_matmul([(y, cd_w_out[0].astype(bf))], residual=xs, gates=g1, gate_index=mod_row(DENSE_TM))
    f = norm_modulate(xl, norm2_w[1], sh2, sc2, mod_row(NORM_TM), packed=True)
    ybuf, dest, wgt = hier_moe(f, moe_group_w[1], moe_group_b[1], moe_expert_w[1], moe_expert_b[1], w1_all, w3_all,
                               w2_all, MOE_EXPERTS)
    out = moe_combine(xl, ybuf, dest, wgt, g2, mod_row(ROW_BLOCK), final_norm_w=final_norm_w)
    return out.reshape(n_batch, n_lat, d)
```

```python
import functools

import jax
import jax.numpy as jnp
from jax import lax
from jax.experimental import pallas as pl
from jax.experimental.pallas import tpu as pltpu

D_MODEL = 4096
DEPTH = 2
CTX_LEN = 256
GRID_W = 64
N_MOD = 6
NORM_EPS = 1e-6

POOL_WINDOWS = (2, 4, 8, 16)
POOL_GROUPS = 4
POOL_GROUP_DIM = D_MODEL // 16
POOL_DIM = POOL_GROUPS * POOL_GROUP_DIM

GDN_HEAD_DIM = 128
GDN_DIM = D_MODEL - POOL_DIM
GDN_HEADS = GDN_DIM // GDN_HEAD_DIM
GDN_CONV = 5
GDN_CHUNK = 64

FOURIER_GROUPS = 4
FOURIER_GROUP_DIM = D_MODEL // 16
FOURIER_DIM = FOURIER_GROUPS * FOURIER_GROUP_DIM

GLA_HEADS = 6
GLA_V_DIM = D_MODEL - FOURIER_DIM
GLA_K_DIM = GLA_V_DIM // 2
GLA_HEAD_K = GLA_K_DIM // GLA_HEADS
GLA_HEAD_V = GLA_V_DIM // GLA_HEADS
GLA_GATE_RANK = 16
GLA_GATE_TEMP = 16.0

MOE_GROUPS = 4
MOE_EXPERTS_PER_GROUP = 8
MOE_EXPERTS = MOE_GROUPS * MOE_EXPERTS_PER_GROUP
MOE_TOP_K = 2

AB_MAIN = POOL_DIM + 4 * GDN_DIM
CD_SPLITS = (FOURIER_DIM, FOURIER_DIM + GLA_K_DIM, FOURIER_DIM + 2 * GLA_K_DIM,
             FOURIER_DIM + 2 * GLA_K_DIM + GLA_V_DIM, FOURIER_DIM + 2 * GLA_K_DIM + 2 * GLA_V_DIM)
CD_MAIN = FOURIER_DIM + 2 * GLA_K_DIM + 2 * GLA_V_DIM

LANES = 128
SUBLANES = 8
VMEM_LIMIT = 48 * 1024 * 1024
MOE_UP_VMEM_LIMIT = 56 * 1024 * 1024
MOE_BLOCK_ROWS = 256
ROW_BLOCK = 256
DENSE_TM = 1024
DENSE_TN = 512
IN_PROJ_TN = 768
NORM_TM = 512
GDN_PREP_HEADS = 4
GDN_SCAN_HEADS = 8


def _cparams(*sem):
    return pltpu.CompilerParams(dimension_semantics=sem, vmem_limit_bytes=VMEM_LIMIT)


def _dot(a, b):
    return jnp.dot(a, b, preferred_element_type=jnp.float32)


def _dot_nt(a, b):
    return lax.dot_general(a, b, (((1,), (1,)), ((), ())), preferred_element_type=jnp.float32)


def _dot_tn(a, b):
    return lax.dot_general(a, b, (((0,), (0,)), ((), ())), preferred_element_type=jnp.float32)


def _bf(x):
    return x.astype(jnp.bfloat16)


def _split3(x):
    hi = _bf(x)
    r1 = x - hi.astype(jnp.float32)
    mid = _bf(r1)
    lo = _bf(r1 - mid.astype(jnp.float32))
    return hi, mid, lo


def _mm_body(*refs, n_pairs, has_res):
    o_ref = refs[-1]
    acc = _dot(refs[0][...], refs[1][...])
    for p in range(1, n_pairs):
        acc = acc + _dot(refs[2 * p][...], refs[2 * p + 1][...])
    if has_res:
        res_ref, gate_ref = refs[2 * n_pairs], refs[2 * n_pairs + 1]
        acc = res_ref[...] + gate_ref[0] * acc
    o_ref[...] = acc.astype(o_ref.dtype)


def dense_matmul(pairs, out_dtype=jnp.float32, tm=DENSE_TM, tn=DENSE_TN, m=None, n=None, w_row_block=0, residual=None,
                 gates=None, gate_index=None, out_shape=None, out_index=None):
    m = pairs[0][0].shape[0] if m is None else m
    n = pairs[0][1].shape[1] if n is None else n
    tm = min(tm, m)
    tn = min(tn, n)
    assert m % tm == 0 and n % tn == 0, (m, n, tm, tn)
    in_specs, args = [], []
    for pair in pairs:
        x, w = pair[:2]
        row_block = pair[2] if len(pair) > 2 else w_row_block
        k = x.shape[1]
        assert w.shape[0] % k == 0
        in_specs += [pl.BlockSpec((tm, k), lambda i, j: (i, 0)),
                     pl.BlockSpec((k, tn), functools.partial(lambda i, j, rb: (rb, j), rb=row_block))]
        args += [x, w]
    if residual is not None:
        in_specs += [pl.BlockSpec((tm, tn), lambda i, j: (i, j)),
                     pl.BlockSpec((1, 1, tn), lambda i, j: (gate_index(i), 0, j))]
        args += [residual, gates]
    return pl.pallas_call(
        functools.partial(_mm_body, n_pairs=len(pairs), has_res=residual is not None),
        out_shape=jax.ShapeDtypeStruct((m, n) if out_shape is None else out_shape, out_dtype),
        grid=(m // tm, n // tn),
        in_specs=in_specs,
        out_specs=pl.BlockSpec((tm, tn), (lambda i, j: (i, j)) if out_index is None else out_index),
        compiler_params=_cparams("parallel", "parallel"),
        name="dense_matmul",
    )(*args)


def _pad_cols(w, mult=LANES):
    pad = (-w.shape[1]) % mult
    return jnp.pad(w, ((0, 0), (0, pad))) if pad else w


HI16 = 0xFFFF0000


def _pack_bf16_halves(y):
    half = y.shape[1] // 2
    u = pltpu.bitcast(_bf(y).astype(jnp.float32), jnp.uint32)
    return lax.shift_right_logical(u[:, :half], jnp.uint32(16)) | (u[:, half:] & jnp.uint32(HI16))


def _unpack_bf16_halves(u):
    lo = pltpu.bitcast(lax.shift_left(u, jnp.uint32(16)), jnp.float32)
    hi = pltpu.bitcast(u & jnp.uint32(HI16), jnp.float32)
    return _bf(lo), _bf(hi)


def _norm_mod_body(x_ref, nw_ref, shift_ref, scale_ref, o_ref, *, packed):
    x = x_ref[...]
    y = x * lax.rsqrt(jnp.mean(x * x, axis=-1, keepdims=True) + NORM_EPS) * nw_ref[...]
    y = y * (1.0 + scale_ref[0]) + shift_ref[0]
    o_ref[...] = _pack_bf16_halves(y) if packed else y.astype(o_ref.dtype)


def norm_modulate(x, norm_w, shift, scale, row_index, m=None, tm=NORM_TM, packed=False):
    m = x.shape[0] if m is None else m
    d = x.shape[1]
    vec = pl.BlockSpec((1, 1, d), lambda i: (row_index(i), 0, 0))
    d_out = d // 2 if packed else d
    return pl.pallas_call(
        functools.partial(_norm_mod_body, packed=packed),
        out_shape=jax.ShapeDtypeStruct((m, d_out), jnp.uint32 if packed else jnp.bfloat16),
        grid=(m // tm,),
        in_specs=[pl.BlockSpec((tm, d), lambda i: (i, 0)), pl.BlockSpec((1, d), lambda i: (0, 0)), vec, vec],
        out_specs=pl.BlockSpec((tm, d_out), lambda i: (i, 0)),
        compiler_params=_cparams("parallel"),
        name="norm_modulate",
    )(x, norm_w.reshape(1, d), shift, scale)


def _gdn_gates_body(tail_ref, par_ref, o_ref, ot_ref, *, n_heads, chunk):
    t = tail_ref[...]
    rows = t.shape[0]
    lane = lax.broadcasted_iota(jnp.int32, t.shape, 1)
    a_row = par_ref[0:1, :]
    dtb_row = par_ref[1:2, :]
    beta = jax.nn.sigmoid(t)
    z = t + dtb_row
    g = a_row * (jnp.maximum(z, 0.0) + jnp.log1p(jnp.exp(-jnp.abs(z))))
    ri = lax.broadcasted_iota(jnp.int32, (rows, rows), 0)
    ci = lax.broadcasted_iota(jnp.int32, (rows, rows), 1)
    shift = chunk.bit_length() - 1
    same = (ri >> shift) == (ci >> shift)
    pre = _bf(jnp.where(same & (ci <= ri), 1.0, 0.0))
    suf = _bf(jnp.where(same & (ci >= ri), 1.0, 0.0))
    ones = _bf(jnp.where(same, 1.0, 0.0))
    parts = _split3(g)
    cf = _dot(pre, parts[0]) + _dot(pre, parts[1]) + _dot(pre, parts[2])
    cb = _dot(suf, parts[0]) + _dot(suf, parts[1]) + _dot(suf, parts[2])
    tot = _dot(ones, parts[0]) + _dot(ones, parts[1]) + _dot(ones, parts[2])
    main = jnp.where(lane < 2 * n_heads, beta,
                     jnp.where(lane < 3 * n_heads, cf, jnp.where(lane < 4 * n_heads, cb, 0.0)))
    o_ref[:, :LANES] = main
    o_ref[:, LANES:] = tot
    ot_ref[...] = main.T


def gdn_gates(proj, tail_col0, a_log, dt_bias, n_heads):
    t = proj.shape[0]
    assert tail_col0 % LANES == 0
    par = jnp.zeros((SUBLANES, LANES), jnp.float32)
    par = par.at[0, 2 * n_heads:4 * n_heads].set(-jnp.exp(a_log.astype(jnp.float32)).reshape(-1))
    par = par.at[1, 2 * n_heads:4 * n_heads].set(dt_bias.astype(jnp.float32).reshape(-1))
    return pl.pallas_call(
        functools.partial(_gdn_gates_body, n_heads=n_heads, chunk=GDN_CHUNK),
        out_shape=(jax.ShapeDtypeStruct((t, 2 * LANES), jnp.float32),
                   jax.ShapeDtypeStruct((t // ROW_BLOCK * LANES, ROW_BLOCK), jnp.float32)),
        grid=(t // ROW_BLOCK,),
        in_specs=[pl.BlockSpec((ROW_BLOCK, LANES), lambda i: (i, tail_col0 // LANES)),
                  pl.BlockSpec((SUBLANES, LANES), lambda i: (0, 0))],
        out_specs=(pl.BlockSpec((ROW_BLOCK, 2 * LANES), lambda i: (i, 0)),
                   pl.BlockSpec((LANES, ROW_BLOCK), lambda i: (i, 0))),
        compiler_params=_cparams("parallel"),
        name="gdn_gates",
    )(proj, par)


def _gdn_prep_body(q_ref, k_ref, v_ref, qp_ref, kp_ref, vp_ref, qn_ref, kn_ref, vn_ref,
                   wq_ref, wk_ref, wv_ref, gt_ref, gtt_ref,
                   u_ref, w_ref, qd_ref, ke_ref, at_ref, ge_ref, *, heads, n_heads, blocks_per_seq, n_lat_blocks):
    rb = pl.program_id(0)
    hg = pl.program_id(1)
    n = ROW_BLOCK
    c = GDN_CHUNK
    n_ch = n // c
    cs = c.bit_length() - 1
    hd = GDN_HEAD_DIM
    is_ctx = rb >= n_lat_blocks
    first = jnp.logical_or(is_ctx, rb % blocks_per_seq == 0)
    last = jnp.logical_or(is_ctx, rb % blocks_per_seq == blocks_per_seq - 1)
    m_prev = jnp.where(first, 0.0, 1.0)
    m_next = jnp.where(last, 0.0, 1.0)

    def conv_silu(cur_ref, prev_ref, next_ref, cw_ref):
        ext = jnp.concatenate([prev_ref[...] * m_prev, cur_ref[...], next_ref[...] * m_next], axis=0)
        acc = None
        for j in range(GDN_CONV):
            lo = SUBLANES - GDN_CONV // 2 + j
            term = ext[lo:lo + ROW_BLOCK, :] * cw_ref[j:j + 1, :]
            acc = term if acc is None else acc + term
        return acc * jax.nn.sigmoid(acc)

    qc = conv_silu(q_ref, qp_ref, qn_ref, wq_ref)
    kc = conv_silu(k_ref, kp_ref, kn_ref, wk_ref)
    vc = conv_silu(v_ref, vp_ref, vn_ref, wv_ref)

    gt_lo = gt_ref[:, :LANES]
    gt_hi = gt_ref[:, LANES:]
    lane = lax.broadcasted_iota(jnp.int32, (n, LANES), 1)
    blk16 = _bf(jnp.where((lax.broadcasted_iota(jnp.int32, (n, n), 0) >> cs)
                          == (lax.broadcasted_iota(jnp.int32, (n, n), 1) >> cs), 1.0, 0.0))
    wr = lax.broadcasted_iota(jnp.int32, (c, n), 0)
    wc = lax.broadcasted_iota(jnp.int32, (c, n), 1)
    eye_wide = jnp.where(wr == (wc & (c - 1)), 1.0, 0.0)
    sr = lax.broadcasted_iota(jnp.int32, (c, LANES), 0)
    sc = lax.broadcasted_iota(jnp.int32, (c, LANES), 1)
    slab_masks = []
    for odd in range(LANES // c):
        rel = sc - odd * c
        own = (sc >> cs) == odd
        slab_masks.append(((own & (rel <= sr), own & (rel < sr)), (own & (rel >= sr), own & (rel > sr))))

    def tall(wide):
        return jnp.concatenate([_bf(wide)] * n_ch, axis=0) * blk16

    def column(src, idx):
        return jnp.sum(jnp.where(lane == idx, src, 0.0), axis=1, keepdims=True)

    per_tile = LANES // c
    chains = []
    for gi in range(heads):
        h = hg * heads + gi
        sl = slice(gi * hd, (gi + 1) * hd)
        qh, kh, v = qc[:, sl], kc[:, sl], vc[:, sl]
        q = qh * lax.rsqrt(jnp.sum(qh * qh, axis=-1, keepdims=True) + NORM_EPS) * (hd ** -0.5)
        k = kh * lax.rsqrt(jnp.sum(kh * kh, axis=-1, keepdims=True) + NORM_EPS)
        k16 = _bf(k)
        qk = _dot_nt(_bf(q), k16)
        kk = _dot_nt(k16, k16)
        for d in range(2):
            beta = column(gt_lo, d * n_heads + h)
            gc = column(gt_lo, (2 + d) * n_heads + h)
            tot = column(gt_hi, (2 + d) * n_heads + h)
            gc_row = gtt_ref[pl.ds((2 + d) * n_heads + h, 1), :]
            p_tiles = []
            for j in range(n_ch):
                rows = slice(j * c, (j + 1) * c)
                cols = slice((j // per_tile) * LANES, (j // per_tile + 1) * LANES)
                incl, strict = slab_masks[j % per_tile][d]
                decay = jnp.where(incl, jnp.exp(jnp.where(incl, gc[rows] - gc_row[:, cols], 0.0)), 0.0)
                a = jnp.where(strict, (beta[rows] * kk[rows, cols]) * decay, 0.0)
                if j % per_tile == 0:
                    p_tiles.append(-a)
                else:
                    p_tiles[-1] = p_tiles[-1] - a
                lo = (j % per_tile) * c
                at_ref[d, rows, gi * c:(gi + 1) * c] = _bf((qk[rows, cols] * decay)[:, lo:lo + c])
                ge_ref[d, j * SUBLANES:(j + 1) * SUBLANES, sl] = jnp.broadcast_to(
                    jnp.exp(tot[j * c:j * c + SUBLANES]), (SUBLANES, hd))
            p = jnp.concatenate(p_tiles, axis=1)
            eg = jnp.exp(gc)
            kbeta = k * beta
            rhs = _bf(jnp.concatenate([v * beta, kbeta * eg], axis=1))
            qd_ref[d, :, sl] = _bf(q * eg)
            ke_ref[d, :, sl] = _bf(k * jnp.exp(tot - gc))
            chains.append([d, sl, p, eye_wide + p, tall(p), rhs])

    span = 2
    while span < c:
        for ch in chains:
            ch[2] = _dot(_bf(ch[2]), ch[4])
            ch[4] = tall(ch[2])
            ch[3] = ch[3] + _dot(_bf(ch[3]), ch[4])
        span *= 2
    for d, sl, _, t, _, rhs in chains:
        sol = _dot(tall(t), rhs)
        u_ref[d, :, sl] = sol[:, :hd]
        w_ref[d, :, sl] = _bf(sol[:, hd:])


def gdn_prepare(proj, gates, gates_t, conv_w, n_heads, qkv_col0, blocks_per_seq, n_lat_blocks):
    t = proj.shape[0]
    hd = GDN_HEAD_DIM
    g = GDN_PREP_HEADS
    gw = g * hd
    dim = n_heads * hd
    nb = t // ROW_BLOCK
    cb0 = qkv_col0 // gw
    per = dim // gw
    halo_per_block = ROW_BLOCK // SUBLANES

    def cur(off):
        return pl.BlockSpec((ROW_BLOCK, gw), lambda rb, hg: (rb, cb0 + off * per + hg))

    def prev(off):
        return pl.BlockSpec((SUBLANES, gw),
                            lambda rb, hg: (jnp.maximum(rb * halo_per_block - 1, 0), cb0 + off * per + hg))

    def nxt(off):
        return pl.BlockSpec((SUBLANES, gw),
                            lambda rb, hg: (jnp.minimum((rb + 1) * halo_per_block, nb * halo_per_block - 1),
                                            cb0 + off * per + hg))

    def cw(off):
        return pl.BlockSpec((GDN_CONV, gw), lambda rb, hg: (0, off * per + hg))

    out_tok = lambda width: pl.BlockSpec((2, ROW_BLOCK, width), lambda rb, hg: (0, rb, hg))
    outs = pl.pallas_call(
        functools.partial(_gdn_prep_body, heads=g, n_heads=n_heads, blocks_per_seq=blocks_per_seq,
                          n_lat_blocks=n_lat_blocks),
        out_shape=(jax.ShapeDtypeStruct((2, t, dim), jnp.float32),
                   jax.ShapeDtypeStruct((2, t, dim), jnp.bfloat16),
                   jax.ShapeDtypeStruct((2, t, dim), jnp.bfloat16),
                   jax.ShapeDtypeStruct((2, t, dim), jnp.bfloat16),
                   jax.ShapeDtypeStruct((2, t, n_heads * GDN_CHUNK), jnp.bfloat16),
                   jax.ShapeDtypeStruct((2, t // SUBLANES, dim), jnp.float32)),
        grid=(nb, n_heads // g),
        in_specs=[cur(0), cur(1), cur(2), prev(0), prev(1), prev(2), nxt(0), nxt(1), nxt(2),
                  cw(0), cw(1), cw(2), pl.BlockSpec((ROW_BLOCK, 2 * LANES), lambda rb, hg: (rb, 0)),
                  pl.BlockSpec((LANES, ROW_BLOCK), lambda rb, hg: (rb, 0))],
        out_specs=(out_tok(gw), out_tok(gw), out_tok(gw), out_tok(gw), out_tok(g * GDN_CHUNK),
                   pl.BlockSpec((2, ROW_BLOCK // SUBLANES, gw), lambda rb, hg: (0, rb, hg))),
        compiler_params=_cparams("parallel", "parallel"),
        name="gdn_prepare",
    )(proj, proj, proj, proj, proj, proj, proj, proj, proj, conv_w, conv_w, conv_w, gates, gates_t)
    return outs


def _gdn_scan_body(uf, wf, qf, kf, af, gf, ub, wb, qb, kb, ab, gb, of_ref, ob_ref, s_ref, *, heads):
    t = pl.program_id(2)
    c = GDN_CHUNK
    hd = GDN_HEAD_DIM
    n_ch = ROW_BLOCK // c

    @pl.when(t == 0)
    def _():
        s_ref[...] = jnp.zeros_like(s_ref)

    pw = 2 * hd
    s_mask = ((lax.broadcasted_iota(jnp.int32, (pw, pw), 0) >> (hd.bit_length() - 1))
              == (lax.broadcasted_iota(jnp.int32, (pw, pw), 1) >> (hd.bit_length() - 1)))
    v_mask = ((lax.broadcasted_iota(jnp.int32, (2 * c, pw), 0) >> (c.bit_length() - 1))
              == (lax.broadcasted_iota(jnp.int32, (2 * c, pw), 1) >> (hd.bit_length() - 1)))
    views = ((uf, wf, qf, kf, af, gf, of_ref), (ub, wb, qb, kb, ab, gb, ob_ref))
    chains = [(d, p) for d in range(2) for p in range(heads // 2)]
    for step in range(n_ch):
        def locate(d, p):
            ch = step if d == 0 else n_ch - 1 - step
            return views[d], ch, slice(ch * c, (ch + 1) * c), slice(p * pw, (p + 1) * pw)

        state, r_all, v_all = {}, {}, {}
        for d, p in chains:
            (u_r, w_r, q_r, k_r, a_r, g_r, o_r), ch, rows, sl = locate(d, p)
            s = s_ref[d, p]
            state[d, p] = s
            s_diag = _bf(jnp.where(s_mask, jnp.concatenate([s, s], axis=0), 0.0))
            r_all[d, p] = _dot(jnp.concatenate([w_r[0, rows, sl], q_r[0, rows, sl]], axis=0), s_diag)
        for d, p in chains:
            (u_r, w_r, q_r, k_r, a_r, g_r, o_r), ch, rows, sl = locate(d, p)
            v_new = u_r[0, rows, sl] - r_all[d, p][:c]
            v_all[d, p] = _bf(jnp.where(v_mask, jnp.concatenate([v_new, v_new], axis=0), 0.0))
        for d, p in chains:
            (u_r, w_r, q_r, k_r, a_r, g_r, o_r), ch, rows, sl = locate(d, p)
            o_r[0, rows, sl] = r_all[d, p][c:] + _dot(a_r[0, rows, p * 2 * c:(p + 1) * 2 * c], v_all[d, p])
        for d, p in chains:
            (u_r, w_r, q_r, k_r, a_r, g_r, o_r), ch, rows, sl = locate(d, p)
            k_stack = jnp.concatenate([k_r[0, rows, p * pw:p * pw + hd], k_r[0, rows, p * pw + hd:(p + 1) * pw]],
                                      axis=0)
            ge = g_r[0, ch * SUBLANES:ch * SUBLANES + 1, sl]
            s_ref[d, p] = state[d, p] * ge + _dot_tn(k_stack, v_all[d, p])


def gdn_scan(u, w, qd, ke, at, ge, n_heads, n_batch, blocks_per_seq):
    t = u.shape[1]
    hd = GDN_HEAD_DIM
    g = GDN_SCAN_HEADS
    gw = g * hd
    n_lat_blocks = n_batch * blocks_per_seq

    def fwd(b, s):
        return jnp.where(s == 0, n_lat_blocks + b, b * blocks_per_seq + s - 1)

    def bwd(b, s):
        return jnp.where(s == 0, n_lat_blocks + b, b * blocks_per_seq + blocks_per_seq - s)

    def tok(d, width, blk):
        return pl.BlockSpec((1, ROW_BLOCK, width), lambda b, hg, s: (d, blk(b, s), hg))

    def gsp(d, blk):
        return pl.BlockSpec((1, ROW_BLOCK // SUBLANES, gw), lambda b, hg, s: (d, blk(b, s), hg))

    in_specs = []
    for d, blk in ((0, fwd), (1, bwd)):
        in_specs += [tok(d, gw, blk), tok(d, gw, blk), tok(d, gw, blk), tok(d, gw, blk),
                     tok(d, g * GDN_CHUNK, blk), gsp(d, blk)]
    o_f, o_b = pl.pallas_call(
        functools.partial(_gdn_scan_body, heads=g),
        out_shape=(jax.ShapeDtypeStruct((1, t, n_heads * hd), jnp.float32),) * 2,
        grid=(n_batch, n_heads // g, blocks_per_seq + 1),
        in_specs=in_specs,
        out_specs=(pl.BlockSpec((1, ROW_BLOCK, gw), lambda b, hg, s: (0, fwd(b, s), hg)),
                   pl.BlockSpec((1, ROW_BLOCK, gw), lambda b, hg, s: (0, bwd(b, s), hg))),
        scratch_shapes=[pltpu.VMEM((2, g // 2, hd, 2 * hd), jnp.float32)],
        compiler_params=_cparams("parallel", "parallel", "arbitrary"),
        name="gdn_scan",
    )(u, w, qd, ke, at, ge, u, w, qd, ke, at, ge)
    return o_f[0], o_b[0]


GATED_NORM_COLS = 1024


def _gated_norm_body(of_ref, ob_ref, z_ref, nw_ref, lead_ref, y_ref, *, hd):
    j = pl.program_id(1)

    @pl.when(j == 0)
    def _():
        y_ref[...] = lead_ref[...]

    @pl.when(j > 0)
    def _():
        for h in range(GATED_NORM_COLS // hd):
            sl = slice(h * hd, (h + 1) * hd)
            o = of_ref[:, sl] + ob_ref[:, sl]
            y = o * lax.rsqrt(jnp.mean(o * o, axis=-1, keepdims=True) + NORM_EPS) * nw_ref[...]
            z = z_ref[:, sl]
            y_ref[:, sl] = (y * (z * jax.nn.sigmoid(z))).astype(y_ref.dtype)


def gated_head_norm(o_f, o_b, proj, z_col0, norm_w, hd, lead):
    m, cw = lead.shape
    dim = o_f.shape[1]
    assert cw == GATED_NORM_COLS and z_col0 % cw == 0 and dim % cw == 0 and cw % hd == 0

    def col(j):
        return jnp.maximum(j - 1, 0)

    return pl.pallas_call(
        functools.partial(_gated_norm_body, hd=hd),
        out_shape=jax.ShapeDtypeStruct((m, cw + dim), jnp.bfloat16),
        grid=(m // ROW_BLOCK, 1 + dim // cw),
        in_specs=[pl.BlockSpec((ROW_BLOCK, cw), lambda i, j: (i, col(j))),
                  pl.BlockSpec((ROW_BLOCK, cw), lambda i, j: (i, col(j))),
                  pl.BlockSpec((ROW_BLOCK, cw), lambda i, j: (i, z_col0 // cw + col(j))),
                  pl.BlockSpec((1, hd), lambda i, j: (0, 0)),
                  pl.BlockSpec((ROW_BLOCK, cw), lambda i, j: (i, 0))],
        out_specs=pl.BlockSpec((ROW_BLOCK, cw), lambda i, j: (i, j)),
        compiler_params=_cparams("parallel", "parallel"),
        name="gated_head_norm",
    )(o_f, o_b, proj, norm_w.reshape(1, hd), lead)


def _pool_body(u_ref, w_ref, sc_ref, y_ref, *, n_lat_blocks):
    rb = pl.program_id(0)
    is_ctx = rb >= n_lat_blocks
    row_len = jnp.where(is_ctx, CTX_LEN, GRID_W)
    shift = jnp.where(is_ctx, CTX_LEN.bit_length() - 1, GRID_W.bit_length() - 1)
    n = ROW_BLOCK
    ri = lax.broadcasted_iota(jnp.int32, (n, n), 0)
    ci = lax.broadcasted_iota(jnp.int32, (n, n), 1)
    same = (ri >> shift) == (ci >> shift)
    pos_r = ri & (row_len - 1)
    pos_c = ci & (row_len - 1)
    rcol = lax.broadcasted_iota(jnp.int32, (n, 1), 0) & (row_len - 1)
    gd = POOL_GROUP_DIM
    for g, win in enumerate(POOL_WINDOWS):
        half = win // 2
        lo = jnp.maximum(pos_r - half, 0)
        hi = jnp.minimum(pos_r + half - 1, row_len - 1)
        band = _bf(jnp.where(same & (pos_c >= lo) & (pos_c <= hi), 1.0, 0.0))
        cnt = (jnp.minimum(rcol + half - 1, row_len - 1) - jnp.maximum(rcol - half, 0) + 1).astype(jnp.float32)
        u = u_ref[:, g * gd:(g + 1) * gd]
        parts = _split3(u)
        win_sum = _dot(band, parts[0]) + _dot(band, parts[1]) + _dot(band, parts[2])
        dlt = win_sum / cnt - u
        y = _dot(_bf(dlt), _bf(w_ref[g])) * sc_ref[:, g * gd:(g + 1) * gd]
        y_ref[:, g * gd:(g + 1) * gd] = y.astype(y_ref.dtype)


def pool_mix(proj, pool_w, pool_scale, n_lat_blocks):
    t = proj.shape[0]
    return pl.pallas_call(
        functools.partial(_pool_body, n_lat_blocks=n_lat_blocks),
        out_shape=jax.ShapeDtypeStruct((t, POOL_DIM), jnp.bfloat16),
        grid=(t // ROW_BLOCK,),
        in_specs=[pl.BlockSpec((ROW_BLOCK, POOL_DIM), lambda i: (i, 0)),
                  pl.BlockSpec((POOL_GROUPS, POOL_GROUP_DIM, POOL_GROUP_DIM), lambda i: (0, 0, 0)),
                  pl.BlockSpec((1, POOL_DIM), lambda i: (0, 0))],
        out_specs=pl.BlockSpec((ROW_BLOCK, POOL_DIM), lambda i: (i, 0)),
        compiler_params=_cparams("parallel"),
        name="pool_mix",
    )(proj, pool_w, pool_scale.reshape(1, POOL_DIM))


def _moe_up_body(be_ref, first_ref, nused_ref, src_ref, src_next_ref, f_ref, w1_ref, w3_ref, h_ref,
                 w1_bf, w3_bf, rows_ref, sem):
    del be_ref
    i = pl.program_id(0)
    bm = rows_ref.shape[1]
    n_used = nused_ref[0]

    def row_copy(slot, r, src_row):
        return pltpu.make_async_copy(f_ref.at[pl.ds(src_row, 1), :], rows_ref.at[slot, pl.ds(r, 1), :], sem.at[slot])

    def issue(slot, idx_ref):
        for r in range(bm):
            row_copy(slot, r, idx_ref[0, 0, r]).start()

    @pl.when(jnp.logical_and(i == 0, n_used > 0))
    def _():
        issue(0, src_ref)

    @pl.when(i + 1 < n_used)
    def _():
        issue(lax.rem(i + 1, 2), src_next_ref)

    @pl.when(first_ref[i] == 1)
    def _():
        w1_bf[...] = _bf(w1_ref[0])
        w3_bf[...] = _bf(w3_ref[0])

    @pl.when(i < n_used)
    def _():
        slot = lax.rem(i, 2)

        def drain(r, carry):
            row_copy(slot, r, 0).wait()
            return carry
        lax.fori_loop(0, bm, drain, 0, unroll=8)
        x_lo, x_hi = _unpack_bf16_halves(rows_ref[slot])
        half = x_lo.shape[1]
        a = _dot(x_lo, w1_bf[:half, :]) + _dot(x_hi, w1_bf[half:, :])
        b = _dot(x_lo, w3_bf[:half, :]) + _dot(x_hi, w3_bf[half:, :])
        h_ref[...] = _bf(a * jax.nn.sigmoid(a) * b)

    @pl.when(i >= nused_ref[0])
    def _():
        h_ref[...] = jnp.zeros_like(h_ref)


def _moe_down_body(be_ref, first_ref, nused_ref, h_ref, w2_ref, o_ref, w2_bf):
    del be_ref
    i = pl.program_id(0)

    @pl.when(first_ref[i] == 1)
    def _():
        w2_bf[...] = _bf(w2_ref[0])

    @pl.when(i < nused_ref[0])
    def _():
        o_ref[...] = _dot(h_ref[...], w2_bf[...]).astype(o_ref.dtype)

    @pl.when(i >= nused_ref[0])
    def _():
        o_ref[...] = jnp.zeros_like(o_ref)


def moe_expert_blocks(f, src, block_expert, block_first, n_used, w1, w3, w2):
    d = w1.shape[1]
    bm = MOE_BLOCK_ROWS
    nb = src.shape[0] // bm
    p = nb * bm
    hid = w1.shape[-1]
    src_blocks = src.reshape(nb, 1, bm)
    up_spec = pltpu.PrefetchScalarGridSpec(
        num_scalar_prefetch=3,
        grid=(nb,),
        in_specs=[pl.BlockSpec((1, 1, bm), lambda i, be, fi, nu: (i, 0, 0), memory_space=pltpu.SMEM),
                  pl.BlockSpec((1, 1, bm), lambda i, be, fi, nu: (jnp.minimum(i + 1, nb - 1), 0, 0),
                               memory_space=pltpu.SMEM),
                  pl.BlockSpec(memory_space=pl.ANY),
                  pl.BlockSpec((1, d, hid), lambda i, be, fi, nu: (be[i], 0, 0)),
                  pl.BlockSpec((1, d, hid), lambda i, be, fi, nu: (be[i], 0, 0))],
        out_specs=pl.BlockSpec((bm, hid), lambda i, be, fi, nu: (i, 0)),
        scratch_shapes=[pltpu.VMEM((d, hid), jnp.bfloat16), pltpu.VMEM((d, hid), jnp.bfloat16),
                        pltpu.VMEM((2, bm, d // 2), jnp.uint32), pltpu.SemaphoreType.DMA((2,))],
    )
    hbuf = pl.pallas_call(
        _moe_up_body,
        out_shape=jax.ShapeDtypeStruct((p, hid), jnp.bfloat16),
        grid_spec=up_spec,
        compiler_params=pltpu.CompilerParams(dimension_semantics=("arbitrary",), vmem_limit_bytes=MOE_UP_VMEM_LIMIT),
        name="moe_up",
    )(block_expert, block_first, n_used, src_blocks, src_blocks, f, w1, w3)
    down_spec = pltpu.PrefetchScalarGridSpec(
        num_scalar_prefetch=3,
        grid=(nb,),
        in_specs=[pl.BlockSpec((bm, hid), lambda i, be, fi, nu: (i, 0)),
                  pl.BlockSpec((1, hid, d), lambda i, be, fi, nu: (be[i], 0, 0))],
        out_specs=pl.BlockSpec((bm, d), lambda i, be, fi, nu: (i, 0)),
        scratch_shapes=[pltpu.VMEM((hid, d), jnp.bfloat16)],
    )
    return pl.pallas_call(
        _moe_down_body,
        out_shape=jax.ShapeDtypeStruct((p, d), jnp.float32),
        grid_spec=down_spec,
        compiler_params=_cparams("arbitrary"),
        name="moe_down",
    )(block_expert, block_first, n_used, hbuf, w2)


def _moe_route_body(h_ref, w_ref, b_ref, ids_ref, wgt_ref, cnt_ref, run_ref):
    i = pl.program_id(0)

    @pl.when(i == 0)
    def _():
        run_ref[...] = jnp.zeros_like(run_ref)

    n = h_ref.shape[0]
    ng, ne = MOE_GROUPS, MOE_EXPERTS_PER_GROUP
    h_lo, h_hi = _unpack_bf16_halves(h_ref[...])
    half = h_lo.shape[1]
    logits = b_ref[...]
    for part in range(3):
        logits = logits + _dot(h_lo, w_ref[part, :half, :]) + _dot(h_hi, w_ref[part, half:, :])
    lane = lax.broadcasted_iota(jnp.int32, logits.shape, 1)
    neg = jnp.float32(-jnp.inf)

    lane_f = lane.astype(jnp.float32)

    def first_argmax(vals):
        top = jnp.max(vals, axis=1, keepdims=True)
        return top, jnp.min(jnp.where(vals == top, lane_f, float(LANES)), axis=1, keepdims=True).astype(jnp.int32)

    gl = jnp.where(lane < ng, logits, neg)
    g_top, group = first_argmax(gl)
    p_group = 1.0 / jnp.sum(jnp.exp(gl - g_top), axis=1, keepdims=True)
    lo = ng + group * ne
    sel = jnp.where((lane >= lo) & (lane < lo + ne), logits, neg)
    v1, i1 = first_argmax(sel)
    v2, i2 = first_argmax(jnp.where(lane == i1, neg, sel))
    e21 = jnp.exp(v2 - v1)
    w1 = p_group / (1.0 + e21)
    w2 = p_group * e21 / (1.0 + e21)
    e1 = i1 - ng
    e2 = i2 - ng
    oh1 = jnp.where(lane == e1, 1.0, 0.0)
    oh2 = jnp.where(lane == e2, 1.0, 0.0)
    both = oh1 + oh2
    ri = lax.broadcasted_iota(jnp.int32, (n, n), 0)
    ci = lax.broadcasted_iota(jnp.int32, (n, n), 1)
    before = _dot(_bf(jnp.where(ci < ri, 1.0, 0.0)), _bf(both)) + run_ref[0:1, :]
    r1 = jnp.sum(oh1 * before, axis=1, keepdims=True)
    r2 = jnp.sum(oh2 * before, axis=1, keepdims=True)
    run_ref[...] = run_ref[...] + jnp.sum(both, axis=0, keepdims=True)
    ids_ref[...] = jnp.where(lane == 0, e1, jnp.where(lane == 1, e2, jnp.where(
        lane == 2, r1.astype(jnp.int32), jnp.where(lane == 3, r2.astype(jnp.int32), 0))))
    wgt_ref[...] = jnp.where(lane == 0, w1, jnp.where(lane == 1, w2, 0.0))
    cnt_ref[...] = run_ref[...].astype(jnp.int32)


def moe_route(h, group_w, group_b, expert_w, expert_b):
    t, d = h.shape[0], group_w.shape[0]
    w = jnp.concatenate([group_w, expert_w.transpose(1, 0, 2).reshape(d, MOE_EXPERTS)], axis=1).astype(jnp.float32)
    w = jnp.stack(_split3(_pad_cols(w)))
    b = _pad_cols(jnp.concatenate([group_b, expert_b.reshape(-1)])[None, :].astype(jnp.float32))
    tm = ROW_BLOCK
    return pl.pallas_call(
        _moe_route_body,
        out_shape=(jax.ShapeDtypeStruct((t, LANES), jnp.int32), jax.ShapeDtypeStruct((t, LANES), jnp.float32),
                   jax.ShapeDtypeStruct((SUBLANES, LANES), jnp.int32)),
        grid=(t // tm,),
        in_specs=[pl.BlockSpec((tm, d // 2), lambda i: (i, 0)), pl.BlockSpec((3, d, LANES), lambda i: (0, 0, 0)),
                  pl.BlockSpec((1, LANES), lambda i: (0, 0))],
        out_specs=(pl.BlockSpec((tm, LANES), lambda i: (i, 0)), pl.BlockSpec((tm, LANES), lambda i: (i, 0)),
                   pl.BlockSpec((SUBLANES, LANES), lambda i: (0, 0))),
        scratch_shapes=[pltpu.VMEM((SUBLANES, LANES), jnp.float32)],
        compiler_params=_cparams("arbitrary"),
        name="moe_route",
    )(h, w, b)


def hier_moe(h, group_w, group_b, expert_w, expert_b, w1, w3, w2, expert0):
    t = h.shape[0]
    ids, weight, counts = moe_route(h, group_w, group_b, expert_w, expert_b)
    expert = ids[:, :MOE_TOP_K].T.reshape(-1)
    rank = ids[:, MOE_TOP_K:2 * MOE_TOP_K].T.reshape(-1)
    counts = counts[0, :MOE_EXPERTS]
    tk = t * MOE_TOP_K
    bm = MOE_BLOCK_ROWS
    padded = (counts + bm - 1) // bm * bm
    e_idx = jnp.arange(MOE_EXPERTS)
    pad_end = jnp.sum(jnp.where(e_idx[None, :] <= e_idx[:, None], padded[None, :], 0), axis=1)
    pad_start = pad_end - padded
    dest = pad_start[expert] + rank
    n_blocks = -(-tk // bm) + MOE_EXPERTS
    token = jnp.tile(jnp.arange(t, dtype=jnp.int32), MOE_TOP_K)
    src = jnp.zeros((n_blocks * bm,), jnp.int32).at[dest].set(token)
    block_start = jnp.arange(n_blocks, dtype=jnp.int32) * bm
    block_expert = jnp.minimum(jnp.sum((pad_end[None, :] <= block_start[:, None]).astype(jnp.int32), axis=1),
                               MOE_EXPERTS - 1)
    block_first = jnp.concatenate([jnp.ones((1,), jnp.int32),
                                   (block_expert[1:] != block_expert[:-1]).astype(jnp.int32)])
    n_used = (pad_end[-1:] // bm).astype(jnp.int32)
    ybuf = moe_expert_blocks(h, src, block_expert + expert0, block_first, n_used, w1, w3, w2)
    return ybuf, dest.astype(jnp.int32), weight


GLA_SUB = 32
GLA_STEP = 64
GLA_SCAN_HEADS = 2


def _gla_prep_body(q_ref, k_ref, v_ref, lr_ref, wg_ref, gb_ref, qd_ref, ke_ref, at_ref, ge_ref, v16_ref):
    n = ROW_BLOCK
    c = GLA_STEP
    sb = GLA_SUB
    n_sb = n // sb
    q = q_ref[...] * (GLA_HEAD_K ** -0.5)
    k = k_ref[...]
    v16_ref[...] = _bf(v_ref[...])
    lr = lr_ref[...]
    ri = lax.broadcasted_iota(jnp.int32, (n, n), 0)
    ci = lax.broadcasted_iota(jnp.int32, (n, n), 1)
    cs = c.bit_length() - 1
    ss = sb.bit_length() - 1
    same_c = (ri >> cs) == (ci >> cs)
    same_s = (ri >> ss) == (ci >> ss)
    dirs = range(2)
    xs = [jnp.dot(lr, wg_ref[d], preferred_element_type=jnp.float32, precision=lax.Precision.HIGHEST) + gb_ref[d]
          for d in dirs]
    gks = [_split3((jnp.minimum(x, 0.0) - jnp.log1p(jnp.exp(-jnp.abs(x)))) * (1.0 / GLA_GATE_TEMP)) for x in xs]
    runs = [_bf(jnp.where(same_c & ((ci <= ri) if d == 0 else (ci >= ri)), 1.0, 0.0)) for d in dirs]
    gs = [_dot(runs[d], gks[d][0]) + _dot(runs[d], gks[d][1]) + _dot(runs[d], gks[d][2]) for d in dirs]
    locs, tots, totc = [], [], []
    for d in dirs:
        g = gs[d]
        off, tot_s, tot_c = [], [], []
        for b in range(n_sb):
            lo = b * sb
            inner = (b % 2 == 1) if d == 0 else (b % 2 == 0)
            edge = g[lo + sb - 1:lo + sb] if d == 0 else g[lo:lo + 1]
            if inner:
                prev = g[lo - 1:lo] if d == 0 else g[lo + sb:lo + sb + 1]
                off.append(jnp.broadcast_to(prev, (sb, prev.shape[1])))
                tot_s.append(jnp.broadcast_to(edge - prev, (sb, prev.shape[1])))
            else:
                off.append(jnp.zeros((sb, g.shape[1]), jnp.float32))
                tot_s.append(jnp.broadcast_to(edge, (sb, edge.shape[1])))
        for j in range(n // c):
            edge = g[j * c + c - 1:j * c + c] if d == 0 else g[j * c:j * c + 1]
            tot_c.append(jnp.broadcast_to(edge, (c, edge.shape[1])))
        locs.append(g - jnp.concatenate(off, axis=0))
        tots.append(jnp.concatenate(tot_s, axis=0))
        totc.append(jnp.concatenate(tot_c, axis=0))
    qd_loc = [_bf(q * jnp.exp(locs[d])) for d in dirs]
    m_diag = [_dot_nt(qd_loc[d], _bf(k * jnp.exp(-locs[d]))) for d in dirs]
    m_cross = [_dot_nt(qd_loc[d], _bf(k * jnp.exp(tots[d] - locs[d]))) for d in dirs]
    for d in dirs:
        tri = (ci <= ri) if d == 0 else (ci >= ri)
        nxt = ((ri >> ss) == (ci >> ss) + 1) if d == 0 else ((ri >> ss) + 1 == (ci >> ss))
        attn = jnp.where(same_s & tri, m_diag[d], 0.0) + jnp.where(same_c & nxt, m_cross[d], 0.0)
        qd_ref[d] = _bf(q * jnp.exp(gs[d]))
        ke_ref[d] = _bf(k * jnp.exp(totc[d] - gs[d]))
        for j in range(n // c):
            at_ref[d, 0, j * c:(j + 1) * c, :] = _bf(attn[j * c:(j + 1) * c, j * c:(j + 1) * c])
            ge_ref[d, j * SUBLANES:(j + 1) * SUBLANES, :] = jnp.exp(totc[d][j * c:j * c + SUBLANES])


def gla_prepare(proj, gate_up, gate_b):
    t = proj.shape[0]
    assert CD_MAIN % LANES == 0
    hk, hv = GLA_HEAD_K, GLA_HEAD_V
    wg = jnp.zeros((2, LANES, GLA_K_DIM), jnp.float32)
    for d in range(2):
        wg = wg.at[d, d * GLA_GATE_RANK:(d + 1) * GLA_GATE_RANK].set(gate_up[d].astype(jnp.float32))
    gb = gate_b.astype(jnp.float32).reshape(2, 1, GLA_K_DIM)
    q0, k0, v0 = CD_SPLITS[0] // hk, CD_SPLITS[1] // hk, CD_SPLITS[2] // hv
    assert CD_SPLITS[0] % hk == 0 and CD_SPLITS[1] % hk == 0 and CD_SPLITS[2] % hv == 0
    return pl.pallas_call(
        _gla_prep_body,
        out_shape=(jax.ShapeDtypeStruct((2, t, GLA_K_DIM), jnp.bfloat16),
                   jax.ShapeDtypeStruct((2, t, GLA_K_DIM), jnp.bfloat16),
                   jax.ShapeDtypeStruct((2, GLA_HEADS, t, GLA_STEP), jnp.bfloat16),
                   jax.ShapeDtypeStruct((2, t // SUBLANES, GLA_K_DIM), jnp.float32),
                   jax.ShapeDtypeStruct((t, GLA_V_DIM), jnp.bfloat16)),
        grid=(t // ROW_BLOCK, GLA_HEADS),
        in_specs=[pl.BlockSpec((ROW_BLOCK, hk), lambda rb, h: (rb, q0 + h)),
                  pl.BlockSpec((ROW_BLOCK, hk), lambda rb, h: (rb, k0 + h)),
                  pl.BlockSpec((ROW_BLOCK, hv), lambda rb, h: (rb, v0 + h)),
                  pl.BlockSpec((ROW_BLOCK, LANES), lambda rb, h: (rb, CD_MAIN // LANES)),
                  pl.BlockSpec((2, LANES, hk), lambda rb, h: (0, 0, h)),
                  pl.BlockSpec((2, 1, hk), lambda rb, h: (0, 0, h))],
        out_specs=(pl.BlockSpec((2, ROW_BLOCK, hk), lambda rb, h: (0, rb, h)),
                   pl.BlockSpec((2, ROW_BLOCK, hk), lambda rb, h: (0, rb, h)),
                   pl.BlockSpec((2, 1, ROW_BLOCK, GLA_STEP), lambda rb, h: (0, h, rb, 0)),
                   pl.BlockSpec((2, ROW_BLOCK // SUBLANES, hk), lambda rb, h: (0, rb, h)),
                   pl.BlockSpec((ROW_BLOCK, hv), lambda rb, h: (rb, h))),
        compiler_params=_cparams("parallel", "parallel"),
        name="gla_prepare",
    )(proj, proj, proj, proj, wg, gb)


def _gla_scan_body(qf, kf, af, gf, vf, qb, kb, ab, gb, vb, of_ref, ob_ref, s_ref):
    t = pl.program_id(2)
    c = GLA_STEP
    n_ch = ROW_BLOCK // c

    @pl.when(t == 0)
    def _():
        s_ref[...] = jnp.zeros_like(s_ref)

    hk, hv = GLA_HEAD_K, GLA_HEAD_V
    views = ((qf, kf, af, gf, vf, of_ref), (qb, kb, ab, gb, vb, ob_ref))
    chains = [(d, g) for d in range(2) for g in range(GLA_SCAN_HEADS)]
    for step in range(n_ch):
        chunk = [step, n_ch - 1 - step]
        rows = [slice(ch * c, (ch + 1) * c) for ch in chunk]
        st = {dg: s_ref[dg[0], dg[1]] for dg in chains}
        inter = {(d, g): _dot_nt(views[d][0][0, rows[d], g * hk:(g + 1) * hk], _bf(st[d, g])) for d, g in chains}
        intra = {(d, g): _dot(views[d][2][0, g, rows[d], :], views[d][4][rows[d], g * hv:(g + 1) * hv])
                 for d, g in chains}
        upd = {(d, g): _dot_tn(views[d][4][rows[d], g * hv:(g + 1) * hv], views[d][1][0, rows[d], g * hk:(g + 1) * hk])
               for d, g in chains}
        for d, g in chains:
            views[d][5][rows[d], g * hv:(g + 1) * hv] = inter[d, g] + intra[d, g]
            ge = views[d][3][0, chunk[d] * SUBLANES:chunk[d] * SUBLANES + 1, g * hk:(g + 1) * hk]
            s_ref[d, g] = st[d, g] * ge + upd[d, g]


def gla_scan(qd, ke, at, ge, v16, n_batch, blocks_per_seq):
    g = GLA_SCAN_HEADS
    hk, hv = g * GLA_HEAD_K, g * GLA_HEAD_V
    t = v16.shape[0]
    n_lat_blocks = n_batch * blocks_per_seq

    def fwd(b, s):
        return jnp.where(s == 0, n_lat_blocks + b, b * blocks_per_seq + s - 1)

    def bwd(b, s):
        return jnp.where(s == 0, n_lat_blocks + b, b * blocks_per_seq + blocks_per_seq - s)

    def out_blk(blk):
        return lambda b, h, s: (blk(b, s), h)

    def key_spec(d, blk, rows):
        return pl.BlockSpec((1, rows, hk), lambda b, h, s: (d, blk(b, s), h))

    def attn_spec(d, blk):
        return pl.BlockSpec((1, g, ROW_BLOCK, GLA_STEP), lambda b, h, s: (d, h, blk(b, s), 0))

    def val_spec(blk):
        return pl.BlockSpec((ROW_BLOCK, hv), lambda b, h, s: (blk(b, s), h))

    in_specs = []
    for d, blk in ((0, fwd), (1, bwd)):
        in_specs += [key_spec(d, blk, ROW_BLOCK), key_spec(d, blk, ROW_BLOCK), attn_spec(d, blk),
                     key_spec(d, blk, ROW_BLOCK // SUBLANES), val_spec(blk)]
    o_f, o_b = pl.pallas_call(
        _gla_scan_body,
        out_shape=(jax.ShapeDtypeStruct((t, GLA_V_DIM), jnp.float32),) * 2,
        grid=(n_batch, GLA_HEADS // g, blocks_per_seq + 1),
        in_specs=in_specs,
        out_specs=(pl.BlockSpec((ROW_BLOCK, hv), out_blk(fwd)), pl.BlockSpec((ROW_BLOCK, hv), out_blk(bwd))),
        scratch_shapes=[pltpu.VMEM((2, g, GLA_HEAD_V, GLA_HEAD_K), jnp.float32)],
        compiler_params=_cparams("parallel", "parallel", "arbitrary"),
        name="gla_scan",
    )(qd, ke, at, ge, v16, qd, ke, at, ge, v16)
    return o_f, o_b


def _fourier_in_body(x_ref, w_ref, o_ref):
    y = _dot(_bf(x_ref[...]), w_ref[0])
    half = y.shape[1] // 2
    o_ref[0] = _bf(y[:, :half])
    o_ref[1] = _bf(y[:, half:])


def dft_tables(n, nc):
    r = 64
    assert n % r == 0
    kk = jnp.arange(n, dtype=jnp.int32)[None, :]
    j1 = jnp.arange(n // r, dtype=jnp.int32)[:, None]
    j0 = jnp.arange(r, dtype=jnp.int32)[:, None]
    a = ((j1 * kk) % (n // r)).astype(jnp.float32) * (2.0 * jnp.pi / (n // r))
    b = ((j0 * kk) % n).astype(jnp.float32) * (2.0 * jnp.pi / n)
    ca, sa, cb, sb = jnp.cos(a)[:, None, :], jnp.sin(a)[:, None, :], jnp.cos(b)[None, :, :], jnp.sin(b)[None, :, :]
    scale = n ** -0.5
    cos_n = ((ca * cb - sa * sb) * scale).reshape(n, n).astype(jnp.bfloat16)
    neg_sin_n = ((sa * cb + ca * sb) * -scale).reshape(n, n).astype(jnp.bfloat16)
    cc = jnp.arange(nc, dtype=jnp.int32)
    ang = ((cc[:, None] * cc[None, :]) % nc).astype(jnp.float32) * (2.0 * jnp.pi / nc)
    return cos_n, neg_sin_n, jnp.cos(ang) * nc ** -0.5, jnp.sin(ang) * nc ** -0.5


def fourier_mix(proj, fourier_w, n_batch, n_lat):
    gd = FOURIER_GROUP_DIM
    ng = FOURIER_GROUPS
    cos_n, neg_sin_n, cos_c, sin_c = dft_tables(n_lat, gd)
    w_all = fourier_w.astype(jnp.float32).transpose(1, 0, 2).reshape(gd, ng * gd)
    folded = dense_matmul([(jnp.concatenate([cos_c, sin_c], axis=0), w_all)], tm=2 * gd)
    w2 = jnp.concatenate([folded[:gd].reshape(gd, ng, gd), folded[gd:].reshape(gd, ng, gd)], axis=2)
    w2 = w2.transpose(1, 0, 2).astype(jnp.bfloat16)
    tm = min(DENSE_TM, n_lat)
    tiles = n_lat // tm
    z = pl.pallas_call(
        _fourier_in_body,
        out_shape=jax.ShapeDtypeStruct((2, n_lat, n_batch * ng * gd), jnp.bfloat16),
        grid=(n_batch, ng, tiles),
        in_specs=[pl.BlockSpec((tm, gd), lambda b, g, i: (b * tiles + i, g)),
                  pl.BlockSpec((1, gd, 2 * gd), lambda b, g, i: (g, 0, 0))],
        out_specs=pl.BlockSpec((2, tm, gd), lambda b, g, i: (0, i, b * ng + g)),
        compiler_params=_cparams("parallel", "parallel", "parallel"),
        name="fourier_in",
    )(proj, w2)
    z = z.reshape(2 * n_lat, n_batch * ng * gd)
    tm2, tn = min(NORM_TM, n_lat), DENSE_TN
    per_b = ng * gd // tn
    return dense_matmul([(cos_n, z, 0), (neg_sin_n, z, 1)], out_dtype=jnp.bfloat16, tm=tm2, tn=tn,
                        out_shape=(n_batch * n_lat, ng * gd),
                        out_index=lambda i, j: ((j // per_b) * (n_lat // tm2) + i, j % per_b))


def modulation(c, c_ctx, mod_w, mod_b, layer):
    b, d = c.shape
    rows = jnp.concatenate([c, c_ctx[None, :], jnp.zeros((SUBLANES - b - 1, d), c.dtype)], axis=0)
    out = dense_matmul([(jax.nn.silu(rows), mod_w.reshape(-1, mod_w.shape[-1]))], tm=SUBLANES,
                       w_row_block=layer) + mod_b[layer]
    return tuple(m[:, None, :] for m in jnp.split(out, N_MOD, axis=-1))


COMBINE_COLS = 512


def _combine_body(dest_ref, dest_next_ref, x_ref, ybuf_ref, w_ref, gate_ref, nw_ref, o_ref, rows_ref, sem, *,
                  final_norm):
    i = pl.program_id(0)
    n_tiles = pl.num_programs(0)
    tm, d = x_ref.shape

    def row_copy(slot, r, k, src_row):
        return pltpu.make_async_copy(ybuf_ref.at[pl.ds(src_row, 1), :], rows_ref.at[slot, k, pl.ds(r, 1), :],
                                     sem.at[slot])

    def issue(slot, idx_ref):
        def body(r, carry):
            for k in range(MOE_TOP_K):
                row_copy(slot, r, k, idx_ref[0, 0, k * tm + r]).start()
            return carry
        lax.fori_loop(0, tm, body, 0, unroll=8)

    @pl.when(i == 0)
    def _():
        issue(0, dest_ref)

    @pl.when(i + 1 < n_tiles)
    def _():
        issue(lax.rem(i + 1, 2), dest_next_ref)

    slot = lax.rem(i, 2)

    def drain(r, carry):
        for k in range(MOE_TOP_K):
            row_copy(slot, r, k, 0).wait()
        return carry
    lax.fori_loop(0, tm, drain, 0, unroll=8)

    w0 = w_ref[:, 0:1]
    w1 = w_ref[:, 1:2]
    sq = jnp.zeros((tm, 1), jnp.float32)
    for c0 in range(0, d, COMBINE_COLS):
        cols = slice(c0, c0 + COMBINE_COLS)
        y = w0 * rows_ref[slot, 0, :, cols] + w1 * rows_ref[slot, 1, :, cols]
        x = x_ref[:, cols] + gate_ref[0, :, cols] * y
        o_ref[:, cols] = x
        if final_norm:
            sq = sq + jnp.sum(x * x, axis=-1, keepdims=True)
    if final_norm:
        inv = lax.rsqrt(sq * (1.0 / d) + NORM_EPS)
        for c0 in range(0, d, COMBINE_COLS):
            cols = slice(c0, c0 + COMBINE_COLS)
            o_ref[:, cols] = o_ref[:, cols] * inv * nw_ref[:, cols]


def moe_combine(x, ybuf, dest, weight, gates, gate_index, final_norm_w=None):
    m, d = dest.shape[0] // MOE_TOP_K, x.shape[1]
    tm = ROW_BLOCK
    dest_tiles = dest.reshape(MOE_TOP_K, m // tm, tm).transpose(1, 0, 2).reshape(m // tm, 1, MOE_TOP_K * tm)
    nw = jnp.ones((1, d), jnp.float32) if final_norm_w is None else final_norm_w.reshape(1, d).astype(jnp.float32)
    return pl.pallas_call(
        functools.partial(_combine_body, final_norm=final_norm_w is not None),
        out_shape=jax.ShapeDtypeStruct((m, d), jnp.float32),
        grid=(m // tm,),
        in_specs=[pl.BlockSpec((1, 1, MOE_TOP_K * tm), lambda i: (i, 0, 0), memory_space=pltpu.SMEM),
                  pl.BlockSpec((1, 1, MOE_TOP_K * tm), lambda i: (jnp.minimum(i + 1, m // tm - 1), 0, 0),
                               memory_space=pltpu.SMEM),
                  pl.BlockSpec((tm, d), lambda i: (i, 0)),
                  pl.BlockSpec(memory_space=pl.ANY),
                  pl.BlockSpec((tm, LANES), lambda i: (i, 0)),
                  pl.BlockSpec((1, 1, d), lambda i: (gate_index(i), 0, 0)),
                  pl.BlockSpec((1, d), lambda i: (0, 0))],
        out_specs=pl.BlockSpec((tm, d), lambda i: (i, 0)),
        scratch_shapes=[pltpu.VMEM((2, MOE_TOP_K, tm, d), jnp.float32), pltpu.SemaphoreType.DMA((2,))],
        compiler_params=_cparams("arbitrary"),
        name="moe_combine",
    )(dest_tiles, dest_tiles, x, ybuf, weight, gates, nw)


def kernel(x, c, ctx, c_ctx, mod_w, mod_b, norm1_w, norm2_w, ab_w_in, pool_w, pool_scale, gdn_conv_w,
           gdn_a_log, gdn_dt_bias, gdn_norm_w, ab_w_out, cd_w_in, fourier_w, gla_gate_up, gla_gate_b,
           gla_norm_w, cd_w_out, moe_group_w, moe_group_b, moe_expert_w, moe_expert_b, moe_w1, moe_w3,
           moe_w2, final_norm_w):
    assert DEPTH == 2
    n_batch, n_lat, d = x.shape
    n_ctx = ctx.shape[1]
    assert n_ctx == ROW_BLOCK and n_lat % DENSE_TM == 0 and (n_batch * n_ctx) % DENSE_TM == 0
    t_lat = n_batch * n_lat
    blocks_per_seq = n_lat // ROW_BLOCK
    n_lat_blocks = n_batch * blocks_per_seq
    xs = jnp.concatenate([x.reshape(t_lat, d), ctx.reshape(n_batch * n_ctx, d)], axis=0)

    def mod_row(rows_per_tile):
        tiles_per_seq = n_lat // rows_per_tile
        return lambda i: jnp.minimum(i // tiles_per_seq, n_batch)

    bf = jnp.bfloat16
    w1_all, w3_all, w2_all = (w.reshape((-1,) + w.shape[2:]) for w in (moe_w1, moe_w3, moe_w2))
    sh1, sc1, g1, sh2, sc2, g2 = modulation(c, c_ctx, mod_w, mod_b, 0)
    h = norm_modulate(xs, norm1_w[0], sh1, sc1, mod_row(NORM_TM))
    proj = dense_matmul([(h, _pad_cols(ab_w_in[0].astype(bf), IN_PROJ_TN))], tn=IN_PROJ_TN)
    gates, gates_t = gdn_gates(proj, AB_MAIN, gdn_a_log[0], gdn_dt_bias[0], GDN_HEADS)
    u, w, qd, ke, at, ge = gdn_prepare(proj, gates, gates_t, gdn_conv_w[0], GDN_HEADS, POOL_DIM, blocks_per_seq,
                                       n_lat_blocks)
    o_f, o_b = gdn_scan(u, w, qd, ke, at, ge, GDN_HEADS, n_batch, blocks_per_seq)
    y = gated_head_norm(o_f, o_b, proj, POOL_DIM + 3 * GDN_DIM, gdn_norm_w[0], GDN_HEAD_DIM,
                        pool_mix(proj, pool_w[0], pool_scale[0], n_lat_blocks))
    xs = dense_matmul([(y, ab_w_out[0].astype(bf))], residual=xs, gates=g1, gate_index=mod_row(DENSE_TM))
    f = norm_modulate(xs, norm2_w[0], sh2, sc2, mod_row(NORM_TM), packed=True)
    ybuf, dest, wgt = hier_moe(f, moe_group_w[0], moe_group_b[0], moe_expert_w[0], moe_expert_b[0], w1_all, w3_all,
                               w2_all, 0)
    xs = moe_combine(xs, ybuf, dest, wgt, g2, mod_row(ROW_BLOCK))

    sh1, sc1, g1, sh2, sc2, g2 = modulation(c, c_ctx, mod_w, mod_b, 1)
    h = norm_modulate(xs, norm1_w[1], sh1, sc1, mod_row(NORM_TM))
    proj = dense_matmul([(h, _pad_cols(cd_w_in[0].astype(bf), IN_PROJ_TN))], tn=IN_PROJ_TN)
    qd, ke, at, ge, v16 = gla_prepare(proj, gla_gate_up[0], gla_gate_b[0])
    o_f, o_b = gla_scan(qd, ke, at, ge, v16, n_batch, blocks_per_seq)
    y = gated_head_norm(o_f, o_b, proj, CD_SPLITS[3], gla_norm_w[0], GLA_HEAD_V,
                        fourier_mix(proj, fourier_w[0], n_batch, n_lat))
    xl = dense_matmul([(y, cd_w_out[0].astype(bf))], residual=xs, gates=g1, gate_index=mod_row(DENSE_TM))
    f = norm_modulate(xl, norm2_w[1], sh2, sc2, mod_row(NORM_TM), packed=True)
    ybuf, dest, wgt = hier_moe(f, moe_group_w[1], moe_group_b[1], moe_expert_w[1], moe_expert_b[1], w1_all, w3_all,
                               w2_all, MOE_EXPERTS)
    out = moe_combine(xl, ybuf, dest, wgt, g2, mod_row(ROW_BLOCK), final_norm_w=final_norm_w)
    return out.reshape(n_batch, n_lat, d)
```
